```python
import math
import jax, jax.numpy as jnp
from jax import lax
import numpy as np

D_MODEL = 1024
BATCH = 8
SEQ = 8192
DEPTH = 2
DEC_BATCH = 8
DEC_SEQ = 4096
PAST_LEN = 128

HEAD_DIM = 64
QBLOCK = 128
ROPE_THETA = 500000.0
ROT_DIV = 4
EPS = 1e-6
NEG = -1e30
N_BRANCH = 4
BRANCH_W = 256
MLA_HEADS = 4
MLA_Q_RANK = 256
MLA_KV_RANK = 128
MLA_NOPE = 64
MLA_ROPE = 32
MLA_V = 64
MLA_SCALE = (MLA_NOPE + MLA_ROPE) ** -0.5
DIL_GROUPS = ((128, 1), (512, 4), (2048, 16))
DIL_HEADS = 4
DIL_SCALE = HEAD_DIM ** -0.5
DIFF_HEADS = 4
DIFF_QK = 32
DIFF_V = 64
DIFF_SCALE = DIFF_QK ** -0.5
NA_HEADS = 4
NA_ROWS = 8
NA_COLS = 16
GRID_W = 64
NA_SCALE = HEAD_DIM ** -0.5
MEM_TOKENS = 256
MEM_HEADS = 4
MEM_HEAD_DIM = D_MODEL // MEM_HEADS
MEM_SCALE = MEM_HEAD_DIM ** -0.5
N_EXPERTS = 16
EXPERT_FF = 512
CAPACITY_FACTOR = 2

IN_WIDTHS = (MLA_Q_RANK, MLA_KV_RANK, MLA_ROPE,
             3 * len(DIL_GROUPS) * DIL_HEADS * HEAD_DIM,
             2 * DIFF_HEADS * DIFF_QK, 2 * DIFF_HEADS * DIFF_QK, DIFF_HEADS * DIFF_V,
             NA_HEADS * HEAD_DIM, NA_HEADS * HEAD_DIM, NA_HEADS * HEAD_DIM)
IN_COLS = sum(IN_WIDTHS)

kernel_name = 'hybrid_gated_encoder'


def rms_norm(x, g):
    x32 = x.astype(jnp.float32)
    y = x32 * lax.rsqrt(jnp.mean(x32 * x32, axis=-1, keepdims=True) + EPS)
    return (y * g.astype(jnp.float32)).astype(x.dtype)


def rope(x, rot_dim):
    seq = x.shape[-2]
    half = rot_dim // 2
    inv_freq = ROPE_THETA ** (-jnp.arange(half, dtype=jnp.float32) * (2.0 / rot_dim))
    ang = jnp.arange(seq, dtype=jnp.float32)[:, None] * inv_freq[None, :]
    cos = jnp.cos(ang).astype(x.dtype)
    sin = jnp.sin(ang).astype(x.dtype)
    x1 = x[..., :half]
    x2 = x[..., half:rot_dim]
    return jnp.concatenate([x1 * cos - x2 * sin, x2 * cos + x1 * sin, x[..., rot_dim:]], axis=-1)


def to_heads(x, n_heads):
    b, s, w = x.shape
    return x.reshape(b, s, n_heads, w // n_heads).transpose(0, 2, 1, 3)


def from_heads(o):
    b, h, s, d = o.shape
    return o.transpose(0, 2, 1, 3).reshape(b, s, h * d)


def to_blocks(x):
    b, h, s, d = x.shape
    return jnp.moveaxis(x.reshape(b, h, s // QBLOCK, QBLOCK, d), 2, 0)


def from_blocks(o):
    nb, b, h, qb, d = o.shape
    return jnp.moveaxis(o, 0, 2).reshape(b, h, nb * qb, d)


def dense_attention(q, k, v, scale):
    def body(qb):
        s = jnp.einsum('bhqd,bhkd->bhqk', qb, k).astype(jnp.float32) * scale
        p = jax.nn.softmax(s, axis=-1).astype(v.dtype)
        return jnp.einsum('bhqk,bhkd->bhqd', p, v)
    return from_blocks(lax.map(body, to_blocks(q)))


def diff_attention(q1, q2, k1, k2, v, lam):
    def body(qs):
        a, b = qs
        p1 = jax.nn.softmax(jnp.einsum('bhqd,bhkd->bhqk', a, k1).astype(jnp.float32) * DIFF_SCALE, axis=-1)
        p2 = jax.nn.softmax(jnp.einsum('bhqd,bhkd->bhqk', b, k2).astype(jnp.float32) * DIFF_SCALE, axis=-1)
        p = (p1 - lam * p2).astype(v.dtype)
        return jnp.einsum('bhqk,bhkd->bhqd', p, v)
    return from_blocks(lax.map(body, (to_blocks(q1), to_blocks(q2))))


def dilated_attention(q, k, v):
    b, g_n, h, s, d = q.shape
    nb = s // QBLOCK
    q_blocks = jnp.moveaxis(q.reshape(b, g_n, h, nb, QBLOCK, d), 3, 0)
    starts = jnp.arange(nb, dtype=jnp.int32) * QBLOCK

    def body(args):
        qb, q0 = args
        t = q0 + jnp.arange(QBLOCK, dtype=jnp.int32)
        outs, lses = [], []
        for g, (win, dil) in enumerate(DIL_GROUPS):
            n_side = win // (2 * dil)
            offs = dil * jnp.arange(-n_side, n_side + 1, dtype=jnp.int32)
            idx = t[:, None] + offs[None, :]
            valid = (idx >= 0) & (idx < s)
            idxc = jnp.clip(idx, 0, s - 1)
            kg = k[:, g][:, :, idxc]
            vg = v[:, g][:, :, idxc]
            sc = jnp.einsum('bhqd,bhqjd->bhqj', qb[:, g], kg).astype(jnp.float32) * DIL_SCALE
            sc = jnp.where(valid, sc, NEG)
            lse = jax.nn.logsumexp(sc, axis=-1, keepdims=True)
            p = jnp.exp(sc - lse).astype(v.dtype)
            outs.append(jnp.einsum('bhqj,bhqjd->bhqd', p, vg))
            lses.append(lse)
        w = jax.nn.softmax(jnp.stack(lses, axis=0), axis=0).astype(v.dtype)
        comb = w[0] * outs[0]
        for g in range(1, len(DIL_GROUPS)):
            comb = comb + w[g] * outs[g]
        return comb

    return from_blocks(lax.map(body, (q_blocks, starts)))


def neighbourhood_attention(q, k, v, bias_table):
    b, h, s, d = q.shape
    rows = s // GRID_W
    kr = min(NA_ROWS, rows)
    kc = NA_COLS
    qg = q.reshape(b, h, rows, GRID_W, d)
    kg = k.reshape(b, h, rows, GRID_W, d)
    vg = v.reshape(b, h, rows, GRID_W, d)
    cols = jnp.arange(GRID_W, dtype=jnp.int32)
    col_start = jnp.clip(cols - kc // 2, 0, GRID_W - kc)
    col_idx = col_start[:, None] + jnp.arange(kc, dtype=jnp.int32)[None, :]
    rel_col = col_idx - cols[:, None] + (NA_COLS - 1)

    def body(args):
        qr, r = args
        rs = jnp.clip(r - kr // 2, 0, rows - kr)
        krow = lax.dynamic_slice_in_dim(kg, rs, kr, axis=2)
        vrow = lax.dynamic_slice_in_dim(vg, rs, kr, axis=2)
        kn = krow[:, :, :, col_idx]
        vn = vrow[:, :, :, col_idx]
        rel_row = rs + jnp.arange(kr, dtype=jnp.int32) - r + (NA_ROWS - 1)
        bias = bias_table[:, rel_row[:, None, None], rel_col[None, :, :]]
        bias = jnp.transpose(bias, (0, 2, 1, 3)).astype(jnp.float32)
        sc = jnp.einsum('bhcd,bhicjd->bhcij', qr, kn).astype(jnp.float32) * NA_SCALE + bias[None]
        p = jax.nn.softmax(sc.reshape(b, h, GRID_W, kr * kc), axis=-1)
        p = p.reshape(b, h, GRID_W, kr, kc).astype(v.dtype)
        return jnp.einsum('bhcij,bhicjd->bhcd', p, vn)

    o = lax.map(body, (jnp.moveaxis(qg, 2, 0), jnp.arange(rows, dtype=jnp.int32)))
    return jnp.moveaxis(o, 0, 2).reshape(b, h, s, d)


def split_points():
    pts, acc = [], 0
    for w in IN_WIDTHS[:-1]:
        acc += w
        pts.append(acc)
    return pts


def token_mixer(h, layer, w_in, g_cq, w_uq, g_ckv, w_ukv, lam_q1, lam_k1, lam_q2, lam_k2,
                g_diff, na_bias, w_gate, w_branch, w_out):
    b, s, _ = h.shape
    z = h @ w_in
    c_q, c_kv, k_pe, qkv_b, q_c, k_c, v_c, q_d, k_d, v_d = jnp.split(z, split_points(), axis=-1)

    q_a = to_heads(rms_norm(c_q, g_cq) @ w_uq, MLA_HEADS)
    q_a = jnp.concatenate([q_a[..., :MLA_NOPE], rope(q_a[..., MLA_NOPE:], MLA_ROPE)], axis=-1)
    kv = to_heads(rms_norm(c_kv, g_ckv) @ w_ukv, MLA_HEADS)
    k_pe = rope(k_pe, MLA_ROPE)[:, None]
    k_a = jnp.concatenate([kv[..., :MLA_NOPE], jnp.broadcast_to(k_pe, (b, MLA_HEADS, s, MLA_ROPE))], axis=-1)
    o_a = dense_attention(q_a, k_a, kv[..., MLA_NOPE:], MLA_SCALE)

    n_g = len(DIL_GROUPS)
    qkv = qkv_b.reshape(b, s, 3, n_g, DIL_HEADS, HEAD_DIM).transpose(2, 0, 3, 4, 1, 5)
    o_b = dilated_attention(rope(qkv[0], HEAD_DIM // ROT_DIV), rope(qkv[1], HEAD_DIM // ROT_DIV), qkv[2])

    def pair(x):
        return x.reshape(b, s, DIFF_HEADS, 2, DIFF_QK).transpose(3, 0, 2, 1, 4)
    qc = rope(pair(q_c), DIFF_QK // ROT_DIV)
    kc = rope(pair(k_c), DIFF_QK // ROT_DIV)
    lam_init = 0.8 - 0.6 * math.exp(-0.3 * layer)
    lam = (jnp.exp(jnp.sum(lam_q1.astype(jnp.float32) * lam_k1.astype(jnp.float32)))
           - jnp.exp(jnp.sum(lam_q2.astype(jnp.float32) * lam_k2.astype(jnp.float32))) + lam_init)
    o_c = diff_attention(qc[0], qc[1], kc[0], kc[1], to_heads(v_c, DIFF_HEADS), lam)
    o_c = rms_norm(o_c, g_diff) * (1.0 - lam_init)

    o_d = neighbourhood_attention(to_heads(q_d, NA_HEADS), to_heads(k_d, NA_HEADS),
                                  to_heads(v_d, NA_HEADS), na_bias)

    merged = 0.0
    for i, o in enumerate((o_a, o_b, o_c, o_d)):
        gate = jax.nn.sigmoid(h @ w_gate[i])
        merged = merged + gate * (from_heads(o) @ w_branch[i])
    return merged @ w_out


def memory_cross_attention(h, mem, g_mem, w_q, w_k, w_v, w_o):
    b, s, d = h.shape
    m_len = mem.shape[1]
    q = (h @ w_q).reshape(b, s, MEM_HEADS, MEM_HEAD_DIM)
    m = rms_norm(mem, g_mem)
    k = (m @ w_k).reshape(b, m_len, MEM_HEADS, MEM_HEAD_DIM)
    v = (m @ w_v).reshape(b, m_len, MEM_HEADS, MEM_HEAD_DIM)
    sc = jnp.einsum('bshd,bmhd->bhsm', q, k).astype(jnp.float32) * MEM_SCALE
    p = jax.nn.softmax(sc, axis=-1).astype(v.dtype)
    o = jnp.einsum('bhsm,bmhd->bshd', p, v).reshape(b, s, d)
    return o @ w_o


def expert_choice_ffn(h, w_router, w_e_gate, w_e_up, w_e_down):
    b, s, d = h.shape
    t = b * s
    cap = CAPACITY_FACTOR * t // N_EXPERTS
    xt = h.reshape(t, d)
    aff = jax.nn.softmax((xt @ w_router).astype(jnp.float32), axis=-1)
    gate, idx = lax.top_k(aff.T, cap)
    xe = jnp.take(xt, idx, axis=0)
    hid = jax.nn.silu(jnp.einsum('ecd,edf->ecf', xe, w_e_gate)) * jnp.einsum('ecd,edf->ecf', xe, w_e_up)
    ye = jnp.einsum('ecf,efd->ecd', hid, w_e_down) * gate[..., None].astype(h.dtype)
    out = jnp.zeros((t, d), h.dtype).at[idx.reshape(-1)].add(ye.reshape(-1, d))
    return out.reshape(b, s, d)


def encoder_trunk(x, mem, w_in, g_mix, g_cq, w_uq, g_ckv, w_ukv, lam_q1, lam_k1, lam_q2, lam_k2,
                  g_diff, na_bias, w_gate, w_branch, w_out, g_cross, g_mem, w_cq, w_ck, w_cv, w_co,
                  g_ffn, w_router, w_e_gate, w_e_up, w_e_down, g_final):
    for l in range(DEPTH):
        h = rms_norm(x, g_mix[l])
        x = x + token_mixer(h, l, w_in[l], g_cq[l], w_uq[l], g_ckv[l], w_ukv[l], lam_q1[l], lam_k1[l],
                            lam_q2[l], lam_k2[l], g_diff[l], na_bias[l], w_gate[l], w_branch[l], w_out[l])
        h = rms_norm(x, g_cross[l])
        x = x + memory_cross_attention(h, mem, g_mem[l], w_cq[l], w_ck[l], w_cv[l], w_co[l])
        h = rms_norm(x, g_ffn[l])
        x = x + expert_choice_ffn(h, w_router[l], w_e_gate[l], w_e_up[l], w_e_down[l])
    return rms_norm(x, g_final)


def setup_inputs(seed: int = 0) -> dict:
    key = jax.random.key(seed)
    k = jax.random.split(key, 31)
    L, D = DEPTH, D_MODEL

    def nrm(i, shape, scale):
        return scale * jax.random.normal(k[i], shape, jnp.float32)

    def gain(i, shape):
        return 1.0 + 0.01 * jax.random.normal(k[i], shape, jnp.float32)

    return {
        'x_prompt': nrm(0, (BATCH, SEQ, D), 1.0),
        'x_sample': nrm(1, (DEC_BATCH, DEC_SEQ, D), 1.0),
        'mem_prompt': nrm(2, (BATCH, MEM_TOKENS, D), 1.0),
        'mem_sample': nrm(3, (DEC_BATCH, MEM_TOKENS, D), 1.0),
        'w_in': nrm(4, (L, D, IN_COLS), D ** -0.5),
        'g_mix': gain(5, (L, D)),
        'g_cq': gain(6, (L, MLA_Q_RANK)),
        'w_uq': nrm(7, (L, MLA_Q_RANK, MLA_HEADS * (MLA_NOPE + MLA_ROPE)), MLA_Q_RANK ** -0.5),
        'g_ckv': gain(8, (L, MLA_KV_RANK)),
        'w_ukv': nrm(9, (L, MLA_KV_RANK, MLA_HEADS * (MLA_NOPE + MLA_V)), MLA_KV_RANK ** -0.5),
        'lam_q1': nrm(10, (L, DIFF_QK), 0.1),
        'lam_k1': nrm(11, (L, DIFF_QK), 0.1),
        'lam_q2': nrm(12, (L, DIFF_QK), 0.1),
        'lam_k2': nrm(13, (L, DIFF_QK), 0.1),
        'g_diff': gain(14, (L, DIFF_V)),
        'na_bias': nrm(15, (L, NA_HEADS, 2 * NA_ROWS - 1, 2 * NA_COLS - 1), 0.02),
        'w_gate': nrm(16, (L, N_BRANCH, D, D), D ** -0.5),
        'w_branch': nrm(17, (L, N_BRANCH, BRANCH_W, D), BRANCH_W ** -0.5),
        'w_out': nrm(18, (L, D, D), D ** -0.5),
        'g_cross': gain(19, (L, D)),
        'g_mem': gain(20, (L, D)),
        'w_cq': nrm(21, (L, D, D), D ** -0.5),
        'w_ck': nrm(22, (L, D, D), D ** -0.5),
        'w_cv': nrm(23, (L, D, D), D ** -0.5),
        'w_co': nrm(24, (L, D, D), D ** -0.5),
        'g_ffn': gain(25, (L, D)),
        'w_router': nrm(26, (L, D, N_EXPERTS), D ** -0.5),
        'w_e_gate': nrm(27, (L, N_EXPERTS, D, EXPERT_FF), D ** -0.5),
        'w_e_up': nrm(28, (L, N_EXPERTS, D, EXPERT_FF), D ** -0.5),
        'w_e_down': nrm(29, (L, N_EXPERTS, EXPERT_FF, D), EXPERT_FF ** -0.5),
        'g_final': gain(30, (D,)),
    }


def reference(x_prompt, x_sample, mem_prompt, mem_sample, w_in, g_mix, g_cq, w_uq, g_ckv, w_ukv,
              lam_q1, lam_k1, lam_q2, lam_k2, g_diff, na_bias, w_gate, w_branch, w_out, g_cross, g_mem,
              w_cq, w_ck, w_cv, w_co, g_ffn, w_router, w_e_gate, w_e_up, w_e_down, g_final):
    y_prompt = encoder_trunk(x_prompt, mem_prompt, w_in, g_mix, g_cq, w_uq, g_ckv, w_ukv, lam_q1, lam_k1,
                             lam_q2, lam_k2, g_diff, na_bias, w_gate, w_branch, w_out, g_cross, g_mem,
                             w_cq, w_ck, w_cv, w_co, g_ffn, w_router, w_e_gate, w_e_up, w_e_down, g_final)
    y_sample = encoder_trunk(x_sample, mem_sample, w_in, g_mix, g_cq, w_uq, g_ckv, w_ukv, lam_q1, lam_k1,
                             lam_q2, lam_k2, g_diff, na_bias, w_gate, w_branch, w_out, g_cross, g_mem,
                             w_cq, w_ck, w_cv, w_co, g_ffn, w_router, w_e_gate, w_e_up, w_e_down, g_final)
    return (y_prompt, y_sample)
```

```python
import functools
import math

import jax
import jax.numpy as jnp
from jax import lax
from jax.experimental import pallas as pl
from jax.experimental.pallas import tpu as pltpu

BF16 = jnp.bfloat16
F32 = jnp.float32
I32 = jnp.int32

D_MODEL = 1024
EPS = 1e-6
NEG = -1e30
ROPE_THETA = 500000.0
HEAD_DIM = 64
MLA_HEADS = 4
MLA_Q_RANK = 256
MLA_KV_RANK = 128
MLA_NOPE = 64
MLA_ROPE = 32
MLA_SCALE = (MLA_NOPE + MLA_ROPE) ** -0.5
DIL_GROUPS = ((128, 1), (512, 4), (2048, 16))
DIL_SIDE = 64
DIL_SCALE = HEAD_DIM ** -0.5
DIFF_QK = 32
DIFF_SCALE = DIFF_QK ** -0.5
NA_ROWS = 8
NA_COLS = 16
GRID_W = 64
NA_SCALE = HEAD_DIM ** -0.5
MEM_HEADS = 4
MEM_HEAD_DIM = D_MODEL // MEM_HEADS
MEM_SCALE = MEM_HEAD_DIM ** -0.5
N_EXPERTS = 16
EXPERT_FF = 512
CAPACITY_FACTOR = 2

LANES = 128
VMEM_LIMIT = 56 * 1024 * 1024

Z_CQ = 0
Z_CKV = 256
Z_KPE = 384
Z_HEAD = 512
Z_QB = 512
Z_KB = 1280
Z_VB = 2048
Z_QC = 2816
Z_KC = 3072
Z_VC = 3328
Z_QD = 3584
Z_KD = 3840
Z_VD = 4096
Z_COLS = 4352
ZR_COLS = Z_COLS - Z_HEAD

TM = 512
TQ = 256
TK = 512
TU = 128
NA_TQ = 128
NA_WIN_ROWS = 10
NA_BIAS_OFF = 2
NA_BIAS_N = 18
RB = 256
CH = 512
SEL_CH = 512
CNT_CH = 4096


def _params(n_grid):
    return pltpu.CompilerParams(dimension_semantics=("arbitrary",) * n_grid,
                                vmem_limit_bytes=VMEM_LIMIT)


def _rms(x, g):
    return x * lax.rsqrt(jnp.mean(x * x, axis=-1, keepdims=True) + EPS) * g


def _dot(a, b):
    return jnp.dot(a, b, preferred_element_type=F32)


def _dot_nt(a, b):
    return lax.dot_general(a, b, (((1,), (1,)), ((), ())), preferred_element_type=F32)


def _dot_tn(a, b):
    return lax.dot_general(a, b, (((0,), (0,)), ((), ())), preferred_element_type=F32)


def _rope_block(x, c, sa, sb, half):
    return x * c + pltpu.roll(x, LANES - half, 1) * sa + pltpu.roll(x, half, 1) * sb


def _proj_kernel(x_ref, g_ref, w_ref, rope_ref, gcq_ref, gckv_ref, wuq_ref, wk_ref, wv_ref,
                 zr_ref, qa_ref, ka_ref, va_ref):
    hb = _rms(x_ref[...], g_ref[...]).astype(BF16)

    def rope(x, kind, half):
        return _rope_block(x, rope_ref[3 * kind], rope_ref[3 * kind + 1], rope_ref[3 * kind + 2], half)

    z0 = _dot(hb, w_ref[:, 0:Z_HEAD])
    cq = _rms(z0[:, Z_CQ:Z_CQ + MLA_Q_RANK], gcq_ref[...]).astype(BF16)
    ckv = _rms(z0[:, Z_CKV:Z_CKV + MLA_KV_RANK], gckv_ref[...]).astype(BF16)
    kpe = rope(z0[:, Z_KPE:Z_KPE + LANES], 0, MLA_ROPE // 2)
    qa = _dot(cq, wuq_ref[...]) * MLA_SCALE
    kn = _dot(ckv, wk_ref[...])
    for h in range(MLA_HEADS):
        sl = slice(h * LANES, (h + 1) * LANES)
        qa_ref[0, h] = rope(qa[:, sl], 0, MLA_ROPE // 2).astype(BF16)
        ka_ref[0, h] = (kn[:, sl] + kpe).astype(BF16)
    va_ref[0] = _dot(ckv, wv_ref[...]).astype(BF16)

    def seg(start, scale, kind, half):
        z = _dot(hb, w_ref[:, start:start + 256])
        if scale is not None:
            z = z * scale
        for j in range(2):
            zb = z[:, j * LANES:(j + 1) * LANES]
            if kind is not None:
                zb = rope(zb, kind, half)
            o = start - Z_HEAD + j * LANES
            zr_ref[:, o:o + LANES] = zb.astype(BF16)

    dil_half = HEAD_DIM // 4 // 2
    diff_half = DIFF_QK // 4 // 2
    for s in range(Z_QB, Z_KB, 256):
        seg(s, DIL_SCALE, 1, dil_half)
    for s in range(Z_KB, Z_VB, 256):
        seg(s, None, 1, dil_half)
    for s in range(Z_VB, Z_QC, 256):
        seg(s, None, None, 0)
    seg(Z_QC, DIFF_SCALE, 2, diff_half)
    seg(Z_KC, None, 2, diff_half)
    seg(Z_VC, None, None, 0)
    seg(Z_QD, NA_SCALE, None, 0)
    seg(Z_KD, None, None, 0)
    seg(Z_VD, None, None, 0)


def _const_spec(shape):
    nd = len(shape)
    return pl.BlockSpec(shape, lambda *_: (0,) * nd)


def _project(x2d, b, s, g_mix, w_z, rope_tab, g_cq, g_ckv, w_uq, w_k, w_v):
    t = b * s
    n_s = s // TM
    return pl.pallas_call(
        _proj_kernel,
        grid=(t // TM,),
        in_specs=[
            pl.BlockSpec((TM, D_MODEL), lambda i: (i, 0)),
            _const_spec((1, D_MODEL)),
            _const_spec((D_MODEL, Z_COLS)),
            pl.BlockSpec((9, TM, LANES), lambda i: (0, i % n_s, 0)),
            _const_spec((1, MLA_Q_RANK)),
            _const_spec((1, MLA_KV_RANK)),
            _const_spec((MLA_Q_RANK, MLA_HEADS * LANES)),
            _const_spec((MLA_KV_RANK, MLA_HEADS * LANES)),
            _const_spec((MLA_KV_RANK, MLA_HEADS * HEAD_DIM)),
        ],
        out_specs=[
            pl.BlockSpec((TM, ZR_COLS), lambda i: (i, 0)),
            pl.BlockSpec((1, MLA_HEADS, TM, LANES), lambda i: (i // n_s, 0, i % n_s, 0)),
            pl.BlockSpec((1, MLA_HEADS, TM, LANES), lambda i: (i // n_s, 0, i % n_s, 0)),
            pl.BlockSpec((1, TM, MLA_HEADS * HEAD_DIM), lambda i: (i // n_s, i % n_s, 0)),
        ],
        out_shape=[
            jax.ShapeDtypeStruct((t, ZR_COLS), BF16),
            jax.ShapeDtypeStruct((b, MLA_HEADS, s, LANES), BF16),
            jax.ShapeDtypeStruct((b, MLA_HEADS, s, LANES), BF16),
            jax.ShapeDtypeStruct((b, s, MLA_HEADS * HEAD_DIM), BF16),
        ],
        compiler_params=_params(1),
        name="proj",
    )(x2d, g_mix, w_z, rope_tab, g_cq, g_ckv, w_uq, w_k, w_v)


def _flash(q, k_at, v_at, n_chunks):
    tq = q.shape[0]

    def body(c, carry):
        m, l, acc = carry
        s = _dot_nt(q, k_at(c))
        m_new = jnp.maximum(m, jnp.max(s, axis=-1, keepdims=True))
        alpha = jnp.exp(m - m_new)
        p = jnp.exp(s - m_new)
        l = alpha * l + jnp.sum(p, axis=-1, keepdims=True)
        acc = alpha * acc + _dot(p.astype(BF16), v_at(c))
        return m_new, l, acc

    init = (jnp.full((tq, 1), -jnp.inf, F32), jnp.zeros((tq, 1), F32), jnp.zeros((tq, LANES), F32))
    _, l, acc = lax.fori_loop(0, n_chunks, body, init)
    return acc, l


def _mla_kernel(q_ref, k_ref, v_ref, o_ref, *, n_chunks):
    lane = lax.broadcasted_iota(I32, (TQ, LANES), 1)
    out = None
    for hh in range(2):
        acc, l = _flash(
            q_ref[0, hh],
            lambda c: k_ref[0, hh, pl.ds(pl.multiple_of(c * TK, TK), TK), :],
            lambda c: v_ref[0, pl.ds(pl.multiple_of(c * TK, TK), TK), :],
            n_chunks)
        o = acc / l
        out = o if hh == 0 else jnp.where(lane < HEAD_DIM, out, o)
    o_ref[0] = out.astype(BF16)


def _mla_attention(qa, ka, va):
    b, _, s, _ = qa.shape
    return pl.pallas_call(
        functools.partial(_mla_kernel, n_chunks=s // TK),
        grid=(b, 2, s // TQ),
        in_specs=[
            pl.BlockSpec((1, 2, TQ, LANES), lambda bi, p, qi: (bi, p, qi, 0)),
            pl.BlockSpec((1, 2, s, LANES), lambda bi, p, qi: (bi, p, 0, 0)),
            pl.BlockSpec((1, s, LANES), lambda bi, p, qi: (bi, 0, p)),
        ],
        out_specs=pl.BlockSpec((1, TQ, LANES), lambda bi, p, qi: (bi, qi, p)),
        out_shape=jax.ShapeDtypeStruct((b, s, 2 * LANES), BF16),
        compiler_params=_params(3),
        name="mla_attn",
    )(qa, ka, va)


def _diff_kernel(q_ref, k_ref, v_ref, lam_ref, g_ref, o_ref, *, n_chunks, lam_init):
    lane = lax.broadcasted_iota(I32, (TQ, LANES), 1)
    lv = lam_ref[...]
    lam = (jnp.exp(jnp.sum(lv[0:1] * lv[1:2], axis=-1, keepdims=True))
           - jnp.exp(jnp.sum(lv[2:3] * lv[3:4], axis=-1, keepdims=True)) + lam_init)
    q = q_ref[0]
    k_at = lambda c: k_ref[0, pl.ds(pl.multiple_of(c * TK, TK), TK), :]
    v_at = lambda c: v_ref[0, pl.ds(pl.multiple_of(c * TK, TK), TK), :]
    out = None
    for hh in range(2):
        parts = []
        for comp in range(2):
            lo = (2 * hh + comp) * DIFF_QK
            qm = jnp.where((lane >= lo) & (lane < lo + DIFF_QK), q, jnp.zeros_like(q))
            acc, l = _flash(qm, k_at, v_at, n_chunks)
            parts.append(acc / l)
        o = parts[0] - lam * parts[1]
        in_head = (lane >= hh * HEAD_DIM) & (lane < (hh + 1) * HEAD_DIM)
        ms = jnp.sum(jnp.where(in_head, o * o, 0.0), axis=-1, keepdims=True) * (1.0 / HEAD_DIM)
        o = o * lax.rsqrt(ms + EPS) * g_ref[...] * (1.0 - lam_init)
        out = o if hh == 0 else jnp.where(lane < HEAD_DIM, out, o)
    o_ref[0] = out.astype(BF16)


def _diff_attention(zr3, lam_vecs, g_diff2, lam_init):
    b, s, _ = zr3.shape
    qb, kb, vb = (Z_QC - Z_HEAD) // LANES, (Z_KC - Z_HEAD) // LANES, (Z_VC - Z_HEAD) // LANES
    return pl.pallas_call(
        functools.partial(_diff_kernel, n_chunks=s // TK, lam_init=lam_init),
        grid=(b, 2, s // TQ),
        in_specs=[
            pl.BlockSpec((1, TQ, LANES), lambda bi, p, qi: (bi, qi, qb + p)),
            pl.BlockSpec((1, s, LANES), lambda bi, p, qi: (bi, 0, kb + p)),
            pl.BlockSpec((1, s, LANES), lambda bi, p, qi: (bi, 0, vb + p)),
            _const_spec((4, DIFF_QK)),
            _const_spec((1, LANES)),
        ],
        out_specs=pl.BlockSpec((1, TQ, LANES), lambda bi, p, qi: (bi, qi, p)),
        out_shape=jax.ShapeDtypeStruct((b, s, 2 * LANES), BF16),
        compiler_params=_params(3),
        name="diff_attn",
    )(zr3, zr3, zr3, lam_vecs, g_diff2)


def _dil_kernel(q_ref, k_ref, v_ref, o_ref, lse_ref, *, u_len, win):
    u0 = pl.program_id(2) * TU
    start = pl.multiple_of(jnp.clip(u0 - DIL_SIDE, 0, u_len - win), DIL_SIDE)
    kw = k_ref[0, pl.ds(start, win), :]
    vw = v_ref[0, pl.ds(start, win), :]
    q = q_ref[0]
    qpos = u0 + lax.broadcasted_iota(I32, (TU, win), 0)
    kpos = start + lax.broadcasted_iota(I32, (TU, win), 1)
    valid = jnp.abs(kpos - qpos) <= DIL_SIDE
    lane = lax.broadcasted_iota(I32, (TU, 4 * HEAD_DIM), 1)
    out = jnp.zeros((TU, 4 * HEAD_DIM), F32)
    lse_out = jnp.zeros((TU, 4 * HEAD_DIM), F32)
    for h in range(4):
        in_head = (lane >= h * HEAD_DIM) & (lane < (h + 1) * HEAD_DIM)
        qm = jnp.where(in_head, q, jnp.zeros_like(q))
        s = jnp.where(valid, _dot_nt(qm, kw), NEG)
        m = jnp.max(s, axis=-1, keepdims=True)
        e = jnp.exp(s - m)
        l = jnp.sum(e, axis=-1, keepdims=True)
        o = _dot((e / l).astype(BF16), vw)
        out = jnp.where(in_head, o, out)
        lse_out = jnp.where(in_head, m + jnp.log(l), lse_out)
    o_ref[0] = out.astype(BF16)
    lse_ref[0] = lse_out


def _dil_attention(zr3, g, dil):
    b, s, _ = zr3.shape
    u_len = s // dil
    win = TU + 2 * DIL_SIDE
    assert u_len >= win and u_len % TU == 0
    zv = zr3.reshape(b, u_len, dil * ZR_COLS)
    units = ZR_COLS // 256
    qu, ku, vu = (Z_QB - Z_HEAD) // 256 + g, (Z_KB - Z_HEAD) // 256 + g, (Z_VB - Z_HEAD) // 256 + g
    o, lse = pl.pallas_call(
        functools.partial(_dil_kernel, u_len=u_len, win=win),
        grid=(b, dil, u_len // TU),
        in_specs=[
            pl.BlockSpec((1, TU, 256), lambda bi, r, ui: (bi, ui, r * units + qu)),
            pl.BlockSpec((1, u_len, 256), lambda bi, r, ui: (bi, 0, r * units + ku)),
            pl.BlockSpec((1, u_len, 256), lambda bi, r, ui: (bi, 0, r * units + vu)),
        ],
        out_specs=[
            pl.BlockSpec((1, TU, 256), lambda bi, r, ui: (bi, ui, r)),
            pl.BlockSpec((1, TU, 256), lambda bi, r, ui: (bi, ui, r)),
        ],
        out_shape=[
            jax.ShapeDtypeStruct((b, u_len, dil * 256), BF16),
            jax.ShapeDtypeStruct((b, u_len, dil * 256), F32),
        ],
        compiler_params=_params(3),
        name=f"dil_attn_{dil}",
    )(zv, zv, zv)
    return o.reshape(b * s, 256), lse.reshape(b * s, 256)


def _na_bias_kernel(tbl_ref, o_ref):
    h = pl.program_id(0)
    d = pl.program_id(1)
    n_rel_r, n_rel_c = 2 * NA_ROWS - 1, 2 * NA_COLS - 1
    dl = jnp.clip(d - NA_BIAS_OFF, 0, n_rel_r - 1)
    dr = jnp.clip(d - NA_BIAS_OFF + 1, 0, n_rel_r - 1)
    qc = lax.broadcasted_iota(I32, (GRID_W, LANES), 0)
    ln = lax.broadcasted_iota(I32, (GRID_W, LANES), 1)
    left = ln < GRID_W
    rel = jnp.where(left, ln, ln - GRID_W) - qc + (NA_COLS - 1)
    acc = jnp.zeros((GRID_W, LANES), F32)
    for dd in range(n_rel_c):
        vl = tbl_ref[(h * n_rel_r + dl) * n_rel_c + dd]
        vr = tbl_ref[(h * n_rel_r + dr) * n_rel_c + dd]
        acc = jnp.where(rel == dd, jnp.where(left, vl, vr), acc)
    o_ref[0, 0] = acc


def _na_bias_tiles(na_bias_l):
    n_h = na_bias_l.shape[0]
    return pl.pallas_call(
        _na_bias_kernel,
        grid=(n_h, NA_BIAS_N),
        in_specs=[pl.BlockSpec(memory_space=pltpu.SMEM)],
        out_specs=pl.BlockSpec((1, 1, GRID_W, LANES), lambda h, d: (h, d, 0, 0)),
        out_shape=jax.ShapeDtypeStruct((n_h, NA_BIAS_N, GRID_W, LANES), F32),
        compiler_params=_params(2),
        name="na_bias",
    )(na_bias_l.reshape(-1))


def _na_kernel(q_ref, k_ref, v_ref, bias_ref, o_ref, *, n_rows):
    r0 = pl.program_id(1) * 2
    ws = jnp.clip(r0 - NA_ROWS // 2, 0, n_rows - NA_WIN_ROWS)
    n_keys = NA_WIN_ROWS * GRID_W
    start = pl.multiple_of(ws * GRID_W, GRID_W)
    kw = k_ref[0, pl.ds(start, n_keys), :]
    vw = v_ref[0, pl.ds(start, n_keys), :]
    q = q_ref[0]
    qi = lax.broadcasted_iota(I32, (NA_TQ, n_keys), 0)
    kj = lax.broadcasted_iota(I32, (NA_TQ, n_keys), 1)
    qr, qc = r0 + qi // GRID_W, qi % GRID_W
    kr, kc = ws + kj // GRID_W, kj % GRID_W
    rs = jnp.clip(qr - NA_ROWS // 2, 0, n_rows - NA_ROWS)
    cs = jnp.clip(qc - NA_COLS // 2, 0, GRID_W - NA_COLS)
    valid = (kr >= rs) & (kr < rs + NA_ROWS) & (kc >= cs) & (kc < cs + NA_COLS)
    lane = lax.broadcasted_iota(I32, (NA_TQ, 4 * HEAD_DIM), 1)
    out = jnp.zeros((NA_TQ, 4 * HEAD_DIM), F32)
    for h in range(4):
        halves = []
        for half in range(2):
            d0 = ws - r0 - half + (NA_ROWS - 1) + NA_BIAS_OFF
            halves.append(jnp.concatenate(
                [bias_ref[h, pl.ds(d0 + 2 * m, 1)][0] for m in range(NA_WIN_ROWS // 2)], axis=1))
        bias = jnp.concatenate(halves, axis=0)
        in_head = (lane >= h * HEAD_DIM) & (lane < (h + 1) * HEAD_DIM)
        qm = jnp.where(in_head, q, jnp.zeros_like(q))
        s = jnp.where(valid, _dot_nt(qm, kw) + bias, NEG)
        m = jnp.max(s, axis=-1, keepdims=True)
        e = jnp.exp(s - m)
        l = jnp.sum(e, axis=-1, keepdims=True)
        o = _dot((e / l).astype(BF16), vw)
        out = jnp.where(in_head, o, out)
    o_ref[0] = out.astype(BF16)


def _na_attention(zr3, bias_tiles):
    b, s, _ = zr3.shape
    n_rows = s // GRID_W
    assert n_rows >= NA_WIN_ROWS
    qu, ku, vu = (Z_QD - Z_HEAD) // 256, (Z_KD - Z_HEAD) // 256, (Z_VD - Z_HEAD) // 256
    return pl.pallas_call(
        functools.partial(_na_kernel, n_rows=n_rows),
        grid=(b, s // NA_TQ),
        in_specs=[
            pl.BlockSpec((1, NA_TQ, 256), lambda bi, i: (bi, i, qu)),
            pl.BlockSpec((1, s, 256), lambda bi, i: (bi, 0, ku)),
            pl.BlockSpec((1, s, 256), lambda bi, i: (bi, 0, vu)),
            _const_spec(bias_tiles.shape),
        ],
        out_specs=pl.BlockSpec((1, NA_TQ, 256), lambda bi, i: (bi, i, 0)),
        out_shape=jax.ShapeDtypeStruct((b, s, 256), BF16),
        compiler_params=_params(2),
        name="na_attn",
    )(zr3, zr3, zr3, bias_tiles)


def _merge_kernel(x_ref, g_ref, oa_ref, ob0_ref, ob1_ref, ob2_ref, l0_ref, l1_ref, l2_ref, oc_ref, od_ref,
                  wg_ref, wb_ref, wo_ref, y_ref):
    x = x_ref[...]
    hb = _rms(x, g_ref[...]).astype(BF16)
    lses = (l0_ref[...], l1_ref[...], l2_ref[...])
    mx = jnp.maximum(jnp.maximum(lses[0], lses[1]), lses[2])
    ws = [jnp.exp(l - mx) for l in lses]
    den = ws[0] + ws[1] + ws[2]
    obs = (ob0_ref, ob1_ref, ob2_ref)
    ob = (ws[0] / den) * obs[0][...].astype(F32)
    for g in range(1, 3):
        ob = ob + (ws[g] / den) * obs[g][...].astype(F32)
    branches = (oa_ref[...], ob.astype(BF16), oc_ref[...], od_ref[...])
    merged = None
    for i, o in enumerate(branches):
        gate = jax.nn.sigmoid(_dot(hb, wg_ref[i]))
        term = gate * _dot(o, wb_ref[i])
        merged = term if merged is None else merged + term
    y_ref[...] = x + _dot(merged.astype(BF16), wo_ref[...])


def _merge(x2d, g_mix, oa, obs, lses, oc, od, w_gate, w_branch, w_out):
    t = x2d.shape[0]
    tok = lambda w: pl.BlockSpec((TM, w), lambda i: (i, 0))
    return pl.pallas_call(
        _merge_kernel,
        grid=(t // TM,),
        in_specs=[tok(D_MODEL), _const_spec((1, D_MODEL))] + [tok(256)] * 9 + [
            _const_spec(w_gate.shape), _const_spec(w_branch.shape), _const_spec(w_out.shape)],
        out_specs=tok(D_MODEL),
        out_shape=jax.ShapeDtypeStruct((t, D_MODEL), F32),
        compiler_params=_params(1),
        name="merge",
    )(x2d, g_mix, oa, obs[0], obs[1], obs[2], lses[0], lses[1], lses[2], oc, od, w_gate, w_branch, w_out)


def _memkv_kernel(m_ref, g_ref, wk_ref, wv_ref, k_ref, v_ref):
    mb = _rms(m_ref[0], g_ref[...]).astype(BF16)
    k_ref[0] = _dot(mb, wk_ref[...]).astype(BF16)
    v_ref[0] = _dot(mb, wv_ref[...]).astype(BF16)


def _mem_kv(mem, g_mem, w_ck, w_cv):
    b, m_len, d = mem.shape
    blk = pl.BlockSpec((1, m_len, d), lambda bi: (bi, 0, 0))
    return pl.pallas_call(
        _memkv_kernel,
        grid=(b,),
        in_specs=[blk, _const_spec((1, d)), _const_spec((d, d)), _const_spec((d, d))],
        out_specs=[blk, blk],
        out_shape=[jax.ShapeDtypeStruct((b, m_len, d), BF16)] * 2,
        compiler_params=_params(1),
        name="mem_kv",
    )(mem, g_mem, w_ck, w_cv)


def _cross_kernel(x_ref, g_ref, wq_ref, k_ref, v_ref, wo_ref, gf_ref, wr_ref, y_ref, h_ref, aff_ref):
    x = x_ref[...]
    hb = _rms(x, g_ref[...]).astype(BF16)
    q = (_dot(hb, wq_ref[...]) * MEM_SCALE).astype(BF16)
    outs = []
    for h in range(MEM_HEADS):
        sl = slice(h * MEM_HEAD_DIM, (h + 1) * MEM_HEAD_DIM)
        s = _dot_nt(q[:, sl], k_ref[0, :, sl])
        m = jnp.max(s, axis=-1, keepdims=True)
        e = jnp.exp(s - m)
        p = e / jnp.sum(e, axis=-1, keepdims=True)
        outs.append(_dot(p.astype(BF16), v_ref[0, :, sl]).astype(BF16))
    y = x + _dot(jnp.concatenate(outs, axis=1), wo_ref[...])
    y_ref[...] = y
    h3 = _rms(y, gf_ref[...]).astype(BF16)
    h_ref[...] = h3
    logits = _dot_nt(wr_ref[...], h3)
    m = jnp.max(logits, axis=0, keepdims=True)
    e = jnp.exp(logits - m)
    aff_ref[...] = e / jnp.sum(e, axis=0, keepdims=True)


def _cross(x2d, b, s, g_cross, w_cq, k_mem, v_mem, w_co, g_ffn, w_router_t):
    t = b * s
    n_s = s // TM
    m_len = k_mem.shape[1]
    tok = lambda: pl.BlockSpec((TM, D_MODEL), lambda i: (i, 0))
    mem_blk = pl.BlockSpec((1, m_len, D_MODEL), lambda i: (i // n_s, 0, 0))
    return pl.pallas_call(
        _cross_kernel,
        grid=(t // TM,),
        in_specs=[tok(), _const_spec((1, D_MODEL)), _const_spec((D_MODEL, D_MODEL)), mem_blk, mem_blk,
                  _const_spec((D_MODEL, D_MODEL)), _const_spec((1, D_MODEL)),
                  _const_spec((N_EXPERTS, D_MODEL))],
        out_specs=[tok(), tok(), pl.BlockSpec((N_EXPERTS, TM), lambda i: (0, i))],
        out_shape=[jax.ShapeDtypeStruct((t, D_MODEL), F32), jax.ShapeDtypeStruct((t, D_MODEL), BF16),
                   jax.ShapeDtypeStruct((N_EXPERTS, t), F32)],
        compiler_params=_params(1),
        name="cross",
    )(x2d, g_cross, w_cq, k_mem, v_mem, w_co, g_ffn, w_router_t)


def _select_kernel(aff_ref, pos_ref, *, n_tok, cap):
    n_cnt = n_tok // CNT_CH

    def count_ge(cand):
        def body(c, acc):
            bits = pltpu.bitcast(aff_ref[:, pl.ds(pl.multiple_of(c * CNT_CH, CNT_CH), CNT_CH)], I32)
            return acc + jnp.sum(jnp.where(bits >= cand, 1.0, 0.0), axis=1, keepdims=True)
        return lax.fori_loop(0, n_cnt, body, jnp.zeros((N_EXPERTS, 1), F32))

    def bit_body(i, prefix):
        cand = prefix | lax.shift_left(jnp.int32(1), 30 - i)
        return jnp.where(count_ge(cand) >= cap, cand, prefix)

    thr = lax.fori_loop(0, 31, bit_body, jnp.zeros((N_EXPERTS, 1), I32))
    need_eq = cap - count_ge(thr + 1)

    row = lax.broadcasted_iota(I32, (SEL_CH, SEL_CH), 0)
    col = lax.broadcasted_iota(I32, (SEL_CH, SEL_CH), 1)
    tri = jnp.where(row <= col, 1.0, 0.0).astype(BF16)

    def chunk_body(c, carry):
        c_eq, c_sel = carry
        sl = pl.ds(pl.multiple_of(c * SEL_CH, SEL_CH), SEL_CH)
        bits = pltpu.bitcast(aff_ref[:, sl], I32)
        eq = jnp.where(bits == thr, 1.0, 0.0)
        gt = jnp.where(bits > thr, 1.0, 0.0)
        eq_before = c_eq + _dot(eq.astype(BF16), tri) - eq
        sel = gt + eq * jnp.where(eq_before < need_eq, 1.0, 0.0)
        rank = c_sel + _dot(sel.astype(BF16), tri) - sel
        pos_ref[:, sl] = jnp.where(sel > 0.5, rank, -1.0).astype(I32)
        return (c_eq + jnp.sum(eq, axis=1, keepdims=True), c_sel + jnp.sum(sel, axis=1, keepdims=True))

    zero = jnp.zeros((N_EXPERTS, 1), F32)
    lax.fori_loop(0, n_tok // SEL_CH, chunk_body, (zero, zero))


def _select(aff_t, cap):
    n_tok = aff_t.shape[1]
    return pl.pallas_call(
        functools.partial(_select_kernel, n_tok=n_tok, cap=cap),
        grid=(1,),
        in_specs=[_const_spec(aff_t.shape)],
        out_specs=_const_spec(aff_t.shape),
        out_shape=jax.ShapeDtypeStruct(aff_t.shape, I32),
        compiler_params=_params(1),
        name="select",
    )(aff_t)


def _one_hot_rows(pos_row, j):
    rows = j * RB + lax.broadcasted_iota(I32, (RB, CH), 0)
    return pos_row == rows


def _ffn_kernel(ie_ref, ij_ref, ik_ref, first_ref, last_ref, valid_ref,
                h_ref, pos_ref, aff_ref, wg_ref, wu_ref, wd_ref, ye_ref, acc_ref, gate_ref):
    w = pl.program_id(0)

    @pl.when(first_ref[w] == 1)
    def _():
        acc_ref[...] = jnp.zeros_like(acc_ref)
        gate_ref[...] = jnp.zeros_like(gate_ref)

    @pl.when(valid_ref[w] == 1)
    def _():
        hit = _one_hot_rows(pos_ref[0], ij_ref[w])
        acc_ref[...] += _dot(jnp.where(hit, 1.0, 0.0).astype(BF16), h_ref[...])
        gate_ref[...] += jnp.sum(jnp.where(hit, aff_ref[0], 0.0), axis=1, keepdims=True)

    @pl.when(last_ref[w] == 1)
    def _():
        xe = acc_ref[...].astype(BF16)
        hid = jax.nn.silu(_dot(xe, wg_ref[0])) * _dot(xe, wu_ref[0])
        ye_ref[0] = (_dot(hid.astype(BF16), wd_ref[0]) * gate_ref[...]).astype(BF16)


def _expert_ffn(items, h3, pos3, aff3, w_e_gate, w_e_up, w_e_down, cap):
    n_items = items[0].shape[0]
    grid_spec = pltpu.PrefetchScalarGridSpec(
        num_scalar_prefetch=6,
        grid=(n_items,),
        in_specs=[
            pl.BlockSpec((CH, D_MODEL), lambda w, ie, ij, ik, *_: (ik[w], 0)),
            pl.BlockSpec((1, 1, CH), lambda w, ie, ij, ik, *_: (ie[w], 0, ik[w])),
            pl.BlockSpec((1, 1, CH), lambda w, ie, ij, ik, *_: (ie[w], 0, ik[w])),
            pl.BlockSpec((1, D_MODEL, EXPERT_FF), lambda w, ie, *_: (ie[w], 0, 0)),
            pl.BlockSpec((1, D_MODEL, EXPERT_FF), lambda w, ie, *_: (ie[w], 0, 0)),
            pl.BlockSpec((1, EXPERT_FF, D_MODEL), lambda w, ie, *_: (ie[w], 0, 0)),
        ],
        out_specs=pl.BlockSpec((1, RB, D_MODEL), lambda w, ie, ij, *_: (ie[w], ij[w], 0)),
        scratch_shapes=[pltpu.VMEM((RB, D_MODEL), F32), pltpu.VMEM((RB, 1), F32)],
    )
    return pl.pallas_call(
        _ffn_kernel,
        grid_spec=grid_spec,
        out_shape=jax.ShapeDtypeStruct((N_EXPERTS, cap, D_MODEL), BF16),
        compiler_params=_params(1),
        name="expert_ffn",
    )(*items, h3, pos3, aff3, w_e_gate, w_e_up, w_e_down)


def _scatter_kernel(ik_ref, ie_ref, ij_ref, first_ref, last_ref, valid_ref,
                    x_ref, pos_ref, ye_ref, gfin_ref, y_ref, *, final_norm):
    w = pl.program_id(0)

    @pl.when(first_ref[w] == 1)
    def _():
        y_ref[...] = x_ref[...]

    @pl.when(valid_ref[w] == 1)
    def _():
        hit = _one_hot_rows(pos_ref[0], ij_ref[w])
        y_ref[...] += _dot_tn(jnp.where(hit, 1.0, 0.0).astype(BF16), ye_ref[0])

    if final_norm:
        @pl.when(last_ref[w] == 1)
        def _():
            y_ref[...] = _rms(y_ref[...], gfin_ref[...])


def _expert_scatter(items, x2d, pos3, ye, g_final, final_norm):
    n_items = items[0].shape[0]
    t = x2d.shape[0]
    grid_spec = pltpu.PrefetchScalarGridSpec(
        num_scalar_prefetch=6,
        grid=(n_items,),
        in_specs=[
            pl.BlockSpec((CH, D_MODEL), lambda w, ik, *_: (ik[w], 0)),
            pl.BlockSpec((1, 1, CH), lambda w, ik, ie, *_: (ie[w], 0, ik[w])),
            pl.BlockSpec((1, RB, D_MODEL), lambda w, ik, ie, ij, *_: (ie[w], ij[w], 0)),
            pl.BlockSpec((1, D_MODEL), lambda w, *_: (0, 0)),
        ],
        out_specs=pl.BlockSpec((CH, D_MODEL), lambda w, ik, *_: (ik[w], 0)),
    )
    return pl.pallas_call(
        functools.partial(_scatter_kernel, final_norm=final_norm),
        grid_spec=grid_spec,
        out_shape=jax.ShapeDtypeStruct((t, D_MODEL), F32),
        compiler_params=_params(1),
        name="expert_scatter",
    )(*items, x2d, pos3, ye, g_final)


def _flatten_items(n_per_block, n_max):
    off_incl = jnp.cumsum(n_per_block)
    off_excl = off_incl - n_per_block
    total = off_incl[-1]
    w = jnp.arange(n_max, dtype=I32)
    valid = w < total
    wv = jnp.where(valid, w, total - 1)
    blk = jnp.searchsorted(off_incl, wv, side="right").astype(I32)
    return blk, wv - off_excl[blk], wv, valid, off_excl, off_incl


def _work_items(pos, cap):
    n_e, n_tok = pos.shape
    n_k, n_j = n_tok // CH, cap // RB
    cnt = jnp.sum((pos >= 0).reshape(n_e, n_k, CH), axis=-1, dtype=I32)
    ends = jnp.cumsum(cnt, axis=1)
    begins = ends - cnt

    jb = jnp.arange(n_j, dtype=I32) * RB
    k_first = jax.vmap(lambda r: jnp.searchsorted(r, jb, side="right"))(ends).astype(I32)
    k_last = jax.vmap(lambda r: jnp.searchsorted(r, jb + RB, side="left"))(ends).astype(I32)
    n_blk = (k_last - k_first + 1).reshape(-1)
    blk, off, wv, valid, off_excl, off_incl = _flatten_items(n_blk, n_e * (n_j + n_k))
    g_items = (blk // n_j, blk % n_j, k_first.reshape(-1)[blk] + off,
               (valid & (wv == off_excl[blk])).astype(I32),
               (valid & (wv == off_incl[blk] - 1)).astype(I32), valid.astype(I32))

    has = cnt > 0
    j_first = jnp.minimum(begins // RB, n_j - 1)
    j_last = jnp.where(has, (ends - 1) // RB, j_first)
    always = (jnp.arange(n_e) == 0)[:, None]
    n_ke = jnp.where(has | always, j_last - j_first + 1, 0).T.reshape(-1)
    blk, off, wv, valid, off_excl, off_incl = _flatten_items(n_ke, n_e * (n_j + n_k))
    k_of = blk // n_e
    chunk_first = off_excl[k_of * n_e]
    chunk_end = off_incl[k_of * n_e + n_e - 1]
    s_items = (k_of, blk % n_e, j_first.T.reshape(-1)[blk] + off,
               (valid & (wv == chunk_first)).astype(I32),
               (valid & (wv == chunk_end - 1)).astype(I32), valid.astype(I32))
    return tuple(a.astype(I32) for a in g_items), tuple(a.astype(I32) for a in s_items)


def _rope_tables(s):
    pos = jnp.arange(s, dtype=F32)[:, None]
    lane = jnp.arange(LANES)
    tabs = []
    for period, base, rot in ((LANES, MLA_NOPE, MLA_ROPE), (HEAD_DIM, 0, HEAD_DIM // 4), (DIFF_QK, 0, DIFF_QK // 4)):
        half = rot // 2
        rel = lane % period - base
        inv_freq = ROPE_THETA ** (-jnp.arange(half, dtype=F32) * (2.0 / rot))
        ang = pos * inv_freq[None, :]
        cos, sin = jnp.cos(ang), jnp.sin(ang)
        idx = jnp.clip(jnp.where(rel >= half, rel - half, rel), 0, half - 1)
        first = (rel >= 0) & (rel < half)
        second = (rel >= half) & (rel < rot)
        tabs.append(jnp.where(first | second, cos[:, idx], 1.0))
        tabs.append(jnp.where(first, -sin[:, idx], 0.0))
        tabs.append(jnp.where(second, sin[:, idx], 0.0))
    return jnp.stack(tabs).astype(F32)


def _prep_layer(l, w_in, g_mix, g_cq, w_uq, g_ckv, w_ukv, lam_q1, lam_k1, lam_q2, lam_k2, g_diff, na_bias,
                w_gate, w_branch, w_out, g_cross, g_mem, w_cq, w_ck, w_cv, w_co, g_ffn, w_router,
                w_e_gate, w_e_up, w_e_down):
    d = D_MODEL
    wi = w_in[l]
    kpe0 = MLA_Q_RANK + MLA_KV_RANK
    z64 = jnp.zeros((d, MLA_NOPE), F32)
    z32 = jnp.zeros((d, LANES - MLA_NOPE - MLA_ROPE), F32)
    w_z = jnp.concatenate([wi[:, :kpe0], z64, wi[:, kpe0:kpe0 + MLA_ROPE], z32, wi[:, kpe0 + MLA_ROPE:]], axis=1)
    qh = MLA_NOPE + MLA_ROPE
    wuq = w_uq[l].reshape(MLA_Q_RANK, MLA_HEADS, qh)
    wuq = jnp.pad(wuq, ((0, 0), (0, 0), (0, LANES - qh))).reshape(MLA_Q_RANK, MLA_HEADS * LANES)
    wukv = w_ukv[l].reshape(MLA_KV_RANK, MLA_HEADS, MLA_NOPE + HEAD_DIM)
    wk = jnp.pad(wukv[:, :, :MLA_NOPE], ((0, 0), (0, 0), (0, LANES - MLA_NOPE))).reshape(MLA_KV_RANK, MLA_HEADS * LANES)
    wv = wukv[:, :, MLA_NOPE:].reshape(MLA_KV_RANK, MLA_HEADS * HEAD_DIM)
    return dict(
        g_mix=g_mix[l][None], w_z=w_z.astype(BF16), g_cq=g_cq[l][None], g_ckv=g_ckv[l][None],
        w_uq=wuq.astype(BF16), w_k=wk.astype(BF16), w_v=wv.astype(BF16),
        lam=jnp.stack([lam_q1[l], lam_k1[l], lam_q2[l], lam_k2[l]]).astype(F32),
        g_diff=jnp.tile(g_diff[l], 2)[None], na_bias=na_bias[l],
        w_gate=w_gate[l].astype(BF16), w_branch=w_branch[l].astype(BF16), w_out=w_out[l].astype(BF16),
        g_cross=g_cross[l][None], g_mem=g_mem[l][None], w_cq=w_cq[l].astype(BF16), w_ck=w_ck[l].astype(BF16),
        w_cv=w_cv[l].astype(BF16), w_co=w_co[l].astype(BF16), g_ffn=g_ffn[l][None],
        w_router_t=w_router[l].T.astype(BF16), w_e_gate=w_e_gate[l].astype(BF16),
        w_e_up=w_e_up[l].astype(BF16), w_e_down=w_e_down[l].astype(BF16),
    )


def _trunk(x, mem, layers, bias_tiles, g_final):
    b, s, d = x.shape
    t = b * s
    cap = CAPACITY_FACTOR * t // N_EXPERTS
    assert s % TM == 0 and s % TQ == 0 and s % TK == 0 and t % CH == 0 and cap % RB == 0
    rope_tab = _rope_tables(s)
    x2d = x.reshape(t, d)
    for l, p in enumerate(layers):
        zr, qa, ka, va = _project(x2d, b, s, p["g_mix"], p["w_z"], rope_tab, p["g_cq"], p["g_ckv"],
                                  p["w_uq"], p["w_k"], p["w_v"])
        zr3 = zr.reshape(b, s, ZR_COLS)
        oa = _mla_attention(qa, ka, va).reshape(t, 256)
        dil = [_dil_attention(zr3, g, dl) for g, (_, dl) in enumerate(DIL_GROUPS)]
        lam_init = 0.8 - 0.6 * math.exp(-0.3 * l)
        oc = _diff_attention(zr3, p["lam"], p["g_diff"], lam_init).reshape(t, 256)
        od = _na_attention(zr3, bias_tiles[l]).reshape(t, 256)
        x2d = _merge(x2d, p["g_mix"], oa, [o for o, _ in dil], [ls for _, ls in dil], oc, od,
                     p["w_gate"], p["w_branch"], p["w_out"])
        k_mem, v_mem = _mem_kv(mem, p["g_mem"], p["w_ck"], p["w_cv"])
        x2d, h3, aff_t = _cross(x2d, b, s, p["g_cross"], p["w_cq"], k_mem, v_mem, p["w_co"], p["g_ffn"],
                                p["w_router_t"])
        pos = _select(aff_t, cap)
        g_items, s_items = _work_items(pos, cap)
        pos3 = pos.reshape(N_EXPERTS, 1, t)
        ye = _expert_ffn(g_items, h3, pos3, aff_t.reshape(N_EXPERTS, 1, t), p["w_e_gate"], p["w_e_up"],
                         p["w_e_down"], cap)
        x2d = _expert_scatter(s_items, x2d, pos3, ye, g_final[None], final_norm=(l == len(layers) - 1))
    return x2d.reshape(b, s, d)


def kernel(x_prompt, x_sample, mem_prompt, mem_sample, w_in, g_mix, g_cq, w_uq, g_ckv, w_ukv, lam_q1, lam_k1, lam_q2, lam_k2, g_diff, na_bias, w_gate, w_branch, w_out, g_cross, g_mem, w_cq, w_ck, w_cv, w_co, g_ffn, w_router, w_e_gate, w_e_up, w_e_down, g_final):
    depth = w_in.shape[0]
    layers = [_prep_layer(l, w_in, g_mix, g_cq, w_uq, g_ckv, w_ukv, lam_q1, lam_k1, lam_q2, lam_k2, g_diff,
                          na_bias, w_gate, w_branch, w_out, g_cross, g_mem, w_cq, w_ck, w_cv, w_co, g_ffn,
                          w_router, w_e_gate, w_e_up, w_e_down) for l in range(depth)]
    bias_tiles = [_na_bias_tiles(p["na_bias"]) for p in layers]
    y_prompt = _trunk(x_prompt, mem_prompt, layers, bias_tiles, g_final)
    y_sample = _trunk(x_sample, mem_sample, layers, bias_tiles, g_final)
    return (y_prompt, y_sample)
```

```python
import functools
import math

import jax
import jax.numpy as jnp
from jax import lax
from jax.experimental import pallas as pl
from jax.experimental.pallas import tpu as pltpu

BF16 = jnp.bfloat16
F32 = jnp.float32
I32 = jnp.int32

D_MODEL = 1024
EPS = 1e-6
NEG = -1e30
ROPE_THETA = 500000.0
HEAD_DIM = 64
MLA_HEADS = 4
MLA_Q_RANK = 256
MLA_KV_RANK = 128
MLA_NOPE = 64
MLA_ROPE = 32
MLA_SCALE = (MLA_NOPE + MLA_ROPE) ** -0.5
DIL_GROUPS = ((128, 1), (512, 4), (2048, 16))
DIL_SIDE = 64
DIL_SCALE = HEAD_DIM ** -0.5
DIFF_QK = 32
DIFF_SCALE = DIFF_QK ** -0.5
NA_ROWS = 8
NA_COLS = 16
GRID_W = 64
NA_SCALE = HEAD_DIM ** -0.5
MEM_HEADS = 4
MEM_HEAD_DIM = D_MODEL // MEM_HEADS
MEM_SCALE = MEM_HEAD_DIM ** -0.5
N_EXPERTS = 16
EXPERT_FF = 512
CAPACITY_FACTOR = 2

LANES = 128
VMEM_LIMIT = 56 * 1024 * 1024

Z_CQ = 0
Z_CKV = 256
Z_KPE = 384
Z_HEAD = 512
Z_DIL = 512
Z_DIL_GROUP = 768
Z_QC = 2816
Z_KC = 3072
Z_QD = 3328
Z_KD = 3584
Z_VD = 3840
Z_COLS = 4096
ZR_COLS = Z_COLS - Z_HEAD
LOG2E = math.log2(math.e)

TM = 512
TQ = 256
TK = 512
TU = 128
NA_TQ = 128
NA_WIN_ROWS = 10
NA_BIAS_OFF = 2
NA_BIAS_N = 18
RB = 256
CH = 512
SEL_CH = 512
CNT_CH = 4096


def _params(n_grid):
    return pltpu.CompilerParams(dimension_semantics=("arbitrary",) * n_grid,
                                vmem_limit_bytes=VMEM_LIMIT)


def _rms(x, g):
    return x * lax.rsqrt(jnp.mean(x * x, axis=-1, keepdims=True) + EPS) * g


def _dot(a, b):
    return jnp.dot(a, b, preferred_element_type=F32)


def _dot_nt(a, b):
    return lax.dot_general(a, b, (((1,), (1,)), ((), ())), preferred_element_type=F32)


def _dot_tn(a, b):
    return lax.dot_general(a, b, (((0,), (0,)), ((), ())), preferred_element_type=F32)


def _rope_block(x, c, sa, sb, half):
    return x * c + pltpu.roll(x, LANES - half, 1) * sa + pltpu.roll(x, half, 1) * sb


def _proj_kernel(x_ref, g_ref, w_ref, rope_ref, gcq_ref, gckv_ref, wuq_ref, wk_ref, wvt_ref, wvct_ref,
                 zr_ref, qa_ref, ka_ref, vat_ref, vct_ref):
    hb = _rms(x_ref[...], g_ref[...]).astype(BF16)

    def rope(x, kind, half):
        return _rope_block(x, rope_ref[3 * kind], rope_ref[3 * kind + 1], rope_ref[3 * kind + 2], half)

    z0 = _dot(hb, w_ref[:, 0:Z_HEAD])
    cq = _rms(z0[:, Z_CQ:Z_CQ + MLA_Q_RANK], gcq_ref[...]).astype(BF16)
    ckv = _rms(z0[:, Z_CKV:Z_CKV + MLA_KV_RANK], gckv_ref[...]).astype(BF16)
    kpe = rope(z0[:, Z_KPE:Z_KPE + LANES], 0, MLA_ROPE // 2)
    qa = _dot(cq, wuq_ref[...]) * (MLA_SCALE * LOG2E)
    kn = _dot(ckv, wk_ref[...])
    for h in range(MLA_HEADS):
        sl = slice(h * LANES, (h + 1) * LANES)
        qa_ref[0, h] = rope(qa[:, sl], 0, MLA_ROPE // 2).astype(BF16)
        ka_ref[0, h] = (kn[:, sl] + kpe).astype(BF16)
    vat_ref[0] = _dot_nt(wvt_ref[...], ckv).astype(BF16)
    vct_ref[0] = _dot_nt(wvct_ref[...], hb).astype(BF16)

    def seg(start, scale, kind, half):
        z = _dot(hb, w_ref[:, start:start + 256])
        if scale is not None:
            z = z * scale
        for j in range(2):
            zb = z[:, j * LANES:(j + 1) * LANES]
            if kind is not None:
                zb = rope(zb, kind, half)
            o = start - Z_HEAD + j * LANES
            zr_ref[:, o:o + LANES] = zb.astype(BF16)

    dil_half = HEAD_DIM // 4 // 2
    diff_half = DIFF_QK // 4 // 2
    for g in range(len(DIL_GROUPS)):
        base = Z_DIL + g * Z_DIL_GROUP
        seg(base, DIL_SCALE, 1, dil_half)
        seg(base + 256, None, 1, dil_half)
        seg(base + 512, None, None, 0)
    seg(Z_QC, DIFF_SCALE * LOG2E, 2, diff_half)
    seg(Z_KC, None, 2, diff_half)
    seg(Z_QD, NA_SCALE, None, 0)
    seg(Z_KD, None, None, 0)
    seg(Z_VD, None, None, 0)


def _const_spec(shape):
    nd = len(shape)
    return pl.BlockSpec(shape, lambda *_: (0,) * nd)


def _project(x2d, b, s, g_mix, w_z, rope_tab, g_cq, g_ckv, w_uq, w_k, w_vt, w_vct):
    t = b * s
    n_s = s // TM
    v_cols = MLA_HEADS * HEAD_DIM
    return pl.pallas_call(
        _proj_kernel,
        grid=(t // TM,),
        in_specs=[
            pl.BlockSpec((TM, D_MODEL), lambda i: (i, 0)),
            _const_spec((1, D_MODEL)),
            _const_spec((D_MODEL, Z_COLS)),
            pl.BlockSpec((9, TM, LANES), lambda i: (0, i % n_s, 0)),
            _const_spec((1, MLA_Q_RANK)),
            _const_spec((1, MLA_KV_RANK)),
            _const_spec((MLA_Q_RANK, MLA_HEADS * LANES)),
            _const_spec((MLA_KV_RANK, MLA_HEADS * LANES)),
            _const_spec((v_cols, MLA_KV_RANK)),
            _const_spec((v_cols, D_MODEL)),
        ],
        out_specs=[
            pl.BlockSpec((TM, ZR_COLS), lambda i: (i, 0)),
            pl.BlockSpec((1, MLA_HEADS, TM, LANES), lambda i: (i // n_s, 0, i % n_s, 0)),
            pl.BlockSpec((1, MLA_HEADS, TM, LANES), lambda i: (i // n_s, 0, i % n_s, 0)),
            pl.BlockSpec((1, v_cols, TM), lambda i: (i // n_s, 0, i % n_s)),
            pl.BlockSpec((1, v_cols, TM), lambda i: (i // n_s, 0, i % n_s)),
        ],
        out_shape=[
            jax.ShapeDtypeStruct((t, ZR_COLS), BF16),
            jax.ShapeDtypeStruct((b, MLA_HEADS, s, LANES), BF16),
            jax.ShapeDtypeStruct((b, MLA_HEADS, s, LANES), BF16),
            jax.ShapeDtypeStruct((b, v_cols, s), BF16),
            jax.ShapeDtypeStruct((b, v_cols, s), BF16),
        ],
        compiler_params=_params(1),
        name="proj",
    )(x2d, g_mix, w_z, rope_tab, g_cq, g_ckv, w_uq, w_k, w_vt, w_vct)


def _chunk(c):
    return pl.ds(pl.multiple_of(c * TK, TK), TK)


def _dense_attention(qs, k_at, vt_at, s_ref, p_ref, n_chunks):
    n = len(qs)

    def step(c, cur, nxt, carry):
        rows_next = _chunk(jnp.minimum(c + 1, n_chunks - 1))
        rows_prev = _chunk(jnp.maximum(c - 1, 0))
        new = []
        for i in range(n):
            m, l, acc, alpha_prev = carry[i]
            s_ref[i, nxt] = _dot_nt(k_at(i, rows_next), qs[i])
            acc = alpha_prev * acc + _dot(vt_at(i, rows_prev), p_ref[i, nxt])
            st = s_ref[i, cur]
            m_new = jnp.maximum(m, jnp.max(st, axis=0, keepdims=True))
            alpha = jnp.exp2(m - m_new)
            p = jnp.exp2(st - m_new)
            l = alpha * l + jnp.sum(p, axis=0, keepdims=True)
            p_ref[i, cur] = p.astype(BF16)
            new.append((m_new, l, acc, alpha))
        return tuple(new)

    def body(j, carry):
        return step(2 * j + 1, 1, 0, step(2 * j, 0, 1, carry))

    for i in range(n):
        s_ref[i, 0] = _dot_nt(k_at(i, pl.ds(0, TK)), qs[i])
        p_ref[i, 1] = jnp.zeros((TK, TQ), BF16)
    init = tuple((jnp.full((1, TQ), NEG, F32), jnp.zeros((1, TQ), F32), jnp.zeros((HEAD_DIM, TQ), F32),
                  jnp.ones((1, TQ), F32)) for _ in range(n))
    res = lax.fori_loop(0, n_chunks // 2, body, init)
    out = []
    for i in range(n):
        _, l, acc, alpha = res[i]
        acc = alpha * acc + _dot(vt_at(i, pl.ds((n_chunks - 1) * TK, TK)), p_ref[i, 1])
        out.append((acc, l))
    return out


def _mla_kernel(q_ref, k_ref, vt_ref, o_ref, s_ref, p_ref, *, n_chunks):
    res = _dense_attention(
        [q_ref[0, hh] for hh in range(2)],
        lambda i, rows: k_ref[0, i, rows, :],
        lambda i, rows: vt_ref[0, i * HEAD_DIM:(i + 1) * HEAD_DIM, rows],
        s_ref, p_ref, n_chunks)
    out_t = jnp.concatenate([acc / l for acc, l in res], axis=0)
    o_ref[0] = out_t.T.astype(BF16)


def _mla_attention(qa, ka, vat):
    b, _, s, _ = qa.shape
    assert (s // TK) % 2 == 0
    return pl.pallas_call(
        functools.partial(_mla_kernel, n_chunks=s // TK),
        grid=(b, 2, s // TQ),
        in_specs=[
            pl.BlockSpec((1, 2, TQ, LANES), lambda bi, p, qi: (bi, p, qi, 0)),
            pl.BlockSpec((1, 2, s, LANES), lambda bi, p, qi: (bi, p, 0, 0)),
            pl.BlockSpec((1, LANES, s), lambda bi, p, qi: (bi, p, 0)),
        ],
        out_specs=pl.BlockSpec((1, TQ, LANES), lambda bi, p, qi: (bi, qi, p)),
        out_shape=jax.ShapeDtypeStruct((b, s, 2 * LANES), BF16),
        scratch_shapes=[pltpu.VMEM((2, 2, TK, TQ), F32), pltpu.VMEM((2, 2, TK, TQ), BF16)],
        compiler_params=_params(3),
        name="mla_attn",
    )(qa, ka, vat)


def _diff_kernel(q_ref, k_ref, vt_ref, lam_ref, g_ref, o_ref, s_ref, p_ref, *, n_chunks, lam_init):
    lane = lax.broadcasted_iota(I32, (TQ, LANES), 1)
    lv = lam_ref[...]
    lam = (jnp.exp(jnp.sum(lv[0:1] * lv[1:2], axis=-1, keepdims=True))
           - jnp.exp(jnp.sum(lv[2:3] * lv[3:4], axis=-1, keepdims=True)) + lam_init)
    q = q_ref[0]
    qs = [jnp.where((lane >= i * DIFF_QK) & (lane < (i + 1) * DIFF_QK), q, jnp.zeros_like(q)) for i in range(4)]
    res = _dense_attention(
        qs,
        lambda i, rows: k_ref[0, rows, :],
        lambda i, rows: vt_ref[0, (i // 2) * HEAD_DIM:(i // 2 + 1) * HEAD_DIM, rows],
        s_ref, p_ref, n_chunks)
    parts = [acc / l for acc, l in res]
    out_t = jnp.concatenate([parts[2 * hh] - lam * parts[2 * hh + 1] for hh in range(2)], axis=0)
    o = out_t.T
    head0 = lane < HEAD_DIM
    sq = o * o
    ms0 = jnp.sum(jnp.where(head0, sq, 0.0), axis=-1, keepdims=True)
    ms1 = jnp.sum(jnp.where(head0, 0.0, sq), axis=-1, keepdims=True)
    ms = jnp.where(head0, ms0, ms1) * (1.0 / HEAD_DIM)
    o_ref[0] = (o * lax.rsqrt(ms + EPS) * g_ref[...] * (1.0 - lam_init)).astype(BF16)


def _diff_attention(zr3, vct, lam_vecs, g_diff2, lam_init):
    b, s, _ = zr3.shape
    assert (s // TK) % 2 == 0
    qb, kb = (Z_QC - Z_HEAD) // LANES, (Z_KC - Z_HEAD) // LANES
    return pl.pallas_call(
        functools.partial(_diff_kernel, n_chunks=s // TK, lam_init=lam_init),
        grid=(b, 2, s // TQ),
        in_specs=[
            pl.BlockSpec((1, TQ, LANES), lambda bi, p, qi: (bi, qi, qb + p)),
            pl.BlockSpec((1, s, LANES), lambda bi, p, qi: (bi, 0, kb + p)),
            pl.BlockSpec((1, LANES, s), lambda bi, p, qi: (bi, p, 0)),
            _const_spec((4, DIFF_QK)),
            _const_spec((1, LANES)),
        ],
        out_specs=pl.BlockSpec((1, TQ, LANES), lambda bi, p, qi: (bi, qi, p)),
        out_shape=jax.ShapeDtypeStruct((b, s, 2 * LANES), BF16),
        scratch_shapes=[pltpu.VMEM((4, 2, TK, TQ), F32), pltpu.VMEM((4, 2, TK, TQ), BF16)],
        compiler_params=_params(3),
        name="diff_attn",
    )(zr3, zr3, vct, lam_vecs, g_diff2)


def _dil_kernel(q_ref, k_ref, v_ref, o_ref, lse_ref, *, u_len, win):
    u0 = pl.program_id(2) * TU
    start = pl.multiple_of(jnp.clip(u0 - DIL_SIDE, 0, u_len - win), DIL_SIDE)
    kw = k_ref[0, pl.ds(start, win), :]
    vw = v_ref[0, pl.ds(start, win), :]
    q = q_ref[0]
    qpos = u0 + lax.broadcasted_iota(I32, (TU, win), 0)
    kpos = start + lax.broadcasted_iota(I32, (TU, win), 1)
    valid = jnp.abs(kpos - qpos) <= DIL_SIDE
    lane = lax.broadcasted_iota(I32, (TU, 4 * HEAD_DIM), 1)
    out = jnp.zeros((TU, 4 * HEAD_DIM), F32)
    lse_out = jnp.zeros((TU, 4 * HEAD_DIM), F32)
    for h in range(4):
        in_head = (lane >= h * HEAD_DIM) & (lane < (h + 1) * HEAD_DIM)
        qm = jnp.where(in_head, q, jnp.zeros_like(q))
        s = jnp.where(valid, _dot_nt(qm, kw), NEG)
        m = jnp.max(s, axis=-1, keepdims=True)
        e = jnp.exp(s - m)
        l = jnp.sum(e, axis=-1, keepdims=True)
        o = _dot((e / l).astype(BF16), vw)
        out = jnp.where(in_head, o, out)
        lse_out = jnp.where(in_head, m + jnp.log(l), lse_out)
    o_ref[0] = out.astype(BF16)
    lse_ref[0] = lse_out


def _dil_attention(zr3, g, dil):
    b, s, _ = zr3.shape
    u_len = s // dil
    win = TU + 2 * DIL_SIDE
    assert u_len >= win and u_len % TU == 0
    col0 = Z_DIL - Z_HEAD + g * Z_DIL_GROUP
    if dil == 1:
        zv, units, qu = zr3, ZR_COLS // 256, col0 // 256
    else:
        zv = zr3[:, :, col0:col0 + Z_DIL_GROUP].reshape(b, u_len, dil * Z_DIL_GROUP)
        units, qu = Z_DIL_GROUP // 256, 0
    ku, vu = qu + 1, qu + 2
    o, lse = pl.pallas_call(
        functools.partial(_dil_kernel, u_len=u_len, win=win),
        grid=(b, dil, u_len // TU),
        in_specs=[
            pl.BlockSpec((1, TU, 256), lambda bi, r, ui: (bi, ui, r * units + qu)),
            pl.BlockSpec((1, u_len, 256), lambda bi, r, ui: (bi, 0, r * units + ku)),
            pl.BlockSpec((1, u_len, 256), lambda bi, r, ui: (bi, 0, r * units + vu)),
        ],
        out_specs=[
            pl.BlockSpec((1, TU, 256), lambda bi, r, ui: (bi, ui, r)),
            pl.BlockSpec((1, TU, 256), lambda bi, r, ui: (bi, ui, r)),
        ],
        out_shape=[
            jax.ShapeDtypeStruct((b, u_len, dil * 256), BF16),
            jax.ShapeDtypeStruct((b, u_len, dil * 256), F32),
        ],
        compiler_params=_params(3),
        name=f"dil_attn_{dil}",
    )(zv, zv, zv)
    return o.reshape(b * s, 256), lse.reshape(b * s, 256)


def _na_bias_kernel(tbl_ref, o_ref):
    h = pl.program_id(0)
    d = pl.program_id(1)
    n_rel_r, n_rel_c = 2 * NA_ROWS - 1, 2 * NA_COLS - 1
    dl = jnp.clip(d - NA_BIAS_OFF, 0, n_rel_r - 1)
    dr = jnp.clip(d - NA_BIAS_OFF + 1, 0, n_rel_r - 1)
    qc = lax.broadcasted_iota(I32, (GRID_W, LANES), 0)
    ln = lax.broadcasted_iota(I32, (GRID_W, LANES), 1)
    left = ln < GRID_W
    rel = jnp.where(left, ln, ln - GRID_W) - qc + (NA_COLS - 1)
    acc = jnp.zeros((GRID_W, LANES), F32)
    for dd in range(n_rel_c):
        vl = tbl_ref[(h * n_rel_r + dl) * n_rel_c + dd]
        vr = tbl_ref[(h * n_rel_r + dr) * n_rel_c + dd]
        acc = jnp.where(rel == dd, jnp.where(left, vl, vr), acc)
    o_ref[0, 0] = acc


def _na_bias_tiles(na_bias_l):
    n_h = na_bias_l.shape[0]
    return pl.pallas_call(
        _na_bias_kernel,
        grid=(n_h, NA_BIAS_N),
        in_specs=[pl.BlockSpec(memory_space=pltpu.SMEM)],
        out_specs=pl.BlockSpec((1, 1, GRID_W, LANES), lambda h, d: (h, d, 0, 0)),
        out_shape=jax.ShapeDtypeStruct((n_h, NA_BIAS_N, GRID_W, LANES), F32),
        compiler_params=_params(2),
        name="na_bias",
    )(na_bias_l.reshape(-1))


def _na_kernel(q_ref, k_ref, v_ref, bias_ref, o_ref, *, n_rows):
    r0 = pl.program_id(1) * 2
    ws = jnp.clip(r0 - NA_ROWS // 2, 0, n_rows - NA_WIN_ROWS)
    n_keys = NA_WIN_ROWS * GRID_W
    start = pl.multiple_of(ws * GRID_W, GRID_W)
    kw = k_ref[0, pl.ds(start, n_keys), :]
    vw = v_ref[0, pl.ds(start, n_keys), :]
    q = q_ref[0]
    qi = lax.broadcasted_iota(I32, (NA_TQ, n_keys), 0)
    kj = lax.broadcasted_iota(I32, (NA_TQ, n_keys), 1)
    qr, qc = r0 + qi // GRID_W, qi % GRID_W
    kr, kc = ws + kj // GRID_W, kj % GRID_W
    rs = jnp.clip(qr - NA_ROWS // 2, 0, n_rows - NA_ROWS)
    cs = jnp.clip(qc - NA_COLS // 2, 0, GRID_W - NA_COLS)
    valid = (kr >= rs) & (kr < rs + NA_ROWS) & (kc >= cs) & (kc < cs + NA_COLS)
    lane = lax.broadcasted_iota(I32, (NA_TQ, 4 * HEAD_DIM), 1)
    out = jnp.zeros((NA_TQ, 4 * HEAD_DIM), F32)
    for h in range(4):
        halves = []
        for half in range(2):
            d0 = ws - r0 - half + (NA_ROWS - 1) + NA_BIAS_OFF
            halves.append(jnp.concatenate(
                [bias_ref[h, pl.ds(d0 + 2 * m, 1)][0] for m in range(NA_WIN_ROWS // 2)], axis=1))
        bias = jnp.concatenate(halves, axis=0)
        in_head = (lane >= h * HEAD_DIM) & (lane < (h + 1) * HEAD_DIM)
        qm = jnp.where(in_head, q, jnp.zeros_like(q))
        s = jnp.where(valid, _dot_nt(qm, kw) + bias, NEG)
        m = jnp.max(s, axis=-1, keepdims=True)
        e = jnp.exp(s - m)
        l = jnp.sum(e, axis=-1, keepdims=True)
        o = _dot((e / l).astype(BF16), vw)
        out = jnp.where(in_head, o, out)
    o_ref[0] = out.astype(BF16)


def _na_attention(zr3, bias_tiles):
    b, s, _ = zr3.shape
    n_rows = s // GRID_W
    assert n_rows >= NA_WIN_ROWS
    qu, ku, vu = (Z_QD - Z_HEAD) // 256, (Z_KD - Z_HEAD) // 256, (Z_VD - Z_HEAD) // 256
    return pl.pallas_call(
        functools.partial(_na_kernel, n_rows=n_rows),
        grid=(b, s // NA_TQ),
        in_specs=[
            pl.BlockSpec((1, NA_TQ, 256), lambda bi, i: (bi, i, qu)),
            pl.BlockSpec((1, s, 256), lambda bi, i: (bi, 0, ku)),
            pl.BlockSpec((1, s, 256), lambda bi, i: (bi, 0, vu)),
            _const_spec(bias_tiles.shape),
        ],
        out_specs=pl.BlockSpec((1, NA_TQ, 256), lambda bi, i: (bi, i, 0)),
        out_shape=jax.ShapeDtypeStruct((b, s, 256), BF16),
        compiler_params=_params(2),
        name="na_attn",
    )(zr3, zr3, zr3, bias_tiles)


def _merge_kernel(x_ref, g_ref, oa_ref, ob0_ref, ob1_ref, ob2_ref, l0_ref, l1_ref, l2_ref, oc_ref, od_ref,
                  wg_ref, wb_ref, wo_ref, y_ref):
    x = x_ref[...]
    hb = _rms(x, g_ref[...]).astype(BF16)
    lses = (l0_ref[...], l1_ref[...], l2_ref[...])
    mx = jnp.maximum(jnp.maximum(lses[0], lses[1]), lses[2])
    ws = [jnp.exp(l - mx) for l in lses]
    den = ws[0] + ws[1] + ws[2]
    obs = (ob0_ref, ob1_ref, ob2_ref)
    ob = (ws[0] / den) * obs[0][...].astype(F32)
    for g in range(1, 3):
        ob = ob + (ws[g] / den) * obs[g][...].astype(F32)
    branches = (oa_ref[...], ob.astype(BF16), oc_ref[...], od_ref[...])
    merged = None
    for i, o in enumerate(branches):
        gate = jax.nn.sigmoid(_dot(hb, wg_ref[i]))
        term = gate * _dot(o, wb_ref[i])
        merged = term if merged is None else merged + term
    y_ref[...] = x + _dot(merged.astype(BF16), wo_ref[...])


def _merge(x2d, g_mix, oa, obs, lses, oc, od, w_gate, w_branch, w_out):
    t = x2d.shape[0]
    tok = lambda w: pl.BlockSpec((TM, w), lambda i: (i, 0))
    return pl.pallas_call(
        _merge_kernel,
        grid=(t // TM,),
        in_specs=[tok(D_MODEL), _const_spec((1, D_MODEL))] + [tok(256)] * 9 + [
            _const_spec(w_gate.shape), _const_spec(w_branch.shape), _const_spec(w_out.shape)],
        out_specs=tok(D_MODEL),
        out_shape=jax.ShapeDtypeStruct((t, D_MODEL), F32),
        compiler_params=_params(1),
        name="merge",
    )(x2d, g_mix, oa, obs[0], obs[1], obs[2], lses[0], lses[1], lses[2], oc, od, w_gate, w_branch, w_out)


def _memkv_kernel(m_ref, g_ref, wk_ref, wv_ref, k_ref, v_ref):
    mb = _rms(m_ref[0], g_ref[...]).astype(BF16)
    k_ref[0] = _dot(mb, wk_ref[...]).astype(BF16)
    v_ref[0] = _dot(mb, wv_ref[...]).astype(BF16)


def _mem_kv(mem, g_mem, w_ck, w_cv):
    b, m_len, d = mem.shape
    blk = pl.BlockSpec((1, m_len, d), lambda bi: (bi, 0, 0))
    return pl.pallas_call(
        _memkv_kernel,
        grid=(b,),
        in_specs=[blk, _const_spec((1, d)), _const_spec((d, d)), _const_spec((d, d))],
        out_specs=[blk, blk],
        out_shape=[jax.ShapeDtypeStruct((b, m_len, d), BF16)] * 2,
        compiler_params=_params(1),
        name="mem_kv",
    )(mem, g_mem, w_ck, w_cv)


def _cross_kernel(x_ref, g_ref, wq_ref, k_ref, v_ref, wo_ref, gf_ref, wr_ref, y_ref, h_ref, aff_ref):
    x = x_ref[...]
    hb = _rms(x, g_ref[...]).astype(BF16)
    q = (_dot(hb, wq_ref[...]) * MEM_SCALE).astype(BF16)
    outs = []
    for h in range(MEM_HEADS):
        sl = slice(h * MEM_HEAD_DIM, (h + 1) * MEM_HEAD_DIM)
        s = _dot_nt(q[:, sl], k_ref[0, :, sl])
        m = jnp.max(s, axis=-1, keepdims=True)
        e = jnp.exp(s - m)
        p = e / jnp.sum(e, axis=-1, keepdims=True)
        outs.append(_dot(p.astype(BF16), v_ref[0, :, sl]).astype(BF16))
    y = x + _dot(jnp.concatenate(outs, axis=1), wo_ref[...])
    y_ref[...] = y
    h3 = _rms(y, gf_ref[...]).astype(BF16)
    h_ref[...] = h3
    logits = _dot_nt(wr_ref[...], h3)
    m = jnp.max(logits, axis=0, keepdims=True)
    e = jnp.exp(logits - m)
    aff_ref[...] = e / jnp.sum(e, axis=0, keepdims=True)


def _cross(x2d, b, s, g_cross, w_cq, k_mem, v_mem, w_co, g_ffn, w_router_t):
    t = b * s
    n_s = s // TM
    m_len = k_mem.shape[1]
    tok = lambda: pl.BlockSpec((TM, D_MODEL), lambda i: (i, 0))
    mem_blk = pl.BlockSpec((1, m_len, D_MODEL), lambda i: (i // n_s, 0, 0))
    return pl.pallas_call(
        _cross_kernel,
        grid=(t // TM,),
        in_specs=[tok(), _const_spec((1, D_MODEL)), _const_spec((D_MODEL, D_MODEL)), mem_blk, mem_blk,
                  _const_spec((D_MODEL, D_MODEL)), _const_spec((1, D_MODEL)),
                  _const_spec((N_EXPERTS, D_MODEL))],
        out_specs=[tok(), tok(), pl.BlockSpec((N_EXPERTS, TM), lambda i: (0, i))],
        out_shape=[jax.ShapeDtypeStruct((t, D_MODEL), F32), jax.ShapeDtypeStruct((t, D_MODEL), BF16),
                   jax.ShapeDtypeStruct((N_EXPERTS, t), F32)],
        compiler_params=_params(1),
        name="cross",
    )(x2d, g_cross, w_cq, k_mem, v_mem, w_co, g_ffn, w_router_t)


def _select_kernel(aff_ref, pos_ref, *, n_tok, cap):
    n_cnt = n_tok // CNT_CH

    def count_ge(cand):
        def body(c, acc):
            bits = pltpu.bitcast(aff_ref[:, pl.ds(pl.multiple_of(c * CNT_CH, CNT_CH), CNT_CH)], I32)
            return acc + jnp.sum(jnp.where(bits >= cand, 1.0, 0.0), axis=1, keepdims=True)
        return lax.fori_loop(0, n_cnt, body, jnp.zeros((N_EXPERTS, 1), F32))

    def bit_body(i, prefix):
        cand = prefix | lax.shift_left(jnp.int32(1), 30 - i)
        return jnp.where(count_ge(cand) >= cap, cand, prefix)

    thr = lax.fori_loop(0, 31, bit_body, jnp.zeros((N_EXPERTS, 1), I32))
    need_eq = cap - count_ge(thr + 1)

    row = lax.broadcasted_iota(I32, (SEL_CH, SEL_CH), 0)
    col = lax.broadcasted_iota(I32, (SEL_CH, SEL_CH), 1)
    tri = jnp.where(row <= col, 1.0, 0.0).astype(BF16)

    def chunk_body(c, carry):
        c_eq, c_sel = carry
        sl = pl.ds(pl.multiple_of(c * SEL_CH, SEL_CH), SEL_CH)
        bits = pltpu.bitcast(aff_ref[:, sl], I32)
        eq = jnp.where(bits == thr, 1.0, 0.0)
        gt = jnp.where(bits > thr, 1.0, 0.0)
        eq_before = c_eq + _dot(eq.astype(BF16), tri) - eq
        sel = gt + eq * jnp.where(eq_before < need_eq, 1.0, 0.0)
        rank = c_sel + _dot(sel.astype(BF16), tri) - sel
        pos_ref[:, sl] = jnp.where(sel > 0.5, rank, -1.0).astype(I32)
        return (c_eq + jnp.sum(eq, axis=1, keepdims=True), c_sel + jnp.sum(sel, axis=1, keepdims=True))

    zero = jnp.zeros((N_EXPERTS, 1), F32)
    lax.fori_loop(0, n_tok // SEL_CH, chunk_body, (zero, zero))


def _select(aff_t, cap):
    n_tok = aff_t.shape[1]
    return pl.pallas_call(
        functools.partial(_select_kernel, n_tok=n_tok, cap=cap),
        grid=(1,),
        in_specs=[_const_spec(aff_t.shape)],
        out_specs=_const_spec(aff_t.shape),
        out_shape=jax.ShapeDtypeStruct(aff_t.shape, I32),
        compiler_params=_params(1),
        name="select",
    )(aff_t)


def _one_hot_rows(pos_row, j):
    rows = j * RB + lax.broadcasted_iota(I32, (RB, CH), 0)
    return pos_row == rows


def _ffn_kernel(ie_ref, ij_ref, ik_ref, first_ref, last_ref, valid_ref,
                h_ref, pos_ref, aff_ref, wg_ref, wu_ref, wd_ref, ye_ref, acc_ref, gate_ref):
    w = pl.program_id(0)

    @pl.when(first_ref[w] == 1)
    def _():
        acc_ref[...] = jnp.zeros_like(acc_ref)
        gate_ref[...] = jnp.zeros_like(gate_ref)

    @pl.when(valid_ref[w] == 1)
    def _():
        hit = _one_hot_rows(pos_ref[0], ij_ref[w])
        acc_ref[...] += _dot(jnp.where(hit, 1.0, 0.0).astype(BF16), h_ref[...])
        gate_ref[...] += jnp.sum(jnp.where(hit, aff_ref[0], 0.0), axis=1, keepdims=True)

    @pl.when(last_ref[w] == 1)
    def _():
        xe = acc_ref[...].astype(BF16)
        hid = jax.nn.silu(_dot(xe, wg_ref[0])) * _dot(xe, wu_ref[0])
        ye_ref[0] = (_dot(hid.astype(BF16), wd_ref[0]) * gate_ref[...]).astype(BF16)


def _expert_ffn(items, h3, pos3, aff3, w_e_gate, w_e_up, w_e_down, cap):
    n_items = items[0].shape[0]
    grid_spec = pltpu.PrefetchScalarGridSpec(
        num_scalar_prefetch=6,
        grid=(n_items,),
        in_specs=[
            pl.BlockSpec((CH, D_MODEL), lambda w, ie, ij, ik, *_: (ik[w], 0)),
            pl.BlockSpec((1, 1, CH), lambda w, ie, ij, ik, *_: (ie[w], 0, ik[w])),
            pl.BlockSpec((1, 1, CH), lambda w, ie, ij, ik, *_: (ie[w], 0, ik[w])),
            pl.BlockSpec((1, D_MODEL, EXPERT_FF), lambda w, ie, *_: (ie[w], 0, 0)),
            pl.BlockSpec((1, D_MODEL, EXPERT_FF), lambda w, ie, *_: (ie[w], 0, 0)),
            pl.BlockSpec((1, EXPERT_FF, D_MODEL), lambda w, ie, *_: (ie[w], 0, 0)),
        ],
        out_specs=pl.BlockSpec((1, RB, D_MODEL), lambda w, ie, ij, *_: (ie[w], ij[w], 0)),
        scratch_shapes=[pltpu.VMEM((RB, D_MODEL), F32), pltpu.VMEM((RB, 1), F32)],
    )
    return pl.pallas_call(
        _ffn_kernel,
        grid_spec=grid_spec,
        out_shape=jax.ShapeDtypeStruct((N_EXPERTS, cap, D_MODEL), BF16),
        compiler_params=_params(1),
        name="expert_ffn",
    )(*items, h3, pos3, aff3, w_e_gate, w_e_up, w_e_down)


def _scatter_kernel(ik_ref, ie_ref, ij_ref, first_ref, last_ref, valid_ref,
                    x_ref, pos_ref, ye_ref, gfin_ref, y_ref, *, final_norm):
    w = pl.program_id(0)

    @pl.when(first_ref[w] == 1)
    def _():
        y_ref[...] = x_ref[...]

    @pl.when(valid_ref[w] == 1)
    def _():
        hit = _one_hot_rows(pos_ref[0], ij_ref[w])
        y_ref[...] += _dot_tn(jnp.where(hit, 1.0, 0.0).astype(BF16), ye_ref[0])

    if final_norm:
        @pl.when(last_ref[w] == 1)
        def _():
            y_ref[...] = _rms(y_ref[...], gfin_ref[...])


def _expert_scatter(items, x2d, pos3, ye, g_final, final_norm):
    n_items = items[0].shape[0]
    t = x2d.shape[0]
    grid_spec = pltpu.PrefetchScalarGridSpec(
        num_scalar_prefetch=6,
        grid=(n_items,),
        in_specs=[
            pl.BlockSpec((CH, D_MODEL), lambda w, ik, *_: (ik[w], 0)),
            pl.BlockSpec((1, 1, CH), lambda w, ik, ie, *_: (ie[w], 0, ik[w])),
            pl.BlockSpec((1, RB, D_MODEL), lambda w, ik, ie, ij, *_: (ie[w], ij[w], 0)),
            pl.BlockSpec((1, D_MODEL), lambda w, *_: (0, 0)),
        ],
        out_specs=pl.BlockSpec((CH, D_MODEL), lambda w, ik, *_: (ik[w], 0)),
    )
    return pl.pallas_call(
        functools.partial(_scatter_kernel, final_norm=final_norm),
        grid_spec=grid_spec,
        out_shape=jax.ShapeDtypeStruct((t, D_MODEL), F32),
        compiler_params=_params(1),
        name="expert_scatter",
    )(*items, x2d, pos3, ye, g_final)


def _flatten_items(n_per_block, n_max):
    off_incl = jnp.cumsum(n_per_block)
    off_excl = off_incl - n_per_block
    total = off_incl[-1]
    w = jnp.arange(n_max, dtype=I32)
    valid = w < total
    wv = jnp.where(valid, w, total - 1)
    blk = jnp.searchsorted(off_incl, wv, side="right").astype(I32)
    return blk, wv - off_excl[blk], wv, valid, off_excl, off_incl


def _work_items(pos, cap):
    n_e, n_tok = pos.shape
    n_k, n_j = n_tok // CH, cap // RB
    cnt = jnp.sum((pos >= 0).reshape(n_e, n_k, CH), axis=-1, dtype=I32)
    ends = jnp.cumsum(cnt, axis=1)
    begins = ends - cnt

    jb = jnp.arange(n_j, dtype=I32) * RB
    k_first = jax.vmap(lambda r: jnp.searchsorted(r, jb, side="right"))(ends).astype(I32)
    k_last = jax.vmap(lambda r: jnp.searchsorted(r, jb + RB, side="left"))(ends).astype(I32)
    n_blk = (k_last - k_first + 1).reshape(-1)
    blk, off, wv, valid, off_excl, off_incl = _flatten_items(n_blk, n_e * (n_j + n_k))
    g_items = (blk // n_j, blk % n_j, k_first.reshape(-1)[blk] + off,
               (valid & (wv == off_excl[blk])).astype(I32),
               (valid & (wv == off_incl[blk] - 1)).astype(I32), valid.astype(I32))

    has = cnt > 0
    j_first = jnp.minimum(begins // RB, n_j - 1)
    j_last = jnp.where(has, (ends - 1) // RB, j_first)
    always = (jnp.arange(n_e) == 0)[:, None]
    n_ke = jnp.where(has | always, j_last - j_first + 1, 0).T.reshape(-1)
    blk, off, wv, valid, off_excl, off_incl = _flatten_items(n_ke, n_e * (n_j + n_k))
    k_of = blk // n_e
    chunk_first = off_excl[k_of * n_e]
    chunk_end = off_incl[k_of * n_e + n_e - 1]
    s_items = (k_of, blk % n_e, j_first.T.reshape(-1)[blk] + off,
               (valid & (wv == chunk_first)).astype(I32),
               (valid & (wv == chunk_end - 1)).astype(I32), valid.astype(I32))
    return tuple(a.astype(I32) for a in g_items), tuple(a.astype(I32) for a in s_items)


def _rope_tables(s):
    pos = jnp.arange(s, dtype=F32)[:, None]
    lane = jnp.arange(LANES)
    tabs = []
    for period, base, rot in ((LANES, MLA_NOPE, MLA_ROPE), (HEAD_DIM, 0, HEAD_DIM // 4), (DIFF_QK, 0, DIFF_QK // 4)):
        half = rot // 2
        rel = lane % period - base
        inv_freq = ROPE_THETA ** (-jnp.arange(half, dtype=F32) * (2.0 / rot))
        ang = pos * inv_freq[None, :]
        cos, sin = jnp.cos(ang), jnp.sin(ang)
        idx = jnp.clip(jnp.where(rel >= half, rel - half, rel), 0, half - 1)
        first = (rel >= 0) & (rel < half)
        second = (rel >= half) & (rel < rot)
        tabs.append(jnp.where(first | second, cos[:, idx], 1.0))
        tabs.append(jnp.where(first, -sin[:, idx], 0.0))
        tabs.append(jnp.where(second, sin[:, idx], 0.0))
    return jnp.stack(tabs).astype(F32)


def _prep_layer(l, w_in, g_mix, g_cq, w_uq, g_ckv, w_ukv, lam_q1, lam_k1, lam_q2, lam_k2, g_diff, na_bias,
                w_gate, w_branch, w_out, g_cross, g_mem, w_cq, w_ck, w_cv, w_co, g_ffn, w_router,
                w_e_gate, w_e_up, w_e_down):
    d = D_MODEL
    wi = w_in[l]
    kpe0 = MLA_Q_RANK + MLA_KV_RANK
    z64 = jnp.zeros((d, MLA_NOPE), F32)
    z32 = jnp.zeros((d, LANES - MLA_NOPE - MLA_ROPE), F32)
    rest = wi[:, kpe0 + MLA_ROPE:]
    vc0 = Z_KC + 256 - Z_HEAD
    n_g = len(DIL_GROUPS)
    dil_cols = rest[:, :n_g * Z_DIL_GROUP].reshape(d, 3, n_g, 256).transpose(0, 2, 1, 3).reshape(d, n_g * Z_DIL_GROUP)
    w_z = jnp.concatenate([wi[:, :kpe0], z64, wi[:, kpe0:kpe0 + MLA_ROPE], z32, dil_cols,
                           rest[:, n_g * Z_DIL_GROUP:vc0], rest[:, vc0 + 256:]], axis=1)
    w_vct = rest[:, vc0:vc0 + 256].T
    qh = MLA_NOPE + MLA_ROPE
    wuq = w_uq[l].reshape(MLA_Q_RANK, MLA_HEADS, qh)
    wuq = jnp.pad(wuq, ((0, 0), (0, 0), (0, LANES - qh))).reshape(MLA_Q_RANK, MLA_HEADS * LANES)
    wukv = w_ukv[l].reshape(MLA_KV_RANK, MLA_HEADS, MLA_NOPE + HEAD_DIM)
    wk = jnp.pad(wukv[:, :, :MLA_NOPE], ((0, 0), (0, 0), (0, LANES - MLA_NOPE))).reshape(MLA_KV_RANK, MLA_HEADS * LANES)
    wv = wukv[:, :, MLA_NOPE:].reshape(MLA_KV_RANK, MLA_HEADS * HEAD_DIM)
    return dict(
        g_mix=g_mix[l][None], w_z=w_z.astype(BF16), g_cq=g_cq[l][None], g_ckv=g_ckv[l][None],
        w_uq=wuq.astype(BF16), w_k=wk.astype(BF16), w_vt=wv.T.astype(BF16), w_vct=w_vct.astype(BF16),
        lam=jnp.stack([lam_q1[l], lam_k1[l], lam_q2[l], lam_k2[l]]).astype(F32),
        g_diff=jnp.tile(g_diff[l], 2)[None], na_bias=na_bias[l],
        w_gate=w_gate[l].astype(BF16), w_branch=w_branch[l].astype(BF16), w_out=w_out[l].astype(BF16),
        g_cross=g_cross[l][None], g_mem=g_mem[l][None], w_cq=w_cq[l].astype(BF16), w_ck=w_ck[l].astype(BF16),
        w_cv=w_cv[l].astype(BF16), w_co=w_co[l].astype(BF16), g_ffn=g_ffn[l][None],
        w_router_t=w_router[l].T.astype(BF16), w_e_gate=w_e_gate[l].astype(BF16),
        w_e_up=w_e_up[l].astype(BF16), w_e_down=w_e_down[l].astype(BF16),
    )


def _trunk(x, mem, layers, bias_tiles, g_final):
    b, s, d = x.shape
    t = b * s
    cap = CAPACITY_FACTOR * t // N_EXPERTS
    assert s % TM == 0 and s % TQ == 0 and s % TK == 0 and t % CH == 0 and cap % RB == 0
    rope_tab = _rope_tables(s)
    x2d = x.reshape(t, d)
    for l, p in enumerate(layers):
        zr, qa, ka, vat, vct = _project(x2d, b, s, p["g_mix"], p["w_z"], rope_tab, p["g_cq"], p["g_ckv"],
                                        p["w_uq"], p["w_k"], p["w_vt"], p["w_vct"])
        zr3 = zr.reshape(b, s, ZR_COLS)
        oa = _mla_attention(qa, ka, vat).reshape(t, 256)
        dil = [_dil_attention(zr3, g, dl) for g, (_, dl) in enumerate(DIL_GROUPS)]
        lam_init = 0.8 - 0.6 * math.exp(-0.3 * l)
        oc = _diff_attention(zr3, vct, p["lam"], p["g_diff"], lam_init).reshape(t, 256)
        od = _na_attention(zr3, bias_tiles[l]).reshape(t, 256)
        x2d = _merge(x2d, p["g_mix"], oa, [o for o, _ in dil], [ls for _, ls in dil], oc, od,
                     p["w_gate"], p["w_branch"], p["w_out"])
        k_mem, v_mem = _mem_kv(mem, p["g_mem"], p["w_ck"], p["w_cv"])
        x2d, h3, aff_t = _cross(x2d, b, s, p["g_cross"], p["w_cq"], k_mem, v_mem, p["w_co"], p["g_ffn"],
                                p["w_router_t"])
        pos = _select(aff_t, cap)
        g_items, s_items = _work_items(pos, cap)
        pos3 = pos.reshape(N_EXPERTS, 1, t)
        ye = _expert_ffn(g_items, h3, pos3, aff_t.reshape(N_EXPERTS, 1, t), p["w_e_gate"], p["w_e_up"],
                         p["w_e_down"], cap)
        x2d = _expert_scatter(s_items, x2d, pos3, ye, g_final[None], final_norm=(l == len(layers) - 1))
    return x2d.reshape(b, s, d)


def kernel(x_prompt, x_sample, mem_prompt, mem_sample, w_in, g_mix, g_cq, w_uq, g_ckv, w_ukv, lam_q1, lam_k1, lam_q2, lam_k2, g_diff, na_bias, w_gate, w_branch, w_out, g_cross, g_mem, w_cq, w_ck, w_cv, w_co, g_ffn, w_router, w_e_gate, w_e_up, w_e_down, g_final):
    depth = w_in.shape[0]
    layers = [_prep_layer(l, w_in, g_mix, g_cq, w_uq, g_ckv, w_ukv, lam_q1, lam_k1, lam_q2, lam_k2, g_diff,
                          na_bias, w_gate, w_branch, w_out, g_cross, g_mem, w_cq, w_ck, w_cv, w_co, g_ffn,
                          w_router, w_e_gate, w_e_up, w_e_down) for l in range(depth)]
    bias_tiles = [_na_bias_tiles(p["na_bias"]) for p in layers]
    y_prompt = _trunk(x_prompt, mem_prompt, layers, bias_tiles, g_final)
    y_sample = _trunk(x_sample, mem_sample, layers, bias_tiles, g_final)
    return (y_prompt, y_sample)
```

```python
import functools
import math

import jax
import jax.numpy as jnp
from jax import lax
from jax.experimental import pallas as pl
from jax.experimental.pallas import tpu as pltpu

BF16 = jnp.bfloat16
F32 = jnp.float32
I32 = jnp.int32

D_MODEL = 1024
EPS = 1e-6
NEG = -1e30
ROPE_THETA = 500000.0
HEAD_DIM = 64
MLA_HEADS = 4
MLA_Q_RANK = 256
MLA_KV_RANK = 128
MLA_NOPE = 64
MLA_ROPE = 32
MLA_SCALE = (MLA_NOPE + MLA_ROPE) ** -0.5
DIL_GROUPS = ((128, 1), (512, 4), (2048, 16))
DIL_SIDE = 64
DIL_SCALE = HEAD_DIM ** -0.5
DIFF_QK = 32
DIFF_SCALE = DIFF_QK ** -0.5
NA_ROWS = 8
NA_COLS = 16
GRID_W = 64
NA_SCALE = HEAD_DIM ** -0.5
MEM_HEADS = 4
MEM_HEAD_DIM = D_MODEL // MEM_HEADS
MEM_SCALE = MEM_HEAD_DIM ** -0.5
N_EXPERTS = 16
EXPERT_FF = 512
CAPACITY_FACTOR = 2

LANES = 128
VMEM_LIMIT = 56 * 1024 * 1024

Z_CQ = 0
Z_CKV = 256
Z_KPE = 384
Z_HEAD = 512
Z_DIL = 512
Z_DIL_GROUP = 768
Z_QC = 2816
Z_KC = 3072
Z_QD = 3328
Z_KD = 3584
Z_VD = 3840
Z_COLS = 4096
ZR_COLS = Z_COLS - Z_HEAD
LOG2E = math.log2(math.e)

TM = 512
TQ = 256
TK = 512
MLA_UNROLL = 4
DIFF_UNROLL = 2
TU = 128
NA_TQ = 128
NA_WIN_ROWS = 10
NA_BIAS_OFF = 2
NA_BIAS_N = 18
RB = 256
CH = 512
SEL_CH = 512
CNT_CH = 4096


def _params(n_grid):
    return pltpu.CompilerParams(dimension_semantics=("arbitrary",) * n_grid,
                                vmem_limit_bytes=VMEM_LIMIT)


def _rms(x, g):
    return x * lax.rsqrt(jnp.mean(x * x, axis=-1, keepdims=True) + EPS) * g


def _dot(a, b):
    return jnp.dot(a, b, preferred_element_type=F32)


def _dot_nt(a, b):
    return lax.dot_general(a, b, (((1,), (1,)), ((), ())), preferred_element_type=F32)


def _dot_tn(a, b):
    return lax.dot_general(a, b, (((0,), (0,)), ((), ())), preferred_element_type=F32)


def _rope_block(x, c, sa, sb, half):
    return x * c + pltpu.roll(x, LANES - half, 1) * sa + pltpu.roll(x, half, 1) * sb


def _proj_kernel(x_ref, g_ref, w_ref, rope_ref, gcq_ref, gckv_ref, wuq_ref, wk_ref, wvt_ref, wvct_ref,
                 zr_ref, qa_ref, ka_ref, vat_ref, vct_ref):
    hb = _rms(x_ref[...], g_ref[...]).astype(BF16)

    def rope(x, kind, half):
        return _rope_block(x, rope_ref[3 * kind], rope_ref[3 * kind + 1], rope_ref[3 * kind + 2], half)

    z0 = _dot(hb, w_ref[:, 0:Z_HEAD])
    cq = _rms(z0[:, Z_CQ:Z_CQ + MLA_Q_RANK], gcq_ref[...]).astype(BF16)
    ckv = _rms(z0[:, Z_CKV:Z_CKV + MLA_KV_RANK], gckv_ref[...]).astype(BF16)
    kpe = rope(z0[:, Z_KPE:Z_KPE + LANES], 0, MLA_ROPE // 2)
    qa = _dot(cq, wuq_ref[...]) * (MLA_SCALE * LOG2E)
    kn = _dot(ckv, wk_ref[...])
    for h in range(MLA_HEADS):
        sl = slice(h * LANES, (h + 1) * LANES)
        qa_ref[0, h] = rope(qa[:, sl], 0, MLA_ROPE // 2).astype(BF16)
        ka_ref[0, h] = (kn[:, sl] + kpe).astype(BF16)
    vat_ref[0] = _dot_nt(wvt_ref[...], ckv).astype(BF16)
    vct_ref[0] = _dot_nt(wvct_ref[...], hb).astype(BF16)

    def seg(start, scale, kind, half):
        z = _dot(hb, w_ref[:, start:start + 256])
        if scale is not None:
            z = z * scale
        for j in range(2):
            zb = z[:, j * LANES:(j + 1) * LANES]
            if kind is not None:
                zb = rope(zb, kind, half)
            o = start - Z_HEAD + j * LANES
            zr_ref[:, o:o + LANES] = zb.astype(BF16)

    dil_half = HEAD_DIM // 4 // 2
    diff_half = DIFF_QK // 4 // 2
    for g in range(len(DIL_GROUPS)):
        base = Z_DIL + g * Z_DIL_GROUP
        seg(base, DIL_SCALE, 1, dil_half)
        seg(base + 256, None, 1, dil_half)
        seg(base + 512, None, None, 0)
    seg(Z_QC, DIFF_SCALE * LOG2E, 2, diff_half)
    seg(Z_KC, None, 2, diff_half)
    seg(Z_QD, NA_SCALE, None, 0)
    seg(Z_KD, None, None, 0)
    seg(Z_VD, None, None, 0)


def _const_spec(shape):
    nd = len(shape)
    return pl.BlockSpec(shape, lambda *_: (0,) * nd)


def _project(x2d, b, s, g_mix, w_z, rope_tab, g_cq, g_ckv, w_uq, w_k, w_vt, w_vct):
    t = b * s
    n_s = s // TM
    v_cols = MLA_HEADS * HEAD_DIM
    return pl.pallas_call(
        _proj_kernel,
        grid=(t // TM,),
        in_specs=[
            pl.BlockSpec((TM, D_MODEL), lambda i: (i, 0)),
            _const_spec((1, D_MODEL)),
            _const_spec((D_MODEL, Z_COLS)),
            pl.BlockSpec((9, TM, LANES), lambda i: (0, i % n_s, 0)),
            _const_spec((1, MLA_Q_RANK)),
            _const_spec((1, MLA_KV_RANK)),
            _const_spec((MLA_Q_RANK, MLA_HEADS * LANES)),
            _const_spec((MLA_KV_RANK, MLA_HEADS * LANES)),
            _const_spec((v_cols, MLA_KV_RANK)),
            _const_spec((v_cols, D_MODEL)),
        ],
        out_specs=[
            pl.BlockSpec((TM, ZR_COLS), lambda i: (i, 0)),
            pl.BlockSpec((1, MLA_HEADS, TM, LANES), lambda i: (i // n_s, 0, i % n_s, 0)),
            pl.BlockSpec((1, MLA_HEADS, TM, LANES), lambda i: (i // n_s, 0, i % n_s, 0)),
            pl.BlockSpec((1, v_cols, TM), lambda i: (i // n_s, 0, i % n_s)),
            pl.BlockSpec((1, v_cols, TM), lambda i: (i // n_s, 0, i % n_s)),
        ],
        out_shape=[
            jax.ShapeDtypeStruct((t, ZR_COLS), BF16),
            jax.ShapeDtypeStruct((b, MLA_HEADS, s, LANES), BF16),
            jax.ShapeDtypeStruct((b, MLA_HEADS, s, LANES), BF16),
            jax.ShapeDtypeStruct((b, v_cols, s), BF16),
            jax.ShapeDtypeStruct((b, v_cols, s), BF16),
        ],
        compiler_params=_params(1),
        name="proj",
    )(x2d, g_mix, w_z, rope_tab, g_cq, g_ckv, w_uq, w_k, w_vt, w_vct)


def _chunk(c):
    return pl.ds(pl.multiple_of(c * TK, TK), TK)


def _dense_attention(qs, k_at, vt_at, s_ref, p_ref, n_chunks, unroll):
    n = len(qs)

    def step(c, cur, nxt, carry):
        rows_next = _chunk(jnp.minimum(c + 1, n_chunks - 1))
        rows_prev = _chunk(jnp.maximum(c - 1, 0))
        new = []
        for i in range(n):
            m, l, acc, alpha_prev = carry[i]
            s_ref[i, nxt] = _dot_nt(k_at(i, rows_next), qs[i])
            acc = alpha_prev * acc + _dot(vt_at(i, rows_prev), p_ref[i, nxt])
            st = s_ref[i, cur]
            m_new = jnp.maximum(m, jnp.max(st, axis=0, keepdims=True))
            alpha = jnp.exp2(m - m_new)
            p = jnp.exp2(st - m_new)
            l = alpha * l + jnp.sum(p, axis=0, keepdims=True)
            p_ref[i, cur] = p.astype(BF16)
            new.append((m_new, l, acc, alpha))
        return tuple(new)

    def body(j, carry):
        for u in range(unroll):
            carry = step(unroll * j + u, u % 2, (u + 1) % 2, carry)
        return carry

    for i in range(n):
        s_ref[i, 0] = _dot_nt(k_at(i, pl.ds(0, TK)), qs[i])
        p_ref[i, 1] = jnp.zeros((TK, TQ), BF16)
    init = tuple((jnp.full((1, TQ), NEG, F32), jnp.zeros((1, TQ), F32), jnp.zeros((HEAD_DIM, TQ), F32),
                  jnp.ones((1, TQ), F32)) for _ in range(n))
    assert unroll % 2 == 0 and n_chunks % unroll == 0
    res = lax.fori_loop(0, n_chunks // unroll, body, init)
    out = []
    for i in range(n):
        _, l, acc, alpha = res[i]
        acc = alpha * acc + _dot(vt_at(i, pl.ds((n_chunks - 1) * TK, TK)), p_ref[i, 1])
        out.append((acc, l))
    return out


def _mla_kernel(q_ref, k_ref, vt_ref, o_ref, s_ref, p_ref, *, n_chunks):
    res = _dense_attention(
        [q_ref[0, hh] for hh in range(2)],
        lambda i, rows: k_ref[0, i, rows, :],
        lambda i, rows: vt_ref[0, i * HEAD_DIM:(i + 1) * HEAD_DIM, rows],
        s_ref, p_ref, n_chunks, MLA_UNROLL)
    out_t = jnp.concatenate([acc / l for acc, l in res], axis=0)
    o_ref[0] = out_t.T.astype(BF16)


def _mla_attention(qa, ka, vat):
    b, _, s, _ = qa.shape
    return pl.pallas_call(
        functools.partial(_mla_kernel, n_chunks=s // TK),
        grid=(b, 2, s // TQ),
        in_specs=[
            pl.BlockSpec((1, 2, TQ, LANES), lambda bi, p, qi: (bi, p, qi, 0)),
            pl.BlockSpec((1, 2, s, LANES), lambda bi, p, qi: (bi, p, 0, 0)),
            pl.BlockSpec((1, LANES, s), lambda bi, p, qi: (bi, p, 0)),
        ],
        out_specs=pl.BlockSpec((1, TQ, LANES), lambda bi, p, qi: (bi, qi, p)),
        out_shape=jax.ShapeDtypeStruct((b, s, 2 * LANES), BF16),
        scratch_shapes=[pltpu.VMEM((2, 2, TK, TQ), F32), pltpu.VMEM((2, 2, TK, TQ), BF16)],
        compiler_params=_params(3),
        name="mla_attn",
    )(qa, ka, vat)


def _diff_kernel(q_ref, k_ref, vt_ref, lam_ref, g_ref, o_ref, s_ref, p_ref, *, n_chunks, lam_init):
    lane = lax.broadcasted_iota(I32, (TQ, LANES), 1)
    lv = lam_ref[...]
    lam = (jnp.exp(jnp.sum(lv[0:1] * lv[1:2], axis=-1, keepdims=True))
           - jnp.exp(jnp.sum(lv[2:3] * lv[3:4], axis=-1, keepdims=True)) + lam_init)
    q = q_ref[0]
    qs = [jnp.where((lane >= i * DIFF_QK) & (lane < (i + 1) * DIFF_QK), q, jnp.zeros_like(q)) for i in range(4)]
    res = _dense_attention(
        qs,
        lambda i, rows: k_ref[0, rows, :],
        lambda i, rows: vt_ref[0, (i // 2) * HEAD_DIM:(i // 2 + 1) * HEAD_DIM, rows],
        s_ref, p_ref, n_chunks, DIFF_UNROLL)
    parts = [acc / l for acc, l in res]
    out_t = jnp.concatenate([parts[2 * hh] - lam * parts[2 * hh + 1] for hh in range(2)], axis=0)
    o = out_t.T
    head0 = lane < HEAD_DIM
    sq = o * o
    ms0 = jnp.sum(jnp.where(head0, sq, 0.0), axis=-1, keepdims=True)
    ms1 = jnp.sum(jnp.where(head0, 0.0, sq), axis=-1, keepdims=True)
    ms = jnp.where(head0, ms0, ms1) * (1.0 / HEAD_DIM)
    o_ref[0] = (o * lax.rsqrt(ms + EPS) * g_ref[...] * (1.0 - lam_init)).astype(BF16)


def _diff_attention(zr3, vct, lam_vecs, g_diff2, lam_init):
    b, s, _ = zr3.shape
    qb, kb = (Z_QC - Z_HEAD) // LANES, (Z_KC - Z_HEAD) // LANES
    return pl.pallas_call(
        functools.partial(_diff_kernel, n_chunks=s // TK, lam_init=lam_init),
        grid=(b, 2, s // TQ),
        in_specs=[
            pl.BlockSpec((1, TQ, LANES), lambda bi, p, qi: (bi, qi, qb + p)),
            pl.BlockSpec((1, s, LANES), lambda bi, p, qi: (bi, 0, kb + p)),
            pl.BlockSpec((1, LANES, s), lambda bi, p, qi: (bi, p, 0)),
            _const_spec((4, DIFF_QK)),
            _const_spec((1, LANES)),
        ],
        out_specs=pl.BlockSpec((1, TQ, LANES), lambda bi, p, qi: (bi, qi, p)),
        out_shape=jax.ShapeDtypeStruct((b, s, 2 * LANES), BF16),
        scratch_shapes=[pltpu.VMEM((4, 2, TK, TQ), F32), pltpu.VMEM((4, 2, TK, TQ), BF16)],
        compiler_params=_params(3),
        name="diff_attn",
    )(zr3, zr3, vct, lam_vecs, g_diff2)


def _local_heads(q, kw, vw, valid, biases):
    n = q.shape[0]
    lane = lax.broadcasted_iota(I32, (n, 4 * HEAD_DIM), 1)
    in_head = [(lane >= h * HEAD_DIM) & (lane < (h + 1) * HEAD_DIM) for h in range(4)]
    scores = [_dot_nt(jnp.where(in_head[h], q, jnp.zeros_like(q)), kw) for h in range(4)]
    probs, lses = [], []
    for h in range(4):
        s = scores[h] if biases is None else scores[h] + biases[h]
        s = jnp.where(valid, s, NEG)
        m = jnp.max(s, axis=-1, keepdims=True)
        e = jnp.exp(s - m)
        l = jnp.sum(e, axis=-1, keepdims=True)
        probs.append((e / l).astype(BF16))
        lses.append(m + jnp.log(l))
    outs = [_dot(p, vw) for p in probs]
    out, lse = outs[3], lses[3]
    for h in (2, 1, 0):
        out = jnp.where(in_head[h], outs[h], out)
        lse = jnp.where(in_head[h], lses[h], lse)
    return out, jnp.broadcast_to(lse, out.shape)


def _dil_kernel(q_ref, k_ref, v_ref, o_ref, lse_ref, *, u_len, win):
    u0 = pl.program_id(2) * TU
    start = pl.multiple_of(jnp.clip(u0 - DIL_SIDE, 0, u_len - win), DIL_SIDE)
    kw = k_ref[0, pl.ds(start, win), :]
    vw = v_ref[0, pl.ds(start, win), :]
    q = q_ref[0]
    qpos = u0 + lax.broadcasted_iota(I32, (TU, win), 0)
    kpos = start + lax.broadcasted_iota(I32, (TU, win), 1)
    valid = jnp.abs(kpos - qpos) <= DIL_SIDE
    out, lse_out = _local_heads(q, kw, vw, valid, None)
    o_ref[0] = out.astype(BF16)
    lse_ref[0] = lse_out


def _dil_attention(zr3, g, dil):
    b, s, _ = zr3.shape
    u_len = s // dil
    win = TU + 2 * DIL_SIDE
    assert u_len >= win and u_len % TU == 0
    col0 = Z_DIL - Z_HEAD + g * Z_DIL_GROUP
    if dil == 1:
        zv, units, qu = zr3, ZR_COLS // 256, col0 // 256
    else:
        zv = zr3[:, :, col0:col0 + Z_DIL_GROUP].reshape(b, u_len, dil * Z_DIL_GROUP)
        units, qu = Z_DIL_GROUP // 256, 0
    ku, vu = qu + 1, qu + 2
    o, lse = pl.pallas_call(
        functools.partial(_dil_kernel, u_len=u_len, win=win),
        grid=(b, dil, u_len // TU),
        in_specs=[
            pl.BlockSpec((1, TU, 256), lambda bi, r, ui: (bi, ui, r * units + qu)),
            pl.BlockSpec((1, u_len, 256), lambda bi, r, ui: (bi, 0, r * units + ku)),
            pl.BlockSpec((1, u_len, 256), lambda bi, r, ui: (bi, 0, r * units + vu)),
        ],
        out_specs=[
            pl.BlockSpec((1, TU, 256), lambda bi, r, ui: (bi, ui, r)),
            pl.BlockSpec((1, TU, 256), lambda bi, r, ui: (bi, ui, r)),
        ],
        out_shape=[
            jax.ShapeDtypeStruct((b, u_len, dil * 256), BF16),
            jax.ShapeDtypeStruct((b, u_len, dil * 256), F32),
        ],
        compiler_params=_params(3),
        name=f"dil_attn_{dil}",
    )(zv, zv, zv)
    return o.reshape(b * s, 256), lse.reshape(b * s, 256)


def _na_bias_kernel(tbl_ref, o_ref):
    h = pl.program_id(0)
    d = pl.program_id(1)
    n_rel_r, n_rel_c = 2 * NA_ROWS - 1, 2 * NA_COLS - 1
    dl = jnp.clip(d - NA_BIAS_OFF, 0, n_rel_r - 1)
    dr = jnp.clip(d - NA_BIAS_OFF + 1, 0, n_rel_r - 1)
    qc = lax.broadcasted_iota(I32, (GRID_W, LANES), 0)
    ln = lax.broadcasted_iota(I32, (GRID_W, LANES), 1)
    left = ln < GRID_W
    rel = jnp.where(left, ln, ln - GRID_W) - qc + (NA_COLS - 1)
    acc = jnp.zeros((GRID_W, LANES), F32)
    for dd in range(n_rel_c):
        vl = tbl_ref[(h * n_rel_r + dl) * n_rel_c + dd]
        vr = tbl_ref[(h * n_rel_r + dr) * n_rel_c + dd]
        acc = jnp.where(rel == dd, jnp.where(left, vl, vr), acc)
    o_ref[0, 0] = acc


def _na_bias_tiles(na_bias_l):
    n_h = na_bias_l.shape[0]
    return pl.pallas_call(
        _na_bias_kernel,
        grid=(n_h, NA_BIAS_N),
        in_specs=[pl.BlockSpec(memory_space=pltpu.SMEM)],
        out_specs=pl.BlockSpec((1, 1, GRID_W, LANES), lambda h, d: (h, d, 0, 0)),
        out_shape=jax.ShapeDtypeStruct((n_h, NA_BIAS_N, GRID_W, LANES), F32),
        compiler_params=_params(2),
        name="na_bias",
    )(na_bias_l.reshape(-1))


def _na_kernel(q_ref, k_ref, v_ref, bias_ref, o_ref, *, n_rows):
    r0 = pl.program_id(1) * 2
    ws = jnp.clip(r0 - NA_ROWS // 2, 0, n_rows - NA_WIN_ROWS)
    n_keys = NA_WIN_ROWS * GRID_W
    start = pl.multiple_of(ws * GRID_W, GRID_W)
    kw = k_ref[0, pl.ds(start, n_keys), :]
    vw = v_ref[0, pl.ds(start, n_keys), :]
    q = q_ref[0]
    qi = lax.broadcasted_iota(I32, (NA_TQ, n_keys), 0)
    kj = lax.broadcasted_iota(I32, (NA_TQ, n_keys), 1)
    qr, qc = r0 + qi // GRID_W, qi % GRID_W
    kr, kc = ws + kj // GRID_W, kj % GRID_W
    rs = jnp.clip(qr - NA_ROWS // 2, 0, n_rows - NA_ROWS)
    cs = jnp.clip(qc - NA_COLS // 2, 0, GRID_W - NA_COLS)
    valid = (kr >= rs) & (kr < rs + NA_ROWS) & (kc >= cs) & (kc < cs + NA_COLS)
    biases = []
    for h in range(4):
        halves = []
        for half in range(2):
            d0 = ws - r0 - half + (NA_ROWS - 1) + NA_BIAS_OFF
            halves.append(jnp.concatenate(
                [bias_ref[h, pl.ds(d0 + 2 * m, 1)][0] for m in range(NA_WIN_ROWS // 2)], axis=1))
        biases.append(jnp.concatenate(halves, axis=0))
    out, _ = _local_heads(q, kw, vw, valid, biases)
    o_ref[0] = out.astype(BF16)


def _na_attention(zr3, bias_tiles):
    b, s, _ = zr3.shape
    n_rows = s // GRID_W
    assert n_rows >= NA_WIN_ROWS
    qu, ku, vu = (Z_QD - Z_HEAD) // 256, (Z_KD - Z_HEAD) // 256, (Z_VD - Z_HEAD) // 256
    return pl.pallas_call(
        functools.partial(_na_kernel, n_rows=n_rows),
        grid=(b, s // NA_TQ),
        in_specs=[
            pl.BlockSpec((1, NA_TQ, 256), lambda bi, i: (bi, i, qu)),
            pl.BlockSpec((1, s, 256), lambda bi, i: (bi, 0, ku)),
            pl.BlockSpec((1, s, 256), lambda bi, i: (bi, 0, vu)),
            _const_spec(bias_tiles.shape),
        ],
        out_specs=pl.BlockSpec((1, NA_TQ, 256), lambda bi, i: (bi, i, 0)),
        out_shape=jax.ShapeDtypeStruct((b, s, 256), BF16),
        compiler_params=_params(2),
        name="na_attn",
    )(zr3, zr3, zr3, bias_tiles)


def _merge_kernel(x_ref, g_ref, oa_ref, ob0_ref, ob1_ref, ob2_ref, l0_ref, l1_ref, l2_ref, oc_ref, od_ref,
                  wg_ref, wb_ref, wo_ref, y_ref):
    x = x_ref[...]
    hb = _rms(x, g_ref[...]).astype(BF16)
    lses = (l0_ref[...], l1_ref[...], l2_ref[...])
    mx = jnp.maximum(jnp.maximum(lses[0], lses[1]), lses[2])
    ws = [jnp.exp(l - mx) for l in lses]
    den = ws[0] + ws[1] + ws[2]
    obs = (ob0_ref, ob1_ref, ob2_ref)
    ob = (ws[0] / den) * obs[0][...].astype(F32)
    for g in range(1, 3):
        ob = ob + (ws[g] / den) * obs[g][...].astype(F32)
    branches = (oa_ref[...], ob.astype(BF16), oc_ref[...], od_ref[...])
    merged = None
    for i, o in enumerate(branches):
        gate = jax.nn.sigmoid(_dot(hb, wg_ref[i]))
        term = gate * _dot(o, wb_ref[i])
        merged = term if merged is None else merged + term
    y_ref[...] = x + _dot(merged.astype(BF16), wo_ref[...])


def _merge(x2d, g_mix, oa, obs, lses, oc, od, w_gate, w_branch, w_out):
    t = x2d.shape[0]
    tok = lambda w: pl.BlockSpec((TM, w), lambda i: (i, 0))
    return pl.pallas_call(
        _merge_kernel,
        grid=(t // TM,),
        in_specs=[tok(D_MODEL), _const_spec((1, D_MODEL))] + [tok(256)] * 9 + [
            _const_spec(w_gate.shape), _const_spec(w_branch.shape), _const_spec(w_out.shape)],
        out_specs=tok(D_MODEL),
        out_shape=jax.ShapeDtypeStruct((t, D_MODEL), F32),
        compiler_params=_params(1),
        name="merge",
    )(x2d, g_mix, oa, obs[0], obs[1], obs[2], lses[0], lses[1], lses[2], oc, od, w_gate, w_branch, w_out)


def _memkv_kernel(m_ref, g_ref, wk_ref, wv_ref, k_ref, v_ref):
    mb = _rms(m_ref[0], g_ref[...]).astype(BF16)
    k_ref[0] = _dot(mb, wk_ref[...]).astype(BF16)
    v_ref[0] = _dot(mb, wv_ref[...]).astype(BF16)


def _mem_kv(mem, g_mem, w_ck, w_cv):
    b, m_len, d = mem.shape
    blk = pl.BlockSpec((1, m_len, d), lambda bi: (bi, 0, 0))
    return pl.pallas_call(
        _memkv_kernel,
        grid=(b,),
        in_specs=[blk, _const_spec((1, d)), _const_spec((d, d)), _const_spec((d, d))],
        out_specs=[blk, blk],
        out_shape=[jax.ShapeDtypeStruct((b, m_len, d), BF16)] * 2,
        compiler_params=_params(1),
        name="mem_kv",
    )(mem, g_mem, w_ck, w_cv)


def _cross_kernel(x_ref, g_ref, wq_ref, k_ref, v_ref, wo_ref, gf_ref, wr_ref, y_ref, h_ref, aff_ref):
    x = x_ref[...]
    hb = _rms(x, g_ref[...]).astype(BF16)
    q = (_dot(hb, wq_ref[...]) * MEM_SCALE).astype(BF16)
    heads = [slice(h * MEM_HEAD_DIM, (h + 1) * MEM_HEAD_DIM) for h in range(MEM_HEADS)]
    scores = [_dot_nt(q[:, sl], k_ref[0, :, sl]) for sl in heads]
    probs = []
    for s in scores:
        m = jnp.max(s, axis=-1, keepdims=True)
        e = jnp.exp(s - m)
        probs.append((e / jnp.sum(e, axis=-1, keepdims=True)).astype(BF16))
    outs = [_dot(p, v_ref[0, :, sl]).astype(BF16) for p, sl in zip(probs, heads)]
    y = x + _dot(jnp.concatenate(outs, axis=1), wo_ref[...])
    y_ref[...] = y
    h3 = _rms(y, gf_ref[...]).astype(BF16)
    h_ref[...] = h3
    logits = _dot_nt(wr_ref[...], h3)
    m = jnp.max(logits, axis=0, keepdims=True)
    e = jnp.exp(logits - m)
    aff_ref[...] = e / jnp.sum(e, axis=0, keepdims=True)


def _cross(x2d, b, s, g_cross, w_cq, k_mem, v_mem, w_co, g_ffn, w_router_t):
    t = b * s
    n_s = s // TM
    m_len = k_mem.shape[1]
    tok = lambda: pl.BlockSpec((TM, D_MODEL), lambda i: (i, 0))
    mem_blk = pl.BlockSpec((1, m_len, D_MODEL), lambda i: (i // n_s, 0, 0))
    return pl.pallas_call(
        _cross_kernel,
        grid=(t // TM,),
        in_specs=[tok(), _const_spec((1, D_MODEL)), _const_spec((D_MODEL, D_MODEL)), mem_blk, mem_blk,
                  _const_spec((D_MODEL, D_MODEL)), _const_spec((1, D_MODEL)),
                  _const_spec((N_EXPERTS, D_MODEL))],
        out_specs=[tok(), tok(), pl.BlockSpec((N_EXPERTS, TM), lambda i: (0, i))],
        out_shape=[jax.ShapeDtypeStruct((t, D_MODEL), F32), jax.ShapeDtypeStruct((t, D_MODEL), BF16),
                   jax.ShapeDtypeStruct((N_EXPERTS, t), F32)],
        compiler_params=_params(1),
        name="cross",
    )(x2d, g_cross, w_cq, k_mem, v_mem, w_co, g_ffn, w_router_t)


def _select_kernel(aff_ref, pos_ref, *, n_tok, cap):
    n_cnt = n_tok // CNT_CH

    def count_ge(cand):
        def body(c, acc):
            bits = pltpu.bitcast(aff_ref[:, pl.ds(pl.multiple_of(c * CNT_CH, CNT_CH), CNT_CH)], I32)
            return acc + jnp.sum(jnp.where(bits >= cand, 1.0, 0.0), axis=1, keepdims=True)
        return lax.fori_loop(0, n_cnt, body, jnp.zeros((N_EXPERTS, 1), F32))

    def bit_body(i, prefix):
        cand = prefix | lax.shift_left(jnp.int32(1), 30 - i)
        return jnp.where(count_ge(cand) >= cap, cand, prefix)

    thr = lax.fori_loop(0, 31, bit_body, jnp.zeros((N_EXPERTS, 1), I32))
    need_eq = cap - count_ge(thr + 1)

    row = lax.broadcasted_iota(I32, (SEL_CH, SEL_CH), 0)
    col = lax.broadcasted_iota(I32, (SEL_CH, SEL_CH), 1)
    tri = jnp.where(row <= col, 1.0, 0.0).astype(BF16)

    def chunk_body(c, carry):
        c_eq, c_sel = carry
        sl = pl.ds(pl.multiple_of(c * SEL_CH, SEL_CH), SEL_CH)
        bits = pltpu.bitcast(aff_ref[:, sl], I32)
        eq = jnp.where(bits == thr, 1.0, 0.0)
        gt = jnp.where(bits > thr, 1.0, 0.0)
        eq_before = c_eq + _dot(eq.astype(BF16), tri) - eq
        sel = gt + eq * jnp.where(eq_before < need_eq, 1.0, 0.0)
        rank = c_sel + _dot(sel.astype(BF16), tri) - sel
        pos_ref[:, sl] = jnp.where(sel > 0.5, rank, -1.0).astype(I32)
        return (c_eq + jnp.sum(eq, axis=1, keepdims=True), c_sel + jnp.sum(sel, axis=1, keepdims=True))

    zero = jnp.zeros((N_EXPERTS, 1), F32)
    lax.fori_loop(0, n_tok // SEL_CH, chunk_body, (zero, zero))


def _select(aff_t, cap):
    n_tok = aff_t.shape[1]
    return pl.pallas_call(
        functools.partial(_select_kernel, n_tok=n_tok, cap=cap),
        grid=(1,),
        in_specs=[_const_spec(aff_t.shape)],
        out_specs=_const_spec(aff_t.shape),
        out_shape=jax.ShapeDtypeStruct(aff_t.shape, I32),
        compiler_params=_params(1),
        name="select",
    )(aff_t)


def _one_hot_rows(pos_row, j):
    rows = j * RB + lax.broadcasted_iota(I32, (RB, CH), 0)
    return pos_row == rows


def _ffn_kernel(ie_ref, ij_ref, ik_ref, first_ref, last_ref, valid_ref,
                h_ref, pos_ref, aff_ref, wg_ref, wu_ref, wd_ref, ye_ref, acc_ref, gate_ref):
    w = pl.program_id(0)

    @pl.when(first_ref[w] == 1)
    def _():
        acc_ref[...] = jnp.zeros_like(acc_ref)
        gate_ref[...] = jnp.zeros_like(gate_ref)

    @pl.when(valid_ref[w] == 1)
    def _():
        hit = _one_hot_rows(pos_ref[0], ij_ref[w])
        acc_ref[...] += _dot(jnp.where(hit, 1.0, 0.0).astype(BF16), h_ref[...])
        gate_ref[...] += jnp.sum(jnp.where(hit, aff_ref[0], 0.0), axis=1, keepdims=True)

    @pl.when(last_ref[w] == 1)
    def _():
        xe = acc_ref[...].astype(BF16)
        hid = jax.nn.silu(_dot(xe, wg_ref[0])) * _dot(xe, wu_ref[0])
        ye_ref[0] = (_dot(hid.astype(BF16), wd_ref[0]) * gate_ref[...]).astype(BF16)


def _expert_ffn(items, h3, pos3, aff3, w_e_gate, w_e_up, w_e_down, cap):
    n_items = items[0].shape[0]
    grid_spec = pltpu.PrefetchScalarGridSpec(
        num_scalar_prefetch=6,
        grid=(n_items,),
        in_specs=[
            pl.BlockSpec((CH, D_MODEL), lambda w, ie, ij, ik, *_: (ik[w], 0)),
            pl.BlockSpec((1, 1, CH), lambda w, ie, ij, ik, *_: (ie[w], 0, ik[w])),
            pl.BlockSpec((1, 1, CH), lambda w, ie, ij, ik, *_: (ie[w], 0, ik[w])),
            pl.BlockSpec((1, D_MODEL, EXPERT_FF), lambda w, ie, *_: (ie[w], 0, 0)),
            pl.BlockSpec((1, D_MODEL, EXPERT_FF), lambda w, ie, *_: (ie[w], 0, 0)),
            pl.BlockSpec((1, EXPERT_FF, D_MODEL), lambda w, ie, *_: (ie[w], 0, 0)),
        ],
        out_specs=pl.BlockSpec((1, RB, D_MODEL), lambda w, ie, ij, *_: (ie[w], ij[w], 0)),
        scratch_shapes=[pltpu.VMEM((RB, D_MODEL), F32), pltpu.VMEM((RB, 1), F32)],
    )
    return pl.pallas_call(
        _ffn_kernel,
        grid_spec=grid_spec,
        out_shape=jax.ShapeDtypeStruct((N_EXPERTS, cap, D_MODEL), BF16),
        compiler_params=_params(1),
        name="expert_ffn",
    )(*items, h3, pos3, aff3, w_e_gate, w_e_up, w_e_down)


def _scatter_kernel(ik_ref, ie_ref, ij_ref, first_ref, last_ref, valid_ref,
                    x_ref, pos_ref, ye_ref, gfin_ref, y_ref, *, final_norm):
    w = pl.program_id(0)

    @pl.when(first_ref[w] == 1)
    def _():
        y_ref[...] = x_ref[...]

    @pl.when(valid_ref[w] == 1)
    def _():
        hit = _one_hot_rows(pos_ref[0], ij_ref[w])
        y_ref[...] += _dot_tn(jnp.where(hit, 1.0, 0.0).astype(BF16), ye_ref[0])

    if final_norm:
        @pl.when(last_ref[w] == 1)
        def _():
            y_ref[...] = _rms(y_ref[...], gfin_ref[...])


def _expert_scatter(items, x2d, pos3, ye, g_final, final_norm):
    n_items = items[0].shape[0]
    t = x2d.shape[0]
    grid_spec = pltpu.PrefetchScalarGridSpec(
        num_scalar_prefetch=6,
        grid=(n_items,),
        in_specs=[
            pl.BlockSpec((CH, D_MODEL), lambda w, ik, *_: (ik[w], 0)),
            pl.BlockSpec((1, 1, CH), lambda w, ik, ie, *_: (ie[w], 0, ik[w])),
            pl.BlockSpec((1, RB, D_MODEL), lambda w, ik, ie, ij, *_: (ie[w], ij[w], 0)),
            pl.BlockSpec((1, D_MODEL), lambda w, *_: (0, 0)),
        ],
        out_specs=pl.BlockSpec((CH, D_MODEL), lambda w, ik, *_: (ik[w], 0)),
    )
    return pl.pallas_call(
        functools.partial(_scatter_kernel, final_norm=final_norm),
        grid_spec=grid_spec,
        out_shape=jax.ShapeDtypeStruct((t, D_MODEL), F32),
        compiler_params=_params(1),
        name="expert_scatter",
    )(*items, x2d, pos3, ye, g_final)


def _flatten_items(n_per_block, n_max):
    off_incl = jnp.cumsum(n_per_block)
    off_excl = off_incl - n_per_block
    total = off_incl[-1]
    w = jnp.arange(n_max, dtype=I32)
    valid = w < total
    wv = jnp.where(valid, w, total - 1)
    blk = jnp.searchsorted(off_incl, wv, side="right").astype(I32)
    return blk, wv - off_excl[blk], wv, valid, off_excl, off_incl


def _work_items(pos, cap):
    n_e, n_tok = pos.shape
    n_k, n_j = n_tok // CH, cap // RB
    cnt = jnp.sum((pos >= 0).reshape(n_e, n_k, CH), axis=-1, dtype=I32)
    ends = jnp.cumsum(cnt, axis=1)
    begins = ends - cnt

    jb = jnp.arange(n_j, dtype=I32) * RB
    k_first = jax.vmap(lambda r: jnp.searchsorted(r, jb, side="right"))(ends).astype(I32)
    k_last = jax.vmap(lambda r: jnp.searchsorted(r, jb + RB, side="left"))(ends).astype(I32)
    n_blk = (k_last - k_first + 1).reshape(-1)
    blk, off, wv, valid, off_excl, off_incl = _flatten_items(n_blk, n_e * (n_j + n_k))
    g_items = (blk // n_j, blk % n_j, k_first.reshape(-1)[blk] + off,
               (valid & (wv == off_excl[blk])).astype(I32),
               (valid & (wv == off_incl[blk] - 1)).astype(I32), valid.astype(I32))

    has = cnt > 0
    j_first = jnp.minimum(begins // RB, n_j - 1)
    j_last = jnp.where(has, (ends - 1) // RB, j_first)
    always = (jnp.arange(n_e) == 0)[:, None]
    n_ke = jnp.where(has | always, j_last - j_first + 1, 0).T.reshape(-1)
    blk, off, wv, valid, off_excl, off_incl = _flatten_items(n_ke, n_e * (n_j + n_k))
    k_of = blk // n_e
    chunk_first = off_excl[k_of * n_e]
    chunk_end = off_incl[k_of * n_e + n_e - 1]
    s_items = (k_of, blk % n_e, j_first.T.reshape(-1)[blk] + off,
               (valid & (wv == chunk_first)).astype(I32),
               (valid & (wv == chunk_end - 1)).astype(I32), valid.astype(I32))
    return tuple(a.astype(I32) for a in g_items), tuple(a.astype(I32) for a in s_items)


def _rope_tables(s):
    pos = jnp.arange(s, dtype=F32)[:, None]
    lane = jnp.arange(LANES)
    tabs = []
    for period, base, rot in ((LANES, MLA_NOPE, MLA_ROPE), (HEAD_DIM, 0, HEAD_DIM // 4), (DIFF_QK, 0, DIFF_QK // 4)):
        half = rot // 2
        rel = lane % period - base
        inv_freq = ROPE_THETA ** (-jnp.arange(half, dtype=F32) * (2.0 / rot))
        ang = pos * inv_freq[None, :]
        cos, sin = jnp.cos(ang), jnp.sin(ang)
        idx = jnp.clip(jnp.where(rel >= half, rel - half, rel), 0, half - 1)
        first = (rel >= 0) & (rel < half)
        second = (rel >= half) & (rel < rot)
        tabs.append(jnp.where(first | second, cos[:, idx], 1.0))
        tabs.append(jnp.where(first, -sin[:, idx], 0.0))
        tabs.append(jnp.where(second, sin[:, idx], 0.0))
    return jnp.stack(tabs).astype(F32)


def _prep_layer(l, w_in, g_mix, g_cq, w_uq, g_ckv, w_ukv, lam_q1, lam_k1, lam_q2, lam_k2, g_diff, na_bias,
                w_gate, w_branch, w_out, g_cross, g_mem, w_cq, w_ck, w_cv, w_co, g_ffn, w_router,
                w_e_gate, w_e_up, w_e_down):
    d = D_MODEL
    wi = w_in[l]
    kpe0 = MLA_Q_RANK + MLA_KV_RANK
    z64 = jnp.zeros((d, MLA_NOPE), F32)
    z32 = jnp.zeros((d, LANES - MLA_NOPE - MLA_ROPE), F32)
    rest = wi[:, kpe0 + MLA_ROPE:]
    vc0 = Z_KC + 256 - Z_HEAD
    n_g = len(DIL_GROUPS)
    dil_cols = rest[:, :n_g * Z_DIL_GROUP].reshape(d, 3, n_g, 256).transpose(0, 2, 1, 3).reshape(d, n_g * Z_DIL_GROUP)
    w_z = jnp.concatenate([wi[:, :kpe0], z64, wi[:, kpe0:kpe0 + MLA_ROPE], z32, dil_cols,
                           rest[:, n_g * Z_DIL_GROUP:vc0], rest[:, vc0 + 256:]], axis=1)
    w_vct = rest[:, vc0:vc0 + 256].T
    qh = MLA_NOPE + MLA_ROPE
    wuq = w_uq[l].reshape(MLA_Q_RANK, MLA_HEADS, qh)
    wuq = jnp.pad(wuq, ((0, 0), (0, 0), (0, LANES - qh))).reshape(MLA_Q_RANK, MLA_HEADS * LANES)
    wukv = w_ukv[l].reshape(MLA_KV_RANK, MLA_HEADS, MLA_NOPE + HEAD_DIM)
    wk = jnp.pad(wukv[:, :, :MLA_NOPE], ((0, 0), (0, 0), (0, LANES - MLA_NOPE))).reshape(MLA_KV_RANK, MLA_HEADS * LANES)
    wv = wukv[:, :, MLA_NOPE:].reshape(MLA_KV_RANK, MLA_HEADS * HEAD_DIM)
    return dict(
        g_mix=g_mix[l][None], w_z=w_z.astype(BF16), g_cq=g_cq[l][None], g_ckv=g_ckv[l][None],
        w_uq=wuq.astype(BF16), w_k=wk.astype(BF16), w_vt=wv.T.astype(BF16), w_vct=w_vct.astype(BF16),
        lam=jnp.stack([lam_q1[l], lam_k1[l], lam_q2[l], lam_k2[l]]).astype(F32),
        g_diff=jnp.tile(g_diff[l], 2)[None], na_bias=na_bias[l],
        w_gate=w_gate[l].astype(BF16), w_branch=w_branch[l].astype(BF16), w_out=w_out[l].astype(BF16),
        g_cross=g_cross[l][None], g_mem=g_mem[l][None], w_cq=w_cq[l].astype(BF16), w_ck=w_ck[l].astype(BF16),
        w_cv=w_cv[l].astype(BF16), w_co=w_co[l].astype(BF16), g_ffn=g_ffn[l][None],
        w_router_t=w_router[l].T.astype(BF16), w_e_gate=w_e_gate[l].astype(BF16),
        w_e_up=w_e_up[l].astype(BF16), w_e_down=w_e_down[l].astype(BF16),
    )


def _trunk(x, mem, layers, bias_tiles, g_final):
    b, s, d = x.shape
    t = b * s
    cap = CAPACITY_FACTOR * t // N_EXPERTS
    assert s % TM == 0 and s % TQ == 0 and s % TK == 0 and t % CH == 0 and cap % RB == 0
    rope_tab = _rope_tables(s)
    x2d = x.reshape(t, d)
    for l, p in enumerate(layers):
        zr, qa, ka, vat, vct = _project(x2d, b, s, p["g_mix"], p["w_z"], rope_tab, p["g_cq"], p["g_ckv"],
                                        p["w_uq"], p["w_k"], p["w_vt"], p["w_vct"])
        zr3 = zr.reshape(b, s, ZR_COLS)
        oa = _mla_attention(qa, ka, vat).reshape(t, 256)
        dil = [_dil_attention(zr3, g, dl) for g, (_, dl) in enumerate(DIL_GROUPS)]
        lam_init = 0.8 - 0.6 * math.exp(-0.3 * l)
        oc = _diff_attention(zr3, vct, p["lam"], p["g_diff"], lam_init).reshape(t, 256)
        od = _na_attention(zr3, bias_tiles[l]).reshape(t, 256)
        x2d = _merge(x2d, p["g_mix"], oa, [o for o, _ in dil], [ls for _, ls in dil], oc, od,
                     p["w_gate"], p["w_branch"], p["w_out"])
        k_mem, v_mem = _mem_kv(mem, p["g_mem"], p["w_ck"], p["w_cv"])
        x2d, h3, aff_t = _cross(x2d, b, s, p["g_cross"], p["w_cq"], k_mem, v_mem, p["w_co"], p["g_ffn"],
                                p["w_router_t"])
        pos = _select(aff_t, cap)
        g_items, s_items = _work_items(pos, cap)
        pos3 = pos.reshape(N_EXPERTS, 1, t)
        ye = _expert_ffn(g_items, h3, pos3, aff_t.reshape(N_EXPERTS, 1, t), p["w_e_gate"], p["w_e_up"],
                         p["w_e_down"], cap)
        x2d = _expert_scatter(s_items, x2d, pos3, ye, g_final[None], final_norm=(l == len(layers) - 1))
    return x2d.reshape(b, s, d)


def kernel(x_prompt, x_sample, mem_prompt, mem_sample, w_in, g_mix, g_cq, w_uq, g_ckv, w_ukv, lam_q1, lam_k1, lam_q2, lam_k2, g_diff, na_bias, w_gate, w_branch, w_out, g_cross, g_mem, w_cq, w_ck, w_cv, w_co, g_ffn, w_router, w_e_gate, w_e_up, w_e_down, g_final):
    depth = w_in.shape[0]
    layers = [_prep_layer(l, w_in, g_mix, g_cq, w_uq, g_ckv, w_ukv, lam_q1, lam_k1, lam_q2, lam_k2, g_diff,
                          na_bias, w_gate, w_branch, w_out, g_cross, g_mem, w_cq, w_ck, w_cv, w_co, g_ffn,
                          w_router, w_e_gate, w_e_up, w_e_down) for l in range(depth)]
    bias_tiles = [_na_bias_tiles(p["na_bias"]) for p in layers]
    y_prompt = _trunk(x_prompt, mem_prompt, layers, bias_tiles, g_final)
    y_sample = _trunk(x_sample, mem_sample, layers, bias_tiles, g_final)
    return (y_prompt, y_sample)
```

```python
import functools
import math

import jax
import jax.numpy as jnp
from jax import lax
from jax.experimental import pallas as pl
from jax.experimental.pallas import tpu as pltpu

BF16 = jnp.bfloat16
F32 = jnp.float32
I32 = jnp.int32

D_MODEL = 1024
EPS = 1e-6
NEG = -1e30
ROPE_THETA = 500000.0
HEAD_DIM = 64
MLA_HEADS = 4
MLA_Q_RANK = 256
MLA_KV_RANK = 128
MLA_NOPE = 64
MLA_ROPE = 32
MLA_SCALE = (MLA_NOPE + MLA_ROPE) ** -0.5
DIL_GROUPS = ((128, 1), (512, 4), (2048, 16))
DIL_SIDE = 64
DIL_SCALE = HEAD_DIM ** -0.5
DIFF_QK = 32
DIFF_SCALE = DIFF_QK ** -0.5
NA_ROWS = 8
NA_COLS = 16
GRID_W = 64
NA_SCALE = HEAD_DIM ** -0.5
MEM_HEADS = 4
MEM_HEAD_DIM = D_MODEL // MEM_HEADS
MEM_SCALE = MEM_HEAD_DIM ** -0.5
N_EXPERTS = 16
EXPERT_FF = 512
CAPACITY_FACTOR = 2

LANES = 128
VMEM_LIMIT = 56 * 1024 * 1024

Z_CQ = 0
Z_CKV = 256
Z_KPE = 384
Z_HEAD = 512
Z_DIL = 512
Z_DIL_GROUP = 768
Z_QC = 2816
Z_KC = 3072
Z_QD = 3328
Z_KD = 3584
Z_VD = 3840
Z_COLS = 4096
ZR_COLS = Z_COLS - Z_HEAD
LOG2E = math.log2(math.e)

TM = 512
TQ = 256
TK = 512
MLA_UNROLL = 4
DIFF_UNROLL = 2
TU = 128
NA_TQ = 128
NA_WIN_ROWS = 10
NA_BIAS_OFF = 2
NA_BIAS_N = 18
RB = 256
CH = 512
WIN = 128
ROW_ALIGN = 16
GATHER_GROUP = 4
CNT_CH = 4096


def _params(n_grid):
    return pltpu.CompilerParams(dimension_semantics=("arbitrary",) * n_grid,
                                vmem_limit_bytes=VMEM_LIMIT)


def _rms(x, g):
    return x * lax.rsqrt(jnp.mean(x * x, axis=-1, keepdims=True) + EPS) * g


def _dot(a, b):
    return jnp.dot(a, b, preferred_element_type=F32)


def _dot_nt(a, b):
    return lax.dot_general(a, b, (((1,), (1,)), ((), ())), preferred_element_type=F32)


def _dot_tn(a, b):
    return lax.dot_general(a, b, (((0,), (0,)), ((), ())), preferred_element_type=F32)


def _rope_block(x, c, sa, sb, half):
    return x * c + pltpu.roll(x, LANES - half, 1) * sa + pltpu.roll(x, half, 1) * sb


def _proj_kernel(x_ref, g_ref, w_ref, rope_ref, gcq_ref, gckv_ref, wuq_ref, wk_ref, wvt_ref, wvct_ref,
                 zr_ref, qa_ref, ka_ref, vat_ref, vct_ref):
    hb = _rms(x_ref[...], g_ref[...]).astype(BF16)

    def rope(x, kind, half):
        return _rope_block(x, rope_ref[3 * kind], rope_ref[3 * kind + 1], rope_ref[3 * kind + 2], half)

    z0 = _dot(hb, w_ref[:, 0:Z_HEAD])
    cq = _rms(z0[:, Z_CQ:Z_CQ + MLA_Q_RANK], gcq_ref[...]).astype(BF16)
    ckv = _rms(z0[:, Z_CKV:Z_CKV + MLA_KV_RANK], gckv_ref[...]).astype(BF16)
    kpe = rope(z0[:, Z_KPE:Z_KPE + LANES], 0, MLA_ROPE // 2)
    qa = _dot(cq, wuq_ref[...]) * (MLA_SCALE * LOG2E)
    kn = _dot(ckv, wk_ref[...])
    for h in range(MLA_HEADS):
        sl = slice(h * LANES, (h + 1) * LANES)
        qa_ref[0, h] = rope(qa[:, sl], 0, MLA_ROPE // 2).astype(BF16)
        ka_ref[0, h] = (kn[:, sl] + kpe).astype(BF16)
    vat_ref[0] = _dot_nt(wvt_ref[...], ckv).astype(BF16)
    vct_ref[0] = _dot_nt(wvct_ref[...], hb).astype(BF16)

    def seg(start, scale, kind, half):
        z = _dot(hb, w_ref[:, start:start + 256])
        if scale is not None:
            z = z * scale
        for j in range(2):
            zb = z[:, j * LANES:(j + 1) * LANES]
            if kind is not None:
                zb = rope(zb, kind, half)
            o = start - Z_HEAD + j * LANES
            zr_ref[:, o:o + LANES] = zb.astype(BF16)

    dil_half = HEAD_DIM // 4 // 2
    diff_half = DIFF_QK // 4 // 2
    for g in range(len(DIL_GROUPS)):
        base = Z_DIL + g * Z_DIL_GROUP
        seg(base, DIL_SCALE, 1, dil_half)
        seg(base + 256, None, 1, dil_half)
        seg(base + 512, None, None, 0)
    seg(Z_QC, DIFF_SCALE * LOG2E, 2, diff_half)
    seg(Z_KC, None, 2, diff_half)
    seg(Z_QD, NA_SCALE, None, 0)
    seg(Z_KD, None, None, 0)
    seg(Z_VD, None, None, 0)


def _const_spec(shape):
    nd = len(shape)
    return pl.BlockSpec(shape, lambda *_: (0,) * nd)


def _project(x2d, b, s, g_mix, w_z, rope_tab, g_cq, g_ckv, w_uq, w_k, w_vt, w_vct):
    t = b * s
    n_s = s // TM
    v_cols = MLA_HEADS * HEAD_DIM
    return pl.pallas_call(
        _proj_kernel,
        grid=(t // TM,),
        in_specs=[
            pl.BlockSpec((TM, D_MODEL), lambda i: (i, 0)),
            _const_spec((1, D_MODEL)),
            _const_spec((D_MODEL, Z_COLS)),
            pl.BlockSpec((9, TM, LANES), lambda i: (0, i % n_s, 0)),
            _const_spec((1, MLA_Q_RANK)),
            _const_spec((1, MLA_KV_RANK)),
            _const_spec((MLA_Q_RANK, MLA_HEADS * LANES)),
            _const_spec((MLA_KV_RANK, MLA_HEADS * LANES)),
            _const_spec((v_cols, MLA_KV_RANK)),
            _const_spec((v_cols, D_MODEL)),
        ],
        out_specs=[
            pl.BlockSpec((TM, ZR_COLS), lambda i: (i, 0)),
            pl.BlockSpec((1, MLA_HEADS, TM, LANES), lambda i: (i // n_s, 0, i % n_s, 0)),
            pl.BlockSpec((1, MLA_HEADS, TM, LANES), lambda i: (i // n_s, 0, i % n_s, 0)),
            pl.BlockSpec((1, v_cols, TM), lambda i: (i // n_s, 0, i % n_s)),
            pl.BlockSpec((1, v_cols, TM), lambda i: (i // n_s, 0, i % n_s)),
        ],
        out_shape=[
            jax.ShapeDtypeStruct((t, ZR_COLS), BF16),
            jax.ShapeDtypeStruct((b, MLA_HEADS, s, LANES), BF16),
            jax.ShapeDtypeStruct((b, MLA_HEADS, s, LANES), BF16),
            jax.ShapeDtypeStruct((b, v_cols, s), BF16),
            jax.ShapeDtypeStruct((b, v_cols, s), BF16),
        ],
        compiler_params=_params(1),
        name="proj",
    )(x2d, g_mix, w_z, rope_tab, g_cq, g_ckv, w_uq, w_k, w_vt, w_vct)


def _chunk(c):
    return pl.ds(pl.multiple_of(c * TK, TK), TK)


def _dense_attention(qs, k_at, vt_at, s_ref, p_ref, n_chunks, unroll):
    n = len(qs)

    def step(c, cur, nxt, carry):
        rows_next = _chunk(jnp.minimum(c + 1, n_chunks - 1))
        rows_prev = _chunk(jnp.maximum(c - 1, 0))
        new = []
        for i in range(n):
            m, l, acc, alpha_prev = carry[i]
            s_ref[i, nxt] = _dot_nt(k_at(i, rows_next), qs[i])
            acc = alpha_prev * acc + _dot(vt_at(i, rows_prev), p_ref[i, nxt])
            st = s_ref[i, cur]
            m_new = jnp.maximum(m, jnp.max(st, axis=0, keepdims=True))
            alpha = jnp.exp2(m - m_new)
            p = jnp.exp2(st - m_new)
            l = alpha * l + jnp.sum(p, axis=0, keepdims=True)
            p_ref[i, cur] = p.astype(BF16)
            new.append((m_new, l, acc, alpha))
        return tuple(new)

    def body(j, carry):
        for u in range(unroll):
            carry = step(unroll * j + u, u % 2, (u + 1) % 2, carry)
        return carry

    for i in range(n):
        s_ref[i, 0] = _dot_nt(k_at(i, pl.ds(0, TK)), qs[i])
        p_ref[i, 1] = jnp.zeros((TK, TQ), BF16)
    init = tuple((jnp.full((1, TQ), NEG, F32), jnp.zeros((1, TQ), F32), jnp.zeros((HEAD_DIM, TQ), F32),
                  jnp.ones((1, TQ), F32)) for _ in range(n))
    assert unroll % 2 == 0 and n_chunks % unroll == 0
    res = lax.fori_loop(0, n_chunks // unroll, body, init)
    out = []
    for i in range(n):
        _, l, acc, alpha = res[i]
        acc = alpha * acc + _dot(vt_at(i, pl.ds((n_chunks - 1) * TK, TK)), p_ref[i, 1])
        out.append((acc, l))
    return out


def _mla_kernel(q_ref, k_ref, vt_ref, o_ref, s_ref, p_ref, *, n_chunks):
    res = _dense_attention(
        [q_ref[0, hh] for hh in range(2)],
        lambda i, rows: k_ref[0, i, rows, :],
        lambda i, rows: vt_ref[0, i * HEAD_DIM:(i + 1) * HEAD_DIM, rows],
        s_ref, p_ref, n_chunks, MLA_UNROLL)
    out_t = jnp.concatenate([acc / l for acc, l in res], axis=0)
    o_ref[0] = out_t.T.astype(BF16)


def _mla_attention(qa, ka, vat):
    b, _, s, _ = qa.shape
    return pl.pallas_call(
        functools.partial(_mla_kernel, n_chunks=s // TK),
        grid=(b, 2, s // TQ),
        in_specs=[
            pl.BlockSpec((1, 2, TQ, LANES), lambda bi, p, qi: (bi, p, qi, 0)),
            pl.BlockSpec((1, 2, s, LANES), lambda bi, p, qi: (bi, p, 0, 0)),
            pl.BlockSpec((1, LANES, s), lambda bi, p, qi: (bi, p, 0)),
        ],
        out_specs=pl.BlockSpec((1, TQ, LANES), lambda bi, p, qi: (bi, qi, p)),
        out_shape=jax.ShapeDtypeStruct((b, s, 2 * LANES), BF16),
        scratch_shapes=[pltpu.VMEM((2, 2, TK, TQ), F32), pltpu.VMEM((2, 2, TK, TQ), BF16)],
        compiler_params=_params(3),
        name="mla_attn",
    )(qa, ka, vat)


def _diff_kernel(q_ref, k_ref, vt_ref, lam_ref, g_ref, o_ref, s_ref, p_ref, *, n_chunks, lam_init):
    lane = lax.broadcasted_iota(I32, (TQ, LANES), 1)
    lv = lam_ref[...]
    lam = (jnp.exp(jnp.sum(lv[0:1] * lv[1:2], axis=-1, keepdims=True))
           - jnp.exp(jnp.sum(lv[2:3] * lv[3:4], axis=-1, keepdims=True)) + lam_init)
    q = q_ref[0]
    qs = [jnp.where((lane >= i * DIFF_QK) & (lane < (i + 1) * DIFF_QK), q, jnp.zeros_like(q)) for i in range(4)]
    res = _dense_attention(
        qs,
        lambda i, rows: k_ref[0, rows, :],
        lambda i, rows: vt_ref[0, (i // 2) * HEAD_DIM:(i // 2 + 1) * HEAD_DIM, rows],
        s_ref, p_ref, n_chunks, DIFF_UNROLL)
    parts = [acc / l for acc, l in res]
    out_t = jnp.concatenate([parts[2 * hh] - lam * parts[2 * hh + 1] for hh in range(2)], axis=0)
    o = out_t.T
    head0 = lane < HEAD_DIM
    sq = o * o
    ms0 = jnp.sum(jnp.where(head0, sq, 0.0), axis=-1, keepdims=True)
    ms1 = jnp.sum(jnp.where(head0, 0.0, sq), axis=-1, keepdims=True)
    ms = jnp.where(head0, ms0, ms1) * (1.0 / HEAD_DIM)
    o_ref[0] = (o * lax.rsqrt(ms + EPS) * g_ref[...] * (1.0 - lam_init)).astype(BF16)


def _diff_attention(zr3, vct, lam_vecs, g_diff2, lam_init):
    b, s, _ = zr3.shape
    qb, kb = (Z_QC - Z_HEAD) // LANES, (Z_KC - Z_HEAD) // LANES
    return pl.pallas_call(
        functools.partial(_diff_kernel, n_chunks=s // TK, lam_init=lam_init),
        grid=(b, 2, s // TQ),
        in_specs=[
            pl.BlockSpec((1, TQ, LANES), lambda bi, p, qi: (bi, qi, qb + p)),
            pl.BlockSpec((1, s, LANES), lambda bi, p, qi: (bi, 0, kb + p)),
            pl.BlockSpec((1, LANES, s), lambda bi, p, qi: (bi, p, 0)),
            _const_spec((4, DIFF_QK)),
            _const_spec((1, LANES)),
        ],
        out_specs=pl.BlockSpec((1, TQ, LANES), lambda bi, p, qi: (bi, qi, p)),
        out_shape=jax.ShapeDtypeStruct((b, s, 2 * LANES), BF16),
        scratch_shapes=[pltpu.VMEM((4, 2, TK, TQ), F32), pltpu.VMEM((4, 2, TK, TQ), BF16)],
        compiler_params=_params(3),
        name="diff_attn",
    )(zr3, zr3, vct, lam_vecs, g_diff2)


def _local_heads(q, kw, vw, valid, biases):
    n = q.shape[0]
    lane = lax.broadcasted_iota(I32, (n, 4 * HEAD_DIM), 1)
    in_head = [(lane >= h * HEAD_DIM) & (lane < (h + 1) * HEAD_DIM) for h in range(4)]
    scores = [_dot_nt(jnp.where(in_head[h], q, jnp.zeros_like(q)), kw) for h in range(4)]
    probs, lses = [], []
    for h in range(4):
        s = scores[h] if biases is None else scores[h] + biases[h]
        s = jnp.where(valid, s, NEG)
        m = jnp.max(s, axis=-1, keepdims=True)
        e = jnp.exp(s - m)
        l = jnp.sum(e, axis=-1, keepdims=True)
        probs.append((e / l).astype(BF16))
        lses.append(m + jnp.log(l))
    outs = [_dot(p, vw) for p in probs]
    out, lse = outs[3], lses[3]
    for h in (2, 1, 0):
        out = jnp.where(in_head[h], outs[h], out)
        lse = jnp.where(in_head[h], lses[h], lse)
    return out, jnp.broadcast_to(lse, out.shape)


def _dil_kernel(q_ref, k_ref, v_ref, o_ref, lse_ref, *, u_len, win):
    u0 = pl.program_id(2) * TU
    start = pl.multiple_of(jnp.clip(u0 - DIL_SIDE, 0, u_len - win), DIL_SIDE)
    kw = k_ref[0, pl.ds(start, win), :]
    vw = v_ref[0, pl.ds(start, win), :]
    q = q_ref[0]
    qpos = u0 + lax.broadcasted_iota(I32, (TU, win), 0)
    kpos = start + lax.broadcasted_iota(I32, (TU, win), 1)
    valid = jnp.abs(kpos - qpos) <= DIL_SIDE
    out, lse_out = _local_heads(q, kw, vw, valid, None)
    o_ref[0] = out.astype(BF16)
    lse_ref[0] = lse_out


def _dil_attention(zr3, g, dil):
    b, s, _ = zr3.shape
    u_len = s // dil
    win = TU + 2 * DIL_SIDE
    assert u_len >= win and u_len % TU == 0
    col0 = Z_DIL - Z_HEAD + g * Z_DIL_GROUP
    if dil == 1:
        zv, units, qu = zr3, ZR_COLS // 256, col0 // 256
    else:
        zv = zr3[:, :, col0:col0 + Z_DIL_GROUP].reshape(b, u_len, dil * Z_DIL_GROUP)
        units, qu = Z_DIL_GROUP // 256, 0
    ku, vu = qu + 1, qu + 2
    o, lse = pl.pallas_call(
        functools.partial(_dil_kernel, u_len=u_len, win=win),
        grid=(b, dil, u_len // TU),
        in_specs=[
            pl.BlockSpec((1, TU, 256), lambda bi, r, ui: (bi, ui, r * units + qu)),
            pl.BlockSpec((1, u_len, 256), lambda bi, r, ui: (bi, 0, r * units + ku)),
            pl.BlockSpec((1, u_len, 256), lambda bi, r, ui: (bi, 0, r * units + vu)),
        ],
        out_specs=[
            pl.BlockSpec((1, TU, 256), lambda bi, r, ui: (bi, ui, r)),
            pl.BlockSpec((1, TU, 256), lambda bi, r, ui: (bi, ui, r)),
        ],
        out_shape=[
            jax.ShapeDtypeStruct((b, u_len, dil * 256), BF16),
            jax.ShapeDtypeStruct((b, u_len, dil * 256), F32),
        ],
        compiler_params=_params(3),
        name=f"dil_attn_{dil}",
    )(zv, zv, zv)
    return o.reshape(b * s, 256), lse.reshape(b * s, 256)


def _na_bias_kernel(tbl_ref, o_ref):
    h = pl.program_id(0)
    d = pl.program_id(1)
    n_rel_r, n_rel_c = 2 * NA_ROWS - 1, 2 * NA_COLS - 1
    dl = jnp.clip(d - NA_BIAS_OFF, 0, n_rel_r - 1)
    dr = jnp.clip(d - NA_BIAS_OFF + 1, 0, n_rel_r - 1)
    qc = lax.broadcasted_iota(I32, (GRID_W, LANES), 0)
    ln = lax.broadcasted_iota(I32, (GRID_W, LANES), 1)
    left = ln < GRID_W
    rel = jnp.where(left, ln, ln - GRID_W) - qc + (NA_COLS - 1)
    acc = jnp.zeros((GRID_W, LANES), F32)
    for dd in range(n_rel_c):
        vl = tbl_ref[(h * n_rel_r + dl) * n_rel_c + dd]
        vr = tbl_ref[(h * n_rel_r + dr) * n_rel_c + dd]
        acc = jnp.where(rel == dd, jnp.where(left, vl, vr), acc)
    o_ref[0, 0] = acc


def _na_bias_tiles(na_bias_l):
    n_h = na_bias_l.shape[0]
    return pl.pallas_call(
        _na_bias_kernel,
        grid=(n_h, NA_BIAS_N),
        in_specs=[pl.BlockSpec(memory_space=pltpu.SMEM)],
        out_specs=pl.BlockSpec((1, 1, GRID_W, LANES), lambda h, d: (h, d, 0, 0)),
        out_shape=jax.ShapeDtypeStruct((n_h, NA_BIAS_N, GRID_W, LANES), F32),
        compiler_params=_params(2),
        name="na_bias",
    )(na_bias_l.reshape(-1))


def _na_kernel(q_ref, k_ref, v_ref, bias_ref, o_ref, *, n_rows):
    r0 = pl.program_id(1) * 2
    ws = jnp.clip(r0 - NA_ROWS // 2, 0, n_rows - NA_WIN_ROWS)
    n_keys = NA_WIN_ROWS * GRID_W
    start = pl.multiple_of(ws * GRID_W, GRID_W)
    kw = k_ref[0, pl.ds(start, n_keys), :]
    vw = v_ref[0, pl.ds(start, n_keys), :]
    q = q_ref[0]
    qi = lax.broadcasted_iota(I32, (NA_TQ, n_keys), 0)
    kj = lax.broadcasted_iota(I32, (NA_TQ, n_keys), 1)
    qr, qc = r0 + qi // GRID_W, qi % GRID_W
    kr, kc = ws + kj // GRID_W, kj % GRID_W
    rs = jnp.clip(qr - NA_ROWS // 2, 0, n_rows - NA_ROWS)
    cs = jnp.clip(qc - NA_COLS // 2, 0, GRID_W - NA_COLS)
    valid = (kr >= rs) & (kr < rs + NA_ROWS) & (kc >= cs) & (kc < cs + NA_COLS)
    biases = []
    for h in range(4):
        halves = []
        for half in range(2):
            d0 = ws - r0 - half + (NA_ROWS - 1) + NA_BIAS_OFF
            halves.append(jnp.concatenate(
                [bias_ref[h, pl.ds(d0 + 2 * m, 1)][0] for m in range(NA_WIN_ROWS // 2)], axis=1))
        biases.append(jnp.concatenate(halves, axis=0))
    out, _ = _local_heads(q, kw, vw, valid, biases)
    o_ref[0] = out.astype(BF16)


def _na_attention(zr3, bias_tiles):
    b, s, _ = zr3.shape
    n_rows = s // GRID_W
    assert n_rows >= NA_WIN_ROWS
    qu, ku, vu = (Z_QD - Z_HEAD) // 256, (Z_KD - Z_HEAD) // 256, (Z_VD - Z_HEAD) // 256
    return pl.pallas_call(
        functools.partial(_na_kernel, n_rows=n_rows),
        grid=(b, s // NA_TQ),
        in_specs=[
            pl.BlockSpec((1, NA_TQ, 256), lambda bi, i: (bi, i, qu)),
            pl.BlockSpec((1, s, 256), lambda bi, i: (bi, 0, ku)),
            pl.BlockSpec((1, s, 256), lambda bi, i: (bi, 0, vu)),
            _const_spec(bias_tiles.shape),
        ],
        out_specs=pl.BlockSpec((1, NA_TQ, 256), lambda bi, i: (bi, i, 0)),
        out_shape=jax.ShapeDtypeStruct((b, s, 256), BF16),
        compiler_params=_params(2),
        name="na_attn",
    )(zr3, zr3, zr3, bias_tiles)


def _merge_kernel(x_ref, g_ref, oa_ref, ob0_ref, ob1_ref, ob2_ref, l0_ref, l1_ref, l2_ref, oc_ref, od_ref,
                  wg_ref, wb_ref, wo_ref, y_ref):
    x = x_ref[...]
    hb = _rms(x, g_ref[...]).astype(BF16)
    lses = (l0_ref[...], l1_ref[...], l2_ref[...])
    mx = jnp.maximum(jnp.maximum(lses[0], lses[1]), lses[2])
    ws = [jnp.exp(l - mx) for l in lses]
    den = ws[0] + ws[1] + ws[2]
    obs = (ob0_ref, ob1_ref, ob2_ref)
    ob = (ws[0] / den) * obs[0][...].astype(F32)
    for g in range(1, 3):
        ob = ob + (ws[g] / den) * obs[g][...].astype(F32)
    branches = (oa_ref[...], ob.astype(BF16), oc_ref[...], od_ref[...])
    merged = None
    for i, o in enumerate(branches):
        gate = jax.nn.sigmoid(_dot(hb, wg_ref[i]))
        term = gate * _dot(o, wb_ref[i])
        merged = term if merged is None else merged + term
    y_ref[...] = x + _dot(merged.astype(BF16), wo_ref[...])


def _merge(x2d, g_mix, oa, obs, lses, oc, od, w_gate, w_branch, w_out):
    t = x2d.shape[0]
    tok = lambda w: pl.BlockSpec((TM, w), lambda i: (i, 0))
    return pl.pallas_call(
        _merge_kernel,
        grid=(t // TM,),
        in_specs=[tok(D_MODEL), _const_spec((1, D_MODEL))] + [tok(256)] * 9 + [
            _const_spec(w_gate.shape), _const_spec(w_branch.shape), _const_spec(w_out.shape)],
        out_specs=tok(D_MODEL),
        out_shape=jax.ShapeDtypeStruct((t, D_MODEL), F32),
        compiler_params=_params(1),
        name="merge",
    )(x2d, g_mix, oa, obs[0], obs[1], obs[2], lses[0], lses[1], lses[2], oc, od, w_gate, w_branch, w_out)


def _memkv_kernel(m_ref, g_ref, wk_ref, wv_ref, k_ref, v_ref):
    mb = _rms(m_ref[0], g_ref[...]).astype(BF16)
    k_ref[0] = _dot(mb, wk_ref[...]).astype(BF16)
    v_ref[0] = _dot(mb, wv_ref[...]).astype(BF16)


def _mem_kv(mem, g_mem, w_ck, w_cv):
    b, m_len, d = mem.shape
    blk = pl.BlockSpec((1, m_len, d), lambda bi: (bi, 0, 0))
    return pl.pallas_call(
        _memkv_kernel,
        grid=(b,),
        in_specs=[blk, _const_spec((1, d)), _const_spec((d, d)), _const_spec((d, d))],
        out_specs=[blk, blk],
        out_shape=[jax.ShapeDtypeStruct((b, m_len, d), BF16)] * 2,
        compiler_params=_params(1),
        name="mem_kv",
    )(mem, g_mem, w_ck, w_cv)


def _cross_kernel(x_ref, g_ref, wq_ref, k_ref, v_ref, wo_ref, gf_ref, wr_ref, y_ref, h_ref, aff_ref):
    x = x_ref[...]
    hb = _rms(x, g_ref[...]).astype(BF16)
    q = (_dot(hb, wq_ref[...]) * MEM_SCALE).astype(BF16)
    heads = [slice(h * MEM_HEAD_DIM, (h + 1) * MEM_HEAD_DIM) for h in range(MEM_HEADS)]
    scores = [_dot_nt(q[:, sl], k_ref[0, :, sl]) for sl in heads]
    probs = []
    for s in scores:
        m = jnp.max(s, axis=-1, keepdims=True)
        e = jnp.exp(s - m)
        probs.append((e / jnp.sum(e, axis=-1, keepdims=True)).astype(BF16))
    outs = [_dot(p, v_ref[0, :, sl]).astype(BF16) for p, sl in zip(probs, heads)]
    y = x + _dot(jnp.concatenate(outs, axis=1), wo_ref[...])
    y_ref[...] = y
    h3 = _rms(y, gf_ref[...]).astype(BF16)
    h_ref[...] = h3
    logits = _dot_nt(wr_ref[...], h3)
    m = jnp.max(logits, axis=0, keepdims=True)
    e = jnp.exp(logits - m)
    aff_ref[...] = e / jnp.sum(e, axis=0, keepdims=True)


def _cross(x2d, b, s, g_cross, w_cq, k_mem, v_mem, w_co, g_ffn, w_router_t):
    t = b * s
    n_s = s // TM
    m_len = k_mem.shape[1]
    tok = lambda: pl.BlockSpec((TM, D_MODEL), lambda i: (i, 0))
    mem_blk = pl.BlockSpec((1, m_len, D_MODEL), lambda i: (i // n_s, 0, 0))
    return pl.pallas_call(
        _cross_kernel,
        grid=(t // TM,),
        in_specs=[tok(), _const_spec((1, D_MODEL)), _const_spec((D_MODEL, D_MODEL)), mem_blk, mem_blk,
                  _const_spec((D_MODEL, D_MODEL)), _const_spec((1, D_MODEL)),
                  _const_spec((N_EXPERTS, D_MODEL))],
        out_specs=[tok(), tok(), pl.BlockSpec((N_EXPERTS, TM), lambda i: (0, i))],
        out_shape=[jax.ShapeDtypeStruct((t, D_MODEL), F32), jax.ShapeDtypeStruct((t, D_MODEL), BF16),
                   jax.ShapeDtypeStruct((N_EXPERTS, t), F32)],
        compiler_params=_params(1),
        name="cross",
    )(x2d, g_cross, w_cq, k_mem, v_mem, w_co, g_ffn, w_router_t)


def _select_kernel(aff_ref, pos_ref, *, n_tok, cap):
    n_cnt = n_tok // CNT_CH

    def count_ge(cand):
        def body(c, acc):
            bits = pltpu.bitcast(aff_ref[:, pl.ds(pl.multiple_of(c * CNT_CH, CNT_CH), CNT_CH)], I32)
            return acc + jnp.sum(jnp.where(bits >= cand, 1.0, 0.0), axis=1, keepdims=True)
        return lax.fori_loop(0, n_cnt, body, jnp.zeros((N_EXPERTS, 1), F32))

    def bit_body(i, prefix):
        cand = prefix | lax.shift_left(jnp.int32(1), 30 - i)
        return jnp.where(count_ge(cand) >= cap, cand, prefix)

    thr = lax.fori_loop(0, 31, bit_body, jnp.zeros((N_EXPERTS, 1), I32))
    need_eq = cap - count_ge(thr + 1)

    tri_r = lax.broadcasted_iota(I32, (CH, CH), 0)
    tri_c = lax.broadcasted_iota(I32, (CH, CH), 1)
    tri = jnp.where(tri_r <= tri_c, 1.0, 0.0).astype(BF16)

    def chunk_body(c, carry):
        c_eq, c_row = carry
        sl = pl.ds(pl.multiple_of(c * CH, CH), CH)
        bits = pltpu.bitcast(aff_ref[:, sl], I32)
        eq = jnp.where(bits == thr, 1.0, 0.0)
        gt = jnp.where(bits > thr, 1.0, 0.0)
        eq_before = c_eq + _dot(eq.astype(BF16), tri) - eq
        sel = gt + eq * jnp.where(eq_before < need_eq, 1.0, 0.0)
        row = c_row + _dot(sel.astype(BF16), tri) - sel
        pos_ref[:, sl] = jnp.where(sel > 0.5, row, -1.0).astype(I32)
        taken = jnp.sum(sel, axis=1, keepdims=True)
        c_row = c_row + jnp.ceil(taken * (1.0 / ROW_ALIGN)) * ROW_ALIGN
        return (c_eq + jnp.sum(eq, axis=1, keepdims=True), c_row)

    zero = jnp.zeros((N_EXPERTS, 1), F32)
    lax.fori_loop(0, n_tok // CH, chunk_body, (zero, zero))


def _select(aff_t, cap):
    n_tok = aff_t.shape[1]
    return pl.pallas_call(
        functools.partial(_select_kernel, n_tok=n_tok, cap=cap),
        grid=(1,),
        in_specs=[_const_spec(aff_t.shape)],
        out_specs=_const_spec(aff_t.shape),
        out_shape=jax.ShapeDtypeStruct(aff_t.shape, I32),
        compiler_params=_params(1),
        name="select",
    )(aff_t)


def _window_hits(pos_ref, base_ref, e, k, n_k, p):
    rows = base_ref[e * n_k + k] + p * WIN + lax.broadcasted_iota(I32, (WIN, CH), 0)
    return pos_ref[e:e + 1, :] == rows


def _window_to_hbm(scr, hbm, sem, e, row0):
    return pltpu.make_async_copy(scr.at[pl.ds(e * WIN, WIN)], hbm.at[e, pl.ds(row0, WIN)], sem.at[e])


def _window_from_hbm(hbm, scr, sem, e, row0):
    return pltpu.make_async_copy(hbm.at[e, pl.ds(row0, WIN)], scr.at[pl.ds(e * WIN, WIN)], sem.at[e])


def _gather_kernel(base_ref, npass_ref, h_ref, pos_ref, aff_ref, xe_in, gt_in, xe_ref, gt_ref,
                   x_scr, g_scr, sem_x, sem_g, *, n_k):
    del xe_in, gt_in
    k = pl.program_id(0)

    def one_pass(p, carry):
        for e0 in range(0, N_EXPERTS, GATHER_GROUP):
            hits = [_window_hits(pos_ref, base_ref, e, k, n_k, p) for e in range(e0, e0 + GATHER_GROUP)]
            onehot = jnp.concatenate([jnp.where(h, 1.0, 0.0).astype(BF16) for h in hits], axis=0)
            x_scr[e0 * WIN:(e0 + GATHER_GROUP) * WIN, :] = _dot(onehot, h_ref[...]).astype(BF16)
            for i, h in enumerate(hits):
                e = e0 + i
                gate = jnp.sum(jnp.where(h, aff_ref[e:e + 1, :], 0.0), axis=1, keepdims=True)
                g_scr[e * WIN:(e + 1) * WIN, :] = jnp.broadcast_to(gate, (WIN, LANES))
        copies = []
        for e in range(N_EXPERTS):
            row0 = pl.multiple_of(base_ref[e * n_k + k] + p * WIN, ROW_ALIGN)
            copies.append(_window_to_hbm(x_scr, xe_ref, sem_x, e, row0))
            copies.append(_window_to_hbm(g_scr, gt_ref, sem_g, e, row0))
        for c in copies:
            c.start()
        for c in copies:
            c.wait()
        return carry

    lax.fori_loop(0, npass_ref[k], one_pass, 0)


def _expert_gather(base, npass, h3, pos, aff_t, rows_pad):
    n_tok = h3.shape[0]
    n_k = n_tok // CH
    xe0 = jnp.zeros((N_EXPERTS, rows_pad, D_MODEL), BF16)
    gt0 = jnp.zeros((N_EXPERTS, rows_pad, LANES), F32)
    grid_spec = pltpu.PrefetchScalarGridSpec(
        num_scalar_prefetch=2,
        grid=(n_k,),
        in_specs=[
            pl.BlockSpec((CH, D_MODEL), lambda k, *_: (k, 0)),
            pl.BlockSpec((N_EXPERTS, CH), lambda k, *_: (0, k)),
            pl.BlockSpec((N_EXPERTS, CH), lambda k, *_: (0, k)),
            pl.BlockSpec(memory_space=pl.ANY),
            pl.BlockSpec(memory_space=pl.ANY),
        ],
        out_specs=[pl.BlockSpec(memory_space=pl.ANY), pl.BlockSpec(memory_space=pl.ANY)],
        scratch_shapes=[pltpu.VMEM((N_EXPERTS * WIN, D_MODEL), BF16), pltpu.VMEM((N_EXPERTS * WIN, LANES), F32),
                        pltpu.SemaphoreType.DMA((N_EXPERTS,)), pltpu.SemaphoreType.DMA((N_EXPERTS,))],
    )
    return pl.pallas_call(
        functools.partial(_gather_kernel, n_k=n_k),
        grid_spec=grid_spec,
        out_shape=[jax.ShapeDtypeStruct(xe0.shape, BF16), jax.ShapeDtypeStruct(gt0.shape, F32)],
        input_output_aliases={5: 0, 6: 1},
        compiler_params=_params(1),
        name="expert_gather",
    )(base, npass, h3, pos, aff_t, xe0, gt0)


def _ffn_kernel(end_ref, xe_ref, gt_ref, wg_ref, wu_ref, wd_ref, ye_ref):
    e, j = pl.program_id(0), pl.program_id(1)

    @pl.when(j * RB < end_ref[e])
    def _():
        xe = xe_ref[0]
        hid = jax.nn.silu(_dot(xe, wg_ref[0])) * _dot(xe, wu_ref[0])
        gate = jnp.concatenate([gt_ref[0]] * (D_MODEL // LANES), axis=1)
        ye_ref[0] = (_dot(hid.astype(BF16), wd_ref[0]) * gate).astype(BF16)

    @pl.when(j * RB >= end_ref[e])
    def _():
        ye_ref[0] = jnp.zeros((RB, D_MODEL), BF16)


def _expert_ffn(end, xe, gt, w_e_gate, w_e_up, w_e_down):
    rows_pad = xe.shape[1]
    grid_spec = pltpu.PrefetchScalarGridSpec(
        num_scalar_prefetch=1,
        grid=(N_EXPERTS, rows_pad // RB),
        in_specs=[
            pl.BlockSpec((1, RB, D_MODEL), lambda e, j, *_: (e, j, 0)),
            pl.BlockSpec((1, RB, LANES), lambda e, j, *_: (e, j, 0)),
            pl.BlockSpec((1, D_MODEL, EXPERT_FF), lambda e, j, *_: (e, 0, 0)),
            pl.BlockSpec((1, D_MODEL, EXPERT_FF), lambda e, j, *_: (e, 0, 0)),
            pl.BlockSpec((1, EXPERT_FF, D_MODEL), lambda e, j, *_: (e, 0, 0)),
        ],
        out_specs=pl.BlockSpec((1, RB, D_MODEL), lambda e, j, *_: (e, j, 0)),
    )
    return pl.pallas_call(
        _ffn_kernel,
        grid_spec=grid_spec,
        out_shape=jax.ShapeDtypeStruct(xe.shape, BF16),
        compiler_params=_params(2),
        name="expert_ffn",
    )(end, xe, gt, w_e_gate, w_e_up, w_e_down)


def _scatter_kernel(base_ref, npass_ref, x_ref, pos_ref, ye_ref, gfin_ref, y_ref, stage, sem,
                    *, final_norm, n_k):
    k = pl.program_id(0)
    y_ref[...] = x_ref[...]

    def one_pass(p, carry):
        copies = []
        for e in range(N_EXPERTS):
            row0 = pl.multiple_of(base_ref[e * n_k + k] + p * WIN, ROW_ALIGN)
            copies.append(_window_from_hbm(ye_ref, stage, sem, e, row0))
        for c in copies:
            c.start()
        onehot = jnp.concatenate(
            [jnp.where(_window_hits(pos_ref, base_ref, e, k, n_k, p), 1.0, 0.0).astype(BF16)
             for e in range(N_EXPERTS)], axis=0)
        for c in copies:
            c.wait()
        y_ref[...] += _dot_tn(onehot, stage[...])
        return carry

    lax.fori_loop(0, npass_ref[k], one_pass, 0)
    if final_norm:
        y_ref[...] = _rms(y_ref[...], gfin_ref[...])


def _expert_scatter(base, npass, x2d, pos, ye, g_final, final_norm):
    t = x2d.shape[0]
    n_k = t // CH
    grid_spec = pltpu.PrefetchScalarGridSpec(
        num_scalar_prefetch=2,
        grid=(n_k,),
        in_specs=[
            pl.BlockSpec((CH, D_MODEL), lambda k, *_: (k, 0)),
            pl.BlockSpec((N_EXPERTS, CH), lambda k, *_: (0, k)),
            pl.BlockSpec(memory_space=pl.ANY),
            pl.BlockSpec((1, D_MODEL), lambda k, *_: (0, 0)),
        ],
        out_specs=pl.BlockSpec((CH, D_MODEL), lambda k, *_: (k, 0)),
        scratch_shapes=[pltpu.VMEM((N_EXPERTS * WIN, D_MODEL), BF16), pltpu.SemaphoreType.DMA((N_EXPERTS,))],
    )
    return pl.pallas_call(
        functools.partial(_scatter_kernel, final_norm=final_norm, n_k=n_k),
        grid_spec=grid_spec,
        out_shape=jax.ShapeDtypeStruct((t, D_MODEL), F32),
        compiler_params=_params(1),
        name="expert_scatter",
    )(base, npass, x2d, pos, ye, g_final)


def _segments(pos):
    n_e, n_tok = pos.shape
    n_k = n_tok // CH
    cnt = jnp.sum((pos >= 0).reshape(n_e, n_k, CH), axis=-1, dtype=I32)
    padded = (cnt + (ROW_ALIGN - 1)) // ROW_ALIGN * ROW_ALIGN
    end = jnp.cumsum(padded, axis=1)
    base = end - padded
    npass = jnp.maximum(jnp.max((cnt + (WIN - 1)) // WIN, axis=0), 1)
    return base.reshape(-1).astype(I32), npass.astype(I32), end[:, -1].astype(I32)


def _rows_pad(n_tok, cap):
    rows = cap + ROW_ALIGN * (n_tok // CH) + CH
    return (rows + RB - 1) // RB * RB


def _rope_tables(s):
    pos = jnp.arange(s, dtype=F32)[:, None]
    lane = jnp.arange(LANES)
    tabs = []
    for period, base, rot in ((LANES, MLA_NOPE, MLA_ROPE), (HEAD_DIM, 0, HEAD_DIM // 4), (DIFF_QK, 0, DIFF_QK // 4)):
        half = rot // 2
        rel = lane % period - base
        inv_freq = ROPE_THETA ** (-jnp.arange(half, dtype=F32) * (2.0 / rot))
        ang = pos * inv_freq[None, :]
        cos, sin = jnp.cos(ang), jnp.sin(ang)
        idx = jnp.clip(jnp.where(rel >= half, rel - half, rel), 0, half - 1)
        first = (rel >= 0) & (rel < half)
        second = (rel >= half) & (rel < rot)
        tabs.append(jnp.where(first | second, cos[:, idx], 1.0))
        tabs.append(jnp.where(first, -sin[:, idx], 0.0))
        tabs.append(jnp.where(second, sin[:, idx], 0.0))
    return jnp.stack(tabs).astype(F32)


def _prep_layer(l, w_in, g_mix, g_cq, w_uq, g_ckv, w_ukv, lam_q1, lam_k1, lam_q2, lam_k2, g_diff, na_bias,
                w_gate, w_branch, w_out, g_cross, g_mem, w_cq, w_ck, w_cv, w_co, g_ffn, w_router,
                w_e_gate, w_e_up, w_e_down):
    d = D_MODEL
    wi = w_in[l]
    kpe0 = MLA_Q_RANK + MLA_KV_RANK
    z64 = jnp.zeros((d, MLA_NOPE), F32)
    z32 = jnp.zeros((d, LANES - MLA_NOPE - MLA_ROPE), F32)
    rest = wi[:, kpe0 + MLA_ROPE:]
    vc0 = Z_KC + 256 - Z_HEAD
    n_g = len(DIL_GROUPS)
    dil_cols = rest[:, :n_g * Z_DIL_GROUP].reshape(d, 3, n_g, 256).transpose(0, 2, 1, 3).reshape(d, n_g * Z_DIL_GROUP)
    w_z = jnp.concatenate([wi[:, :kpe0], z64, wi[:, kpe0:kpe0 + MLA_ROPE], z32, dil_cols,
                           rest[:, n_g * Z_DIL_GROUP:vc0], rest[:, vc0 + 256:]], axis=1)
    w_vct = rest[:, vc0:vc0 + 256].T
    qh = MLA_NOPE + MLA_ROPE
    wuq = w_uq[l].reshape(MLA_Q_RANK, MLA_HEADS, qh)
    wuq = jnp.pad(wuq, ((0, 0), (0, 0), (0, LANES - qh))).reshape(MLA_Q_RANK, MLA_HEADS * LANES)
    wukv = w_ukv[l].reshape(MLA_KV_RANK, MLA_HEADS, MLA_NOPE + HEAD_DIM)
    wk = jnp.pad(wukv[:, :, :MLA_NOPE], ((0, 0), (0, 0), (0, LANES - MLA_NOPE))).reshape(MLA_KV_RANK, MLA_HEADS * LANES)
    wv = wukv[:, :, MLA_NOPE:].reshape(MLA_KV_RANK, MLA_HEADS * HEAD_DIM)
    return dict(
        g_mix=g_mix[l][None], w_z=w_z.astype(BF16), g_cq=g_cq[l][None], g_ckv=g_ckv[l][None],
        w_uq=wuq.astype(BF16), w_k=wk.astype(BF16), w_vt=wv.T.astype(BF16), w_vct=w_vct.astype(BF16),
        lam=jnp.stack([lam_q1[l], lam_k1[l], lam_q2[l], lam_k2[l]]).astype(F32),
        g_diff=jnp.tile(g_diff[l], 2)[None], na_bias=na_bias[l],
        w_gate=w_gate[l].astype(BF16), w_branch=w_branch[l].astype(BF16), w_out=w_out[l].astype(BF16),
        g_cross=g_cross[l][None], g_mem=g_mem[l][None], w_cq=w_cq[l].astype(BF16), w_ck=w_ck[l].astype(BF16),
        w_cv=w_cv[l].astype(BF16), w_co=w_co[l].astype(BF16), g_ffn=g_ffn[l][None],
        w_router_t=w_router[l].T.astype(BF16), w_e_gate=w_e_gate[l].astype(BF16),
        w_e_up=w_e_up[l].astype(BF16), w_e_down=w_e_down[l].astype(BF16),
    )


def _trunk(x, mem, layers, bias_tiles, g_final):
    b, s, d = x.shape
    t = b * s
    cap = CAPACITY_FACTOR * t // N_EXPERTS
    assert s % TM == 0 and s % TQ == 0 and s % TK == 0 and t % CH == 0 and cap % RB == 0
    rope_tab = _rope_tables(s)
    x2d = x.reshape(t, d)
    for l, p in enumerate(layers):
        zr, qa, ka, vat, vct = _project(x2d, b, s, p["g_mix"], p["w_z"], rope_tab, p["g_cq"], p["g_ckv"],
                                        p["w_uq"], p["w_k"], p["w_vt"], p["w_vct"])
        zr3 = zr.reshape(b, s, ZR_COLS)
        oa = _mla_attention(qa, ka, vat).reshape(t, 256)
        dil = [_dil_attention(zr3, g, dl) for g, (_, dl) in enumerate(DIL_GROUPS)]
        lam_init = 0.8 - 0.6 * math.exp(-0.3 * l)
        oc = _diff_attention(zr3, vct, p["lam"], p["g_diff"], lam_init).reshape(t, 256)
        od = _na_attention(zr3, bias_tiles[l]).reshape(t, 256)
        x2d = _merge(x2d, p["g_mix"], oa, [o for o, _ in dil], [ls for _, ls in dil], oc, od,
                     p["w_gate"], p["w_branch"], p["w_out"])
        k_mem, v_mem = _mem_kv(mem, p["g_mem"], p["w_ck"], p["w_cv"])
        x2d, h3, aff_t = _cross(x2d, b, s, p["g_cross"], p["w_cq"], k_mem, v_mem, p["w_co"], p["g_ffn"],
                                p["w_router_t"])
        pos = _select(aff_t, cap)
        base, npass, end = _segments(pos)
        xe, gt = _expert_gather(base, npass, h3, pos, aff_t, _rows_pad(t, cap))
        ye = _expert_ffn(end, xe, gt, p["w_e_gate"], p["w_e_up"], p["w_e_down"])
        x2d = _expert_scatter(base, npass, x2d, pos, ye, g_final[None], final_norm=(l == len(layers) - 1))
    return x2d.reshape(b, s, d)


def kernel(x_prompt, x_sample, mem_prompt, mem_sample, w_in, g_mix, g_cq, w_uq, g_ckv, w_ukv, lam_q1, lam_k1, lam_q2, lam_k2, g_diff, na_bias, w_gate, w_branch, w_out, g_cross, g_mem, w_cq, w_ck, w_cv, w_co, g_ffn, w_router, w_e_gate, w_e_up, w_e_down, g_final):
    depth = w_in.shape[0]
    layers = [_prep_layer(l, w_in, g_mix, g_cq, w_uq, g_ckv, w_ukv, lam_q1, lam_k1, lam_q2, lam_k2, g_diff,
                          na_bias, w_gate, w_branch, w_out, g_cross, g_mem, w_cq, w_ck, w_cv, w_co, g_ffn,
                          w_router, w_e_gate, w_e_up, w_e_down) for l in range(depth)]
    bias_tiles = [_na_bias_tiles(p["na_bias"]) for p in layers]
    y_prompt = _trunk(x_prompt, mem_prompt, layers, bias_tiles, g_final)
    y_sample = _trunk(x_sample, mem_sample, layers, bias_tiles, g_final)
    return (y_prompt, y_sample)
```

```python
import functools
import math

import jax
import jax.numpy as jnp
from jax import lax
from jax.experimental import pallas as pl
from jax.experimental.pallas import tpu as pltpu

BF16 = jnp.bfloat16
F32 = jnp.float32
I32 = jnp.int32

D_MODEL = 1024
EPS = 1e-6
NEG = -1e30
ROPE_THETA = 500000.0
HEAD_DIM = 64
MLA_HEADS = 4
MLA_Q_RANK = 256
MLA_KV_RANK = 128
MLA_NOPE = 64
MLA_ROPE = 32
MLA_SCALE = (MLA_NOPE + MLA_ROPE) ** -0.5
DIL_GROUPS = ((128, 1), (512, 4), (2048, 16))
DIL_SIDE = 64
DIL_SCALE = HEAD_DIM ** -0.5
DIFF_QK = 32
DIFF_SCALE = DIFF_QK ** -0.5
NA_ROWS = 8
NA_COLS = 16
GRID_W = 64
NA_SCALE = HEAD_DIM ** -0.5
MEM_HEADS = 4
MEM_HEAD_DIM = D_MODEL // MEM_HEADS
MEM_SCALE = MEM_HEAD_DIM ** -0.5
N_EXPERTS = 16
EXPERT_FF = 512
CAPACITY_FACTOR = 2

LANES = 128
VMEM_LIMIT = 56 * 1024 * 1024

Z_CQ = 0
Z_CKV = 256
Z_KPE = 384
Z_HEAD = 512
Z_DIL0 = 512
Z_DIL_GROUP = 768
Z_QC = 1280
Z_KC = 1536
Z_QD = 1792
Z_KD = 2048
Z_VD = 2304
Z_DILX = 2560
Z_COLS = Z_DILX + 2 * Z_DIL_GROUP
ZR_COLS = Z_DILX - Z_HEAD
LOG2E = math.log2(math.e)

TM = 512
TQ = 256
TK = 512
MLA_UNROLL = 4
DIFF_UNROLL = 2
TU = 128
LOCAL_TILES = 2
NA_TQ = 128
NA_WIN_ROWS = 10
NA_BIAS_OFF = 2
NA_BIAS_N = 18
RB = 256
CH = 512
WIN = 128
ROW_ALIGN = 16
GATHER_GROUP = 4
CNT_CH = 4096


def _params(n_grid):
    return pltpu.CompilerParams(dimension_semantics=("arbitrary",) * n_grid,
                                vmem_limit_bytes=VMEM_LIMIT)


def _rms(x, g):
    return x * lax.rsqrt(jnp.mean(x * x, axis=-1, keepdims=True) + EPS) * g


def _dot(a, b):
    return jnp.dot(a, b, preferred_element_type=F32)


def _dot_nt(a, b):
    return lax.dot_general(a, b, (((1,), (1,)), ((), ())), preferred_element_type=F32)


def _dot_tn(a, b):
    return lax.dot_general(a, b, (((0,), (0,)), ((), ())), preferred_element_type=F32)


def _rope_block(x, c, sa, sb, half):
    return x * c + pltpu.roll(x, LANES - half, 1) * sa + pltpu.roll(x, half, 1) * sb


def _proj_kernel(x_ref, g_ref, w_ref, rope_ref, gcq_ref, gckv_ref, wuq_ref, wk_ref, wvt_ref, wvct_ref,
                 zr_ref, qa_ref, ka_ref, vat_ref, vct_ref, d1_ref, d2_ref, scr_ref):
    hb = _rms(x_ref[...], g_ref[...]).astype(BF16)

    def rope(x, kind, half):
        return _rope_block(x, rope_ref[3 * kind], rope_ref[3 * kind + 1], rope_ref[3 * kind + 2], half)

    z0 = _dot(hb, w_ref[:, 0:Z_HEAD])
    cq = _rms(z0[:, Z_CQ:Z_CQ + MLA_Q_RANK], gcq_ref[...]).astype(BF16)
    ckv = _rms(z0[:, Z_CKV:Z_CKV + MLA_KV_RANK], gckv_ref[...]).astype(BF16)
    kpe = rope(z0[:, Z_KPE:Z_KPE + LANES], 0, MLA_ROPE // 2)
    qa = _dot(cq, wuq_ref[...]) * (MLA_SCALE * LOG2E)
    kn = _dot(ckv, wk_ref[...])
    for h in range(MLA_HEADS):
        sl = slice(h * LANES, (h + 1) * LANES)
        qa_ref[0, h] = rope(qa[:, sl], 0, MLA_ROPE // 2).astype(BF16)
        ka_ref[0, h] = (kn[:, sl] + kpe).astype(BF16)
    vat_ref[0] = _dot_nt(wvt_ref[...], ckv).astype(BF16)
    vct_ref[0] = _dot_nt(wvct_ref[...], hb).astype(BF16)

    n_scr = [0]

    def seg(start, scale, kind, half, dil_out=None):
        z = _dot(hb, w_ref[:, start:start + 256])
        if scale is not None:
            z = z * scale
        for j in range(2):
            zb = z[:, j * LANES:(j + 1) * LANES]
            if kind is not None:
                zb = rope(zb, kind, half)
            if dil_out is None:
                o = start - Z_HEAD + j * LANES
                zr_ref[:, o:o + LANES] = zb.astype(BF16)
            else:
                out_ref, col, dil = dil_out
                slot = n_scr[0] % scr_ref.shape[0]
                n_scr[0] += 1
                scr_ref[slot] = zb
                for r in range(dil):
                    out_ref[0, r, :, col + j * LANES:col + (j + 1) * LANES] = (
                        scr_ref[slot, pl.ds(r, TM // dil, stride=dil), :].astype(BF16))

    dil_half = HEAD_DIM // 4 // 2
    diff_half = DIFF_QK // 4 // 2
    seg(Z_DIL0, DIL_SCALE, 1, dil_half)
    seg(Z_DIL0 + 256, None, 1, dil_half)
    seg(Z_DIL0 + 512, None, None, 0)
    for g, out_ref in ((1, d1_ref), (2, d2_ref)):
        base = Z_DILX + (g - 1) * Z_DIL_GROUP
        dil = DIL_GROUPS[g][1]
        seg(base, DIL_SCALE, 1, dil_half, (out_ref, 0, dil))
        seg(base + 256, None, 1, dil_half, (out_ref, 256, dil))
        seg(base + 512, None, None, 0, (out_ref, 512, dil))
    seg(Z_QC, DIFF_SCALE * LOG2E, 2, diff_half)
    seg(Z_KC, None, 2, diff_half)
    seg(Z_QD, NA_SCALE, None, 0)
    seg(Z_KD, None, None, 0)
    seg(Z_VD, None, None, 0)


def _const_spec(shape):
    nd = len(shape)
    return pl.BlockSpec(shape, lambda *_: (0,) * nd)


def _project(x2d, b, s, g_mix, w_z, rope_tab, g_cq, g_ckv, w_uq, w_k, w_vt, w_vct):
    t = b * s
    n_s = s // TM
    v_cols = MLA_HEADS * HEAD_DIM
    dils = [dl for _, dl in DIL_GROUPS[1:]]
    return pl.pallas_call(
        _proj_kernel,
        grid=(t // TM,),
        in_specs=[
            pl.BlockSpec((TM, D_MODEL), lambda i: (i, 0)),
            _const_spec((1, D_MODEL)),
            _const_spec((D_MODEL, Z_COLS)),
            pl.BlockSpec((9, TM, LANES), lambda i: (0, i % n_s, 0)),
            _const_spec((1, MLA_Q_RANK)),
            _const_spec((1, MLA_KV_RANK)),
            _const_spec((MLA_Q_RANK, MLA_HEADS * LANES)),
            _const_spec((MLA_KV_RANK, MLA_HEADS * LANES)),
            _const_spec((v_cols, MLA_KV_RANK)),
            _const_spec((v_cols, D_MODEL)),
        ],
        out_specs=[
            pl.BlockSpec((TM, ZR_COLS), lambda i: (i, 0)),
            pl.BlockSpec((1, MLA_HEADS, TM, LANES), lambda i: (i // n_s, 0, i % n_s, 0)),
            pl.BlockSpec((1, MLA_HEADS, TM, LANES), lambda i: (i // n_s, 0, i % n_s, 0)),
            pl.BlockSpec((1, v_cols, TM), lambda i: (i // n_s, 0, i % n_s)),
            pl.BlockSpec((1, v_cols, TM), lambda i: (i // n_s, 0, i % n_s)),
        ] + [pl.BlockSpec((1, dl, TM // dl, Z_DIL_GROUP), lambda i: (i // n_s, 0, i % n_s, 0)) for dl in dils],
        out_shape=[
            jax.ShapeDtypeStruct((t, ZR_COLS), BF16),
            jax.ShapeDtypeStruct((b, MLA_HEADS, s, LANES), BF16),
            jax.ShapeDtypeStruct((b, MLA_HEADS, s, LANES), BF16),
            jax.ShapeDtypeStruct((b, v_cols, s), BF16),
            jax.ShapeDtypeStruct((b, v_cols, s), BF16),
        ] + [jax.ShapeDtypeStruct((b, dl, s // dl, Z_DIL_GROUP), BF16) for dl in dils],
        scratch_shapes=[pltpu.VMEM((4, TM, LANES), F32)],
        compiler_params=_params(1),
        name="proj",
    )(x2d, g_mix, w_z, rope_tab, g_cq, g_ckv, w_uq, w_k, w_vt, w_vct)


def _chunk(c):
    return pl.ds(pl.multiple_of(c * TK, TK), TK)


def _dense_attention(qs, k_at, vt_at, s_ref, p_ref, n_chunks, unroll):
    n = len(qs)

    def step(c, cur, nxt, carry):
        rows_next = _chunk(jnp.minimum(c + 1, n_chunks - 1))
        rows_prev = _chunk(jnp.maximum(c - 1, 0))
        new = []
        for i in range(n):
            m, l, acc, alpha_prev = carry[i]
            s_ref[i, nxt] = _dot_nt(k_at(i, rows_next), qs[i])
            acc = alpha_prev * acc + _dot(vt_at(i, rows_prev), p_ref[i, nxt])
            st = s_ref[i, cur]
            m_new = jnp.maximum(m, jnp.max(st, axis=0, keepdims=True))
            alpha = jnp.exp2(m - m_new)
            p = jnp.exp2(st - m_new)
            l = alpha * l + jnp.sum(p, axis=0, keepdims=True)
            p_ref[i, cur] = p.astype(BF16)
            new.append((m_new, l, acc, alpha))
        return tuple(new)

    def body(j, carry):
        for u in range(unroll):
            carry = step(unroll * j + u, u % 2, (u + 1) % 2, carry)
        return carry

    for i in range(n):
        s_ref[i, 0] = _dot_nt(k_at(i, pl.ds(0, TK)), qs[i])
        p_ref[i, 1] = jnp.zeros((TK, TQ), BF16)
    init = tuple((jnp.full((1, TQ), NEG, F32), jnp.zeros((1, TQ), F32), jnp.zeros((HEAD_DIM, TQ), F32),
                  jnp.ones((1, TQ), F32)) for _ in range(n))
    assert unroll % 2 == 0 and n_chunks % unroll == 0
    res = lax.fori_loop(0, n_chunks // unroll, body, init)
    out = []
    for i in range(n):
        _, l, acc, alpha = res[i]
        acc = alpha * acc + _dot(vt_at(i, pl.ds((n_chunks - 1) * TK, TK)), p_ref[i, 1])
        out.append((acc, l))
    return out


def _mla_kernel(q_ref, k_ref, vt_ref, o_ref, s_ref, p_ref, *, n_chunks):
    res = _dense_attention(
        [q_ref[0, hh] for hh in range(2)],
        lambda i, rows: k_ref[0, i, rows, :],
        lambda i, rows: vt_ref[0, i * HEAD_DIM:(i + 1) * HEAD_DIM, rows],
        s_ref, p_ref, n_chunks, MLA_UNROLL)
    out_t = jnp.concatenate([acc / l for acc, l in res], axis=0)
    o_ref[0] = out_t.T.astype(BF16)


def _mla_attention(qa, ka, vat):
    b, _, s, _ = qa.shape
    return pl.pallas_call(
        functools.partial(_mla_kernel, n_chunks=s // TK),
        grid=(b, 2, s // TQ),
        in_specs=[
            pl.BlockSpec((1, 2, TQ, LANES), lambda bi, p, qi: (bi, p, qi, 0)),
            pl.BlockSpec((1, 2, s, LANES), lambda bi, p, qi: (bi, p, 0, 0)),
            pl.BlockSpec((1, LANES, s), lambda bi, p, qi: (bi, p, 0)),
        ],
        out_specs=pl.BlockSpec((1, TQ, LANES), lambda bi, p, qi: (bi, qi, p)),
        out_shape=jax.ShapeDtypeStruct((b, s, 2 * LANES), BF16),
        scratch_shapes=[pltpu.VMEM((2, 2, TK, TQ), F32), pltpu.VMEM((2, 2, TK, TQ), BF16)],
        compiler_params=_params(3),
        name="mla_attn",
    )(qa, ka, vat)


def _diff_kernel(q_ref, k_ref, vt_ref, lam_ref, g_ref, o_ref, s_ref, p_ref, *, n_chunks, lam_init):
    lane = lax.broadcasted_iota(I32, (TQ, LANES), 1)
    lv = lam_ref[...]
    lam = (jnp.exp(jnp.sum(lv[0:1] * lv[1:2], axis=-1, keepdims=True))
           - jnp.exp(jnp.sum(lv[2:3] * lv[3:4], axis=-1, keepdims=True)) + lam_init)
    q = q_ref[0]
    qs = [jnp.where((lane >= i * DIFF_QK) & (lane < (i + 1) * DIFF_QK), q, jnp.zeros_like(q)) for i in range(4)]
    res = _dense_attention(
        qs,
        lambda i, rows: k_ref[0, rows, :],
        lambda i, rows: vt_ref[0, (i // 2) * HEAD_DIM:(i // 2 + 1) * HEAD_DIM, rows],
        s_ref, p_ref, n_chunks, DIFF_UNROLL)
    parts = [acc / l for acc, l in res]
    out_t = jnp.concatenate([parts[2 * hh] - lam * parts[2 * hh + 1] for hh in range(2)], axis=0)
    o = out_t.T
    head0 = lane < HEAD_DIM
    sq = o * o
    ms0 = jnp.sum(jnp.where(head0, sq, 0.0), axis=-1, keepdims=True)
    ms1 = jnp.sum(jnp.where(head0, 0.0, sq), axis=-1, keepdims=True)
    ms = jnp.where(head0, ms0, ms1) * (1.0 / HEAD_DIM)
    o_ref[0] = (o * lax.rsqrt(ms + EPS) * g_ref[...] * (1.0 - lam_init)).astype(BF16)


def _diff_attention(zr3, vct, lam_vecs, g_diff2, lam_init):
    b, s, _ = zr3.shape
    qb, kb = (Z_QC - Z_HEAD) // LANES, (Z_KC - Z_HEAD) // LANES
    return pl.pallas_call(
        functools.partial(_diff_kernel, n_chunks=s // TK, lam_init=lam_init),
        grid=(b, 2, s // TQ),
        in_specs=[
            pl.BlockSpec((1, TQ, LANES), lambda bi, p, qi: (bi, qi, qb + p)),
            pl.BlockSpec((1, s, LANES), lambda bi, p, qi: (bi, 0, kb + p)),
            pl.BlockSpec((1, LANES, s), lambda bi, p, qi: (bi, p, 0)),
            _const_spec((4, DIFF_QK)),
            _const_spec((1, LANES)),
        ],
        out_specs=pl.BlockSpec((1, TQ, LANES), lambda bi, p, qi: (bi, qi, p)),
        out_shape=jax.ShapeDtypeStruct((b, s, 2 * LANES), BF16),
        scratch_shapes=[pltpu.VMEM((4, 2, TK, TQ), F32), pltpu.VMEM((4, 2, TK, TQ), BF16)],
        compiler_params=_params(3),
        name="diff_attn",
    )(zr3, zr3, vct, lam_vecs, g_diff2)


def _local_heads(tiles):
    n = tiles[0][0].shape[0]
    lane = lax.broadcasted_iota(I32, (n, 4 * HEAD_DIM), 1)
    in_head = [(lane >= h * HEAD_DIM) & (lane < (h + 1) * HEAD_DIM) for h in range(4)]
    scores = [[_dot_nt(jnp.where(in_head[h], q, jnp.zeros_like(q)), kw) for h in range(4)]
              for q, kw, _, _, _ in tiles]
    probs, lses = [], []
    for (_, _, _, valid, biases), sc in zip(tiles, scores):
        tile_p, tile_l = [], []
        for h in range(4):
            s = sc[h] if biases is None else sc[h] + biases[h]
            s = jnp.where(valid, s, NEG)
            m = jnp.max(s, axis=-1, keepdims=True)
            e = jnp.exp(s - m)
            l = jnp.sum(e, axis=-1, keepdims=True)
            tile_p.append((e / l).astype(BF16))
            tile_l.append(m + jnp.log(l))
        probs.append(tile_p)
        lses.append(tile_l)
    outs = [[_dot(p, vw) for p in tile_p] for (_, _, vw, _, _), tile_p in zip(tiles, probs)]
    res = []
    for tile_o, tile_l in zip(outs, lses):
        out, lse = tile_o[3], tile_l[3]
        for h in (2, 1, 0):
            out = jnp.where(in_head[h], tile_o[h], out)
            lse = jnp.where(in_head[h], tile_l[h], lse)
        res.append((out, jnp.broadcast_to(lse, out.shape)))
    return res


def _dil_kernel(q_ref, k_ref, v_ref, o_ref, lse_ref, *, u_len, win):
    tiles = []
    for i in range(LOCAL_TILES):
        u0 = (pl.program_id(2) * LOCAL_TILES + i) * TU
        start = pl.multiple_of(jnp.clip(u0 - DIL_SIDE, 0, u_len - win), DIL_SIDE)
        qpos = u0 + lax.broadcasted_iota(I32, (TU, win), 0)
        kpos = start + lax.broadcasted_iota(I32, (TU, win), 1)
        tiles.append((q_ref[0, 0, i * TU:(i + 1) * TU, :], k_ref[0, 0, pl.ds(start, win), :],
                      v_ref[0, 0, pl.ds(start, win), :], jnp.abs(kpos - qpos) <= DIL_SIDE, None))
    for i, (out, lse_out) in enumerate(_local_heads(tiles)):
        o_ref[0, 0, i * TU:(i + 1) * TU, :] = out.astype(BF16)
        lse_ref[0, 0, i * TU:(i + 1) * TU, :] = lse_out


def _dil_attention(src, qu):
    b, dil, u_len, _ = src.shape
    win = TU + 2 * DIL_SIDE
    step = LOCAL_TILES * TU
    assert u_len >= win and u_len % step == 0
    return pl.pallas_call(
        functools.partial(_dil_kernel, u_len=u_len, win=win),
        grid=(b, dil, u_len // step),
        in_specs=[
            pl.BlockSpec((1, 1, step, 256), lambda bi, r, ui: (bi, r, ui, qu)),
            pl.BlockSpec((1, 1, u_len, 256), lambda bi, r, ui: (bi, r, 0, qu + 1)),
            pl.BlockSpec((1, 1, u_len, 256), lambda bi, r, ui: (bi, r, 0, qu + 2)),
        ],
        out_specs=[
            pl.BlockSpec((1, 1, step, 256), lambda bi, r, ui: (bi, r, ui, 0)),
            pl.BlockSpec((1, 1, step, 256), lambda bi, r, ui: (bi, r, ui, 0)),
        ],
        out_shape=[
            jax.ShapeDtypeStruct((b, dil, u_len, 256), BF16),
            jax.ShapeDtypeStruct((b, dil, u_len, 256), F32),
        ],
        compiler_params=_params(3),
        name=f"dil_attn_{dil}",
    )(src, src, src)


def _na_bias_kernel(tbl_ref, o_ref):
    h = pl.program_id(0)
    d = pl.program_id(1)
    n_rel_r, n_rel_c = 2 * NA_ROWS - 1, 2 * NA_COLS - 1
    dl = jnp.clip(d - NA_BIAS_OFF, 0, n_rel_r - 1)
    dr = jnp.clip(d - NA_BIAS_OFF + 1, 0, n_rel_r - 1)
    qc = lax.broadcasted_iota(I32, (GRID_W, LANES), 0)
    ln = lax.broadcasted_iota(I32, (GRID_W, LANES), 1)
    left = ln < GRID_W
    rel = jnp.where(left, ln, ln - GRID_W) - qc + (NA_COLS - 1)
    acc = jnp.zeros((GRID_W, LANES), F32)
    for dd in range(n_rel_c):
        vl = tbl_ref[(h * n_rel_r + dl) * n_rel_c + dd]
        vr = tbl_ref[(h * n_rel_r + dr) * n_rel_c + dd]
        acc = jnp.where(rel == dd, jnp.where(left, vl, vr), acc)
    o_ref[0, 0] = acc


def _na_bias_tiles(na_bias_l):
    n_h = na_bias_l.shape[0]
    return pl.pallas_call(
        _na_bias_kernel,
        grid=(n_h, NA_BIAS_N),
        in_specs=[pl.BlockSpec(memory_space=pltpu.SMEM)],
        out_specs=pl.BlockSpec((1, 1, GRID_W, LANES), lambda h, d: (h, d, 0, 0)),
        out_shape=jax.ShapeDtypeStruct((n_h, NA_BIAS_N, GRID_W, LANES), F32),
        compiler_params=_params(2),
        name="na_bias",
    )(na_bias_l.reshape(-1))


def _na_kernel(q_ref, k_ref, v_ref, bias_ref, o_ref, *, n_rows):
    n_keys = NA_WIN_ROWS * GRID_W
    qi = lax.broadcasted_iota(I32, (NA_TQ, n_keys), 0)
    kj = lax.broadcasted_iota(I32, (NA_TQ, n_keys), 1)
    tiles = []
    for i in range(LOCAL_TILES):
        r0 = (pl.program_id(1) * LOCAL_TILES + i) * 2
        ws = jnp.clip(r0 - NA_ROWS // 2, 0, n_rows - NA_WIN_ROWS)
        start = pl.multiple_of(ws * GRID_W, GRID_W)
        qr, qc = r0 + qi // GRID_W, qi % GRID_W
        kr, kc = ws + kj // GRID_W, kj % GRID_W
        rs = jnp.clip(qr - NA_ROWS // 2, 0, n_rows - NA_ROWS)
        cs = jnp.clip(qc - NA_COLS // 2, 0, GRID_W - NA_COLS)
        valid = (kr >= rs) & (kr < rs + NA_ROWS) & (kc >= cs) & (kc < cs + NA_COLS)
        biases = []
        for h in range(4):
            halves = []
            for half in range(2):
                d0 = ws - r0 - half + (NA_ROWS - 1) + NA_BIAS_OFF
                halves.append(jnp.concatenate(
                    [bias_ref[h, pl.ds(d0 + 2 * m, 1)][0] for m in range(NA_WIN_ROWS // 2)], axis=1))
            biases.append(jnp.concatenate(halves, axis=0))
        tiles.append((q_ref[0, i * NA_TQ:(i + 1) * NA_TQ, :], k_ref[0, pl.ds(start, n_keys), :],
                      v_ref[0, pl.ds(start, n_keys), :], valid, biases))
    for i, (out, _) in enumerate(_local_heads(tiles)):
        o_ref[0, i * NA_TQ:(i + 1) * NA_TQ, :] = out.astype(BF16)


def _na_attention(zr3, bias_tiles):
    b, s, _ = zr3.shape
    n_rows = s // GRID_W
    assert n_rows >= NA_WIN_ROWS
    qu, ku, vu = (Z_QD - Z_HEAD) // 256, (Z_KD - Z_HEAD) // 256, (Z_VD - Z_HEAD) // 256
    return pl.pallas_call(
        functools.partial(_na_kernel, n_rows=n_rows),
        grid=(b, s // (LOCAL_TILES * NA_TQ)),
        in_specs=[
            pl.BlockSpec((1, LOCAL_TILES * NA_TQ, 256), lambda bi, i: (bi, i, qu)),
            pl.BlockSpec((1, s, 256), lambda bi, i: (bi, 0, ku)),
            pl.BlockSpec((1, s, 256), lambda bi, i: (bi, 0, vu)),
            _const_spec(bias_tiles.shape),
        ],
        out_specs=pl.BlockSpec((1, LOCAL_TILES * NA_TQ, 256), lambda bi, i: (bi, i, 0)),
        out_shape=jax.ShapeDtypeStruct((b, s, 256), BF16),
        compiler_params=_params(2),
        name="na_attn",
    )(zr3, zr3, zr3, bias_tiles)


def _merge_kernel(x_ref, g_ref, oa_ref, ob0_ref, ob1_ref, ob2_ref, l0_ref, l1_ref, l2_ref, oc_ref, od_ref,
                  wg_ref, wb_ref, wo_ref, y_ref, scr_ref):
    x = x_ref[...]
    hb = _rms(x, g_ref[...]).astype(BF16)

    n_scr = [0]

    def token_order(ref):
        dil = ref.shape[1]
        if dil == 1:
            return ref[0, 0].astype(F32)
        slot = n_scr[0]
        n_scr[0] += 2
        for c in range(2):
            for r in range(dil):
                scr_ref[slot + c, pl.ds(r, TM // dil, stride=dil), :] = (
                    ref[0, r, :, c * LANES:(c + 1) * LANES].astype(F32))
        return jnp.concatenate([scr_ref[slot], scr_ref[slot + 1]], axis=1)

    lses = [token_order(r) for r in (l0_ref, l1_ref, l2_ref)]
    mx = jnp.maximum(jnp.maximum(lses[0], lses[1]), lses[2])
    ws = [jnp.exp(l - mx) for l in lses]
    den = ws[0] + ws[1] + ws[2]
    obs = [token_order(r) for r in (ob0_ref, ob1_ref, ob2_ref)]
    ob = (ws[0] / den) * obs[0]
    for g in range(1, 3):
        ob = ob + (ws[g] / den) * obs[g]
    branches = (oa_ref[...], ob.astype(BF16), oc_ref[...], od_ref[...])
    merged = None
    for i, o in enumerate(branches):
        gate = jax.nn.sigmoid(_dot(hb, wg_ref[i]))
        term = gate * _dot(o, wb_ref[i])
        merged = term if merged is None else merged + term
    y_ref[...] = x + _dot(merged.astype(BF16), wo_ref[...])


def _merge(x2d, s, g_mix, oa, obs, lses, oc, od, w_gate, w_branch, w_out):
    t = x2d.shape[0]
    n_s = s // TM
    tok = lambda w: pl.BlockSpec((TM, w), lambda i: (i, 0))
    res = lambda a: pl.BlockSpec((1, a.shape[1], TM // a.shape[1], 256), lambda i: (i // n_s, 0, i % n_s, 0))
    n_slots = 2 * sum(a.shape[1] > 1 for a in list(obs) + list(lses))
    return pl.pallas_call(
        _merge_kernel,
        grid=(t // TM,),
        in_specs=[tok(D_MODEL), _const_spec((1, D_MODEL)), tok(256)] + [res(a) for a in obs] + [res(a) for a in lses] + [
            tok(256), tok(256),
            _const_spec(w_gate.shape), _const_spec(w_branch.shape), _const_spec(w_out.shape)],
        out_specs=tok(D_MODEL),
        out_shape=jax.ShapeDtypeStruct((t, D_MODEL), F32),
        scratch_shapes=[pltpu.VMEM((n_slots, TM, LANES), F32)],
        compiler_params=_params(1),
        name="merge",
    )(x2d, g_mix, oa, obs[0], obs[1], obs[2], lses[0], lses[1], lses[2], oc, od, w_gate, w_branch, w_out)


def _memkv_kernel(m_ref, g_ref, wk_ref, wv_ref, k_ref, v_ref):
    mb = _rms(m_ref[0], g_ref[...]).astype(BF16)
    k_ref[0] = _dot(mb, wk_ref[...]).astype(BF16)
    v_ref[0] = _dot(mb, wv_ref[...]).astype(BF16)


def _mem_kv(mem, g_mem, w_ck, w_cv):
    b, m_len, d = mem.shape
    blk = pl.BlockSpec((1, m_len, d), lambda bi: (bi, 0, 0))
    return pl.pallas_call(
        _memkv_kernel,
        grid=(b,),
        in_specs=[blk, _const_spec((1, d)), _const_spec((d, d)), _const_spec((d, d))],
        out_specs=[blk, blk],
        out_shape=[jax.ShapeDtypeStruct((b, m_len, d), BF16)] * 2,
        compiler_params=_params(1),
        name="mem_kv",
    )(mem, g_mem, w_ck, w_cv)


def _cross_kernel(x_ref, g_ref, wq_ref, k_ref, v_ref, wo_ref, gf_ref, wr_ref, y_ref, h_ref, aff_ref):
    x = x_ref[...]
    hb = _rms(x, g_ref[...]).astype(BF16)
    q = (_dot(hb, wq_ref[...]) * MEM_SCALE).astype(BF16)
    heads = [slice(h * MEM_HEAD_DIM, (h + 1) * MEM_HEAD_DIM) for h in range(MEM_HEADS)]
    scores = [_dot_nt(q[:, sl], k_ref[0, :, sl]) for sl in heads]
    probs = []
    for s in scores:
        m = jnp.max(s, axis=-1, keepdims=True)
        e = jnp.exp(s - m)
        probs.append((e / jnp.sum(e, axis=-1, keepdims=True)).astype(BF16))
    outs = [_dot(p, v_ref[0, :, sl]).astype(BF16) for p, sl in zip(probs, heads)]
    y = x + _dot(jnp.concatenate(outs, axis=1), wo_ref[...])
    y_ref[...] = y
    h3 = _rms(y, gf_ref[...]).astype(BF16)
    h_ref[...] = h3
    logits = _dot_nt(wr_ref[...], h3)
    m = jnp.max(logits, axis=0, keepdims=True)
    e = jnp.exp(logits - m)
    aff_ref[...] = e / jnp.sum(e, axis=0, keepdims=True)


def _cross(x2d, b, s, g_cross, w_cq, k_mem, v_mem, w_co, g_ffn, w_router_t):
    t = b * s
    n_s = s // TM
    m_len = k_mem.shape[1]
    tok = lambda: pl.BlockSpec((TM, D_MODEL), lambda i: (i, 0))
    mem_blk = pl.BlockSpec((1, m_len, D_MODEL), lambda i: (i // n_s, 0, 0))
    return pl.pallas_call(
        _cross_kernel,
        grid=(t // TM,),
        in_specs=[tok(), _const_spec((1, D_MODEL)), _const_spec((D_MODEL, D_MODEL)), mem_blk, mem_blk,
                  _const_spec((D_MODEL, D_MODEL)), _const_spec((1, D_MODEL)),
                  _const_spec((N_EXPERTS, D_MODEL))],
        out_specs=[tok(), tok(), pl.BlockSpec((N_EXPERTS, TM), lambda i: (0, i))],
        out_shape=[jax.ShapeDtypeStruct((t, D_MODEL), F32), jax.ShapeDtypeStruct((t, D_MODEL), BF16),
                   jax.ShapeDtypeStruct((N_EXPERTS, t), F32)],
        compiler_params=_params(1),
        name="cross",
    )(x2d, g_cross, w_cq, k_mem, v_mem, w_co, g_ffn, w_router_t)


def _select_kernel(aff_ref, pos_ref, *, n_tok, cap):
    n_cnt = n_tok // CNT_CH

    def count_ge(cand):
        def body(c, acc):
            bits = pltpu.bitcast(aff_ref[:, pl.ds(pl.multiple_of(c * CNT_CH, CNT_CH), CNT_CH)], I32)
            return acc + jnp.sum(jnp.where(bits >= cand, 1.0, 0.0), axis=1, keepdims=True)
        return lax.fori_loop(0, n_cnt, body, jnp.zeros((N_EXPERTS, 1), F32))

    def bit_body(i, prefix):
        cand = prefix | lax.shift_left(jnp.int32(1), 30 - i)
        return jnp.where(count_ge(cand) >= cap, cand, prefix)

    thr = lax.fori_loop(0, 31, bit_body, jnp.zeros((N_EXPERTS, 1), I32))
    need_eq = cap - count_ge(thr + 1)

    tri_r = lax.broadcasted_iota(I32, (CH, CH), 0)
    tri_c = lax.broadcasted_iota(I32, (CH, CH), 1)
    tri = jnp.where(tri_r <= tri_c, 1.0, 0.0).astype(BF16)

    def chunk_body(c, carry):
        c_eq, c_row = carry
        sl = pl.ds(pl.multiple_of(c * CH, CH), CH)
        bits = pltpu.bitcast(aff_ref[:, sl], I32)
        eq = jnp.where(bits == thr, 1.0, 0.0)
        gt = jnp.where(bits > thr, 1.0, 0.0)
        eq_before = c_eq + _dot(eq.astype(BF16), tri) - eq
        sel = gt + eq * jnp.where(eq_before < need_eq, 1.0, 0.0)
        row = c_row + _dot(sel.astype(BF16), tri) - sel
        pos_ref[:, sl] = jnp.where(sel > 0.5, row, -1.0).astype(I32)
        taken = jnp.sum(sel, axis=1, keepdims=True)
        c_row = c_row + jnp.ceil(taken * (1.0 / ROW_ALIGN)) * ROW_ALIGN
        return (c_eq + jnp.sum(eq, axis=1, keepdims=True), c_row)

    zero = jnp.zeros((N_EXPERTS, 1), F32)
    lax.fori_loop(0, n_tok // CH, chunk_body, (zero, zero))


def _select(aff_t, cap):
    n_tok = aff_t.shape[1]
    return pl.pallas_call(
        functools.partial(_select_kernel, n_tok=n_tok, cap=cap),
        grid=(1,),
        in_specs=[_const_spec(aff_t.shape)],
        out_specs=_const_spec(aff_t.shape),
        out_shape=jax.ShapeDtypeStruct(aff_t.shape, I32),
        compiler_params=_params(1),
        name="select",
    )(aff_t)


def _window_hits(pos_ref, base_ref, e, k, n_k, p):
    rows = base_ref[e * n_k + k] + p * WIN + lax.broadcasted_iota(I32, (WIN, CH), 0)
    return pos_ref[e:e + 1, :] == rows


def _window_to_hbm(scr, hbm, sem, e, row0):
    return pltpu.make_async_copy(scr.at[pl.ds(e * WIN, WIN)], hbm.at[e, pl.ds(row0, WIN)], sem.at[e])


def _window_from_hbm(hbm, scr, sem, e, row0):
    return pltpu.make_async_copy(hbm.at[e, pl.ds(row0, WIN)], scr.at[pl.ds(e * WIN, WIN)], sem.at[e])


def _gather_kernel(base_ref, npass_ref, h_ref, pos_ref, aff_ref, xe_in, gt_in, xe_ref, gt_ref,
                   x_scr, g_scr, sem_x, sem_g, *, n_k):
    del xe_in, gt_in
    k = pl.program_id(0)
    n_pass = npass_ref[k]
    xs, gs = x_scr.at[k % 2], g_scr.at[k % 2]

    def window_copies(kk, p):
        copies = []
        for e in range(N_EXPERTS):
            row0 = pl.multiple_of(base_ref[e * n_k + kk] + p * WIN, ROW_ALIGN)
            copies.append(_window_to_hbm(x_scr.at[kk % 2], xe_ref, sem_x, e, row0))
            copies.append(_window_to_hbm(g_scr.at[kk % 2], gt_ref, sem_g, e, row0))
        return copies

    def one_pass(p, carry):
        for e0 in range(0, N_EXPERTS, GATHER_GROUP):
            hits = [_window_hits(pos_ref, base_ref, e, k, n_k, p) for e in range(e0, e0 + GATHER_GROUP)]
            onehot = jnp.concatenate([jnp.where(h, 1.0, 0.0).astype(BF16) for h in hits], axis=0)
            xs[e0 * WIN:(e0 + GATHER_GROUP) * WIN, :] = _dot(onehot, h_ref[...]).astype(BF16)
            for i, h in enumerate(hits):
                e = e0 + i
                gate = jnp.sum(jnp.where(h, aff_ref[e:e + 1, :], 0.0), axis=1, keepdims=True)
                gs[e * WIN:(e + 1) * WIN, :] = jnp.broadcast_to(gate, (WIN, LANES))

        @pl.when((p == 0) & (k > 0))
        def _():
            for c in window_copies(k - 1, npass_ref[k - 1] - 1):
                c.wait()

        for c in window_copies(k, p):
            c.start()

        @pl.when(p < n_pass - 1)
        def _():
            for c in window_copies(k, p):
                c.wait()

        return carry

    lax.fori_loop(0, n_pass, one_pass, 0)

    @pl.when(k == n_k - 1)
    def _():
        for c in window_copies(k, n_pass - 1):
            c.wait()


def _expert_gather(base, npass, h3, pos, aff_t, rows_pad):
    n_tok = h3.shape[0]
    n_k = n_tok // CH
    xe0 = jnp.zeros((N_EXPERTS, rows_pad, D_MODEL), BF16)
    gt0 = jnp.zeros((N_EXPERTS, rows_pad, LANES), F32)
    grid_spec = pltpu.PrefetchScalarGridSpec(
        num_scalar_prefetch=2,
        grid=(n_k,),
        in_specs=[
            pl.BlockSpec((CH, D_MODEL), lambda k, *_: (k, 0)),
            pl.BlockSpec((N_EXPERTS, CH), lambda k, *_: (0, k)),
            pl.BlockSpec((N_EXPERTS, CH), lambda k, *_: (0, k)),
            pl.BlockSpec(memory_space=pl.ANY),
            pl.BlockSpec(memory_space=pl.ANY),
        ],
        out_specs=[pl.BlockSpec(memory_space=pl.ANY), pl.BlockSpec(memory_space=pl.ANY)],
        scratch_shapes=[pltpu.VMEM((2, N_EXPERTS * WIN, D_MODEL), BF16), pltpu.VMEM((2, N_EXPERTS * WIN, LANES), F32),
                        pltpu.SemaphoreType.DMA((N_EXPERTS,)), pltpu.SemaphoreType.DMA((N_EXPERTS,))],
    )
    return pl.pallas_call(
        functools.partial(_gather_kernel, n_k=n_k),
        grid_spec=grid_spec,
        out_shape=[jax.ShapeDtypeStruct(xe0.shape, BF16), jax.ShapeDtypeStruct(gt0.shape, F32)],
        input_output_aliases={5: 0, 6: 1},
        compiler_params=_params(1),
        name="expert_gather",
    )(base, npass, h3, pos, aff_t, xe0, gt0)


def _ffn_kernel(end_ref, xe_ref, gt_ref, wg_ref, wu_ref, wd_ref, ye_ref):
    e, j = pl.program_id(0), pl.program_id(1)

    @pl.when(j * RB < end_ref[e])
    def _():
        xe = xe_ref[0]
        hid = jax.nn.silu(_dot(xe, wg_ref[0])) * _dot(xe, wu_ref[0])
        gate = jnp.concatenate([gt_ref[0]] * (D_MODEL // LANES), axis=1)
        ye_ref[0] = (_dot(hid.astype(BF16), wd_ref[0]) * gate).astype(BF16)

    @pl.when(j * RB >= end_ref[e])
    def _():
        ye_ref[0] = jnp.zeros((RB, D_MODEL), BF16)


def _expert_ffn(end, xe, gt, w_e_gate, w_e_up, w_e_down):
    rows_pad = xe.shape[1]
    grid_spec = pltpu.PrefetchScalarGridSpec(
        num_scalar_prefetch=1,
        grid=(N_EXPERTS, rows_pad // RB),
        in_specs=[
            pl.BlockSpec((1, RB, D_MODEL), lambda e, j, *_: (e, j, 0)),
            pl.BlockSpec((1, RB, LANES), lambda e, j, *_: (e, j, 0)),
            pl.BlockSpec((1, D_MODEL, EXPERT_FF), lambda e, j, *_: (e, 0, 0)),
            pl.BlockSpec((1, D_MODEL, EXPERT_FF), lambda e, j, *_: (e, 0, 0)),
            pl.BlockSpec((1, EXPERT_FF, D_MODEL), lambda e, j, *_: (e, 0, 0)),
        ],
        out_specs=pl.BlockSpec((1, RB, D_MODEL), lambda e, j, *_: (e, j, 0)),
    )
    return pl.pallas_call(
        _ffn_kernel,
        grid_spec=grid_spec,
        out_shape=jax.ShapeDtypeStruct(xe.shape, BF16),
        compiler_params=_params(2),
        name="expert_ffn",
    )(end, xe, gt, w_e_gate, w_e_up, w_e_down)


def _scatter_kernel(base_ref, npass_ref, x_ref, pos_ref, ye_ref, gfin_ref, y_ref, stage, sem,
                    *, final_norm, n_k):
    k = pl.program_id(0)
    slot = k % 2
    y_ref[...] = x_ref[...]

    def window_copies(kk, p, s):
        copies = []
        for e in range(N_EXPERTS):
            row0 = pl.multiple_of(base_ref[e * n_k + kk] + p * WIN, ROW_ALIGN)
            copies.append(_window_from_hbm(ye_ref, stage.at[s], sem.at[s], e, row0))
        return copies

    def onehot(p):
        return jnp.concatenate(
            [jnp.where(_window_hits(pos_ref, base_ref, e, k, n_k, p), 1.0, 0.0).astype(BF16)
             for e in range(N_EXPERTS)], axis=0)

    @pl.when(k == 0)
    def _():
        for c in window_copies(0, 0, 0):
            c.start()

    @pl.when(k + 1 < n_k)
    def _():
        for c in window_copies(k + 1, 0, 1 - slot):
            c.start()

    hot = onehot(0)
    for c in window_copies(k, 0, slot):
        c.wait()
    y_ref[...] += _dot_tn(hot, stage[slot])

    def later_pass(p, carry):
        copies = window_copies(k, p, slot)
        for c in copies:
            c.start()
        hot = onehot(p)
        for c in copies:
            c.wait()
        y_ref[...] += _dot_tn(hot, stage[slot])
        return carry

    lax.fori_loop(1, npass_ref[k], later_pass, 0)
    if final_norm:
        y_ref[...] = _rms(y_ref[...], gfin_ref[...])


def _expert_scatter(base, npass, x2d, pos, ye, g_final, final_norm):
    t = x2d.shape[0]
    n_k = t // CH
    grid_spec = pltpu.PrefetchScalarGridSpec(
        num_scalar_prefetch=2,
        grid=(n_k,),
        in_specs=[
            pl.BlockSpec((CH, D_MODEL), lambda k, *_: (k, 0)),
            pl.BlockSpec((N_EXPERTS, CH), lambda k, *_: (0, k)),
            pl.BlockSpec(memory_space=pl.ANY),
            pl.BlockSpec((1, D_MODEL), lambda k, *_: (0, 0)),
        ],
        out_specs=pl.BlockSpec((CH, D_MODEL), lambda k, *_: (k, 0)),
        scratch_shapes=[pltpu.VMEM((2, N_EXPERTS * WIN, D_MODEL), BF16), pltpu.SemaphoreType.DMA((2, N_EXPERTS))],
    )
    return pl.pallas_call(
        functools.partial(_scatter_kernel, final_norm=final_norm, n_k=n_k),
        grid_spec=grid_spec,
        out_shape=jax.ShapeDtypeStruct((t, D_MODEL), F32),
        compiler_params=_params(1),
        name="expert_scatter",
    )(base, npass, x2d, pos, ye, g_final)


def _segments(pos):
    n_e, n_tok = pos.shape
    n_k = n_tok // CH
    cnt = jnp.sum((pos >= 0).reshape(n_e, n_k, CH), axis=-1, dtype=I32)
    padded = (cnt + (ROW_ALIGN - 1)) // ROW_ALIGN * ROW_ALIGN
    end = jnp.cumsum(padded, axis=1)
    base = end - padded
    npass = jnp.maximum(jnp.max((cnt + (WIN - 1)) // WIN, axis=0), 1)
    return base.reshape(-1).astype(I32), npass.astype(I32), end[:, -1].astype(I32)


def _rows_pad(n_tok, cap):
    rows = cap + ROW_ALIGN * (n_tok // CH) + CH
    return (rows + RB - 1) // RB * RB


def _rope_tables(s):
    pos = jnp.arange(s, dtype=F32)[:, None]
    lane = jnp.arange(LANES)
    tabs = []
    for period, base, rot in ((LANES, MLA_NOPE, MLA_ROPE), (HEAD_DIM, 0, HEAD_DIM // 4), (DIFF_QK, 0, DIFF_QK // 4)):
        half = rot // 2
        rel = lane % period - base
        inv_freq = ROPE_THETA ** (-jnp.arange(half, dtype=F32) * (2.0 / rot))
        ang = pos * inv_freq[None, :]
        cos, sin = jnp.cos(ang), jnp.sin(ang)
        idx = jnp.clip(jnp.where(rel >= half, rel - half, rel), 0, half - 1)
        first = (rel >= 0) & (rel < half)
        second = (rel >= half) & (rel < rot)
        tabs.append(jnp.where(first | second, cos[:, idx], 1.0))
        tabs.append(jnp.where(first, -sin[:, idx], 0.0))
        tabs.append(jnp.where(second, sin[:, idx], 0.0))
    return jnp.stack(tabs).astype(F32)


def _prep_layer(l, w_in, g_mix, g_cq, w_uq, g_ckv, w_ukv, lam_q1, lam_k1, lam_q2, lam_k2, g_diff, na_bias,
                w_gate, w_branch, w_out, g_cross, g_mem, w_cq, w_ck, w_cv, w_co, g_ffn, w_router,
                w_e_gate, w_e_up, w_e_down):
    d = D_MODEL
    wi = w_in[l]
    kpe0 = MLA_Q_RANK + MLA_KV_RANK
    z64 = jnp.zeros((d, MLA_NOPE), F32)
    z32 = jnp.zeros((d, LANES - MLA_NOPE - MLA_ROPE), F32)
    rest = wi[:, kpe0 + MLA_ROPE:]
    n_g = len(DIL_GROUPS)
    n_dil = n_g * Z_DIL_GROUP
    vc0 = n_dil + 512
    dil_cols = rest[:, :n_dil].reshape(d, 3, n_g, 256).transpose(0, 2, 1, 3).reshape(d, n_dil)
    w_z = jnp.concatenate([wi[:, :kpe0], z64, wi[:, kpe0:kpe0 + MLA_ROPE], z32, dil_cols[:, :Z_DIL_GROUP],
                           rest[:, n_dil:vc0], rest[:, vc0 + 256:], dil_cols[:, Z_DIL_GROUP:]], axis=1)
    w_vct = rest[:, vc0:vc0 + 256].T
    qh = MLA_NOPE + MLA_ROPE
    wuq = w_uq[l].reshape(MLA_Q_RANK, MLA_HEADS, qh)
    wuq = jnp.pad(wuq, ((0, 0), (0, 0), (0, LANES - qh))).reshape(MLA_Q_RANK, MLA_HEADS * LANES)
    wukv = w_ukv[l].reshape(MLA_KV_RANK, MLA_HEADS, MLA_NOPE + HEAD_DIM)
    wk = jnp.pad(wukv[:, :, :MLA_NOPE], ((0, 0), (0, 0), (0, LANES - MLA_NOPE))).reshape(MLA_KV_RANK, MLA_HEADS * LANES)
    wv = wukv[:, :, MLA_NOPE:].reshape(MLA_KV_RANK, MLA_HEADS * HEAD_DIM)
    return dict(
        g_mix=g_mix[l][None], w_z=w_z.astype(BF16), g_cq=g_cq[l][None], g_ckv=g_ckv[l][None],
        w_uq=wuq.astype(BF16), w_k=wk.astype(BF16), w_vt=wv.T.astype(BF16), w_vct=w_vct.astype(BF16),
        lam=jnp.stack([lam_q1[l], lam_k1[l], lam_q2[l], lam_k2[l]]).astype(F32),
        g_diff=jnp.tile(g_diff[l], 2)[None], na_bias=na_bias[l],
        w_gate=w_gate[l].astype(BF16), w_branch=w_branch[l].astype(BF16), w_out=w_out[l].astype(BF16),
        g_cross=g_cross[l][None], g_mem=g_mem[l][None], w_cq=w_cq[l].astype(BF16), w_ck=w_ck[l].astype(BF16),
        w_cv=w_cv[l].astype(BF16), w_co=w_co[l].astype(BF16), g_ffn=g_ffn[l][None],
        w_router_t=w_router[l].T.astype(BF16), w_e_gate=w_e_gate[l].astype(BF16),
        w_e_up=w_e_up[l].astype(BF16), w_e_down=w_e_down[l].astype(BF16),
    )


def _trunk(x, mem, layers, bias_tiles, g_final):
    b, s, d = x.shape
    t = b * s
    cap = CAPACITY_FACTOR * t // N_EXPERTS
    assert s % TM == 0 and s % TQ == 0 and s % TK == 0 and t % CH == 0 and cap % RB == 0
    rope_tab = _rope_tables(s)
    x2d = x.reshape(t, d)
    for l, p in enumerate(layers):
        zr, qa, ka, vat, vct, d1, d2 = _project(x2d, b, s, p["g_mix"], p["w_z"], rope_tab, p["g_cq"], p["g_ckv"],
                                                p["w_uq"], p["w_k"], p["w_vt"], p["w_vct"])
        zr3 = zr.reshape(b, s, ZR_COLS)
        oa = _mla_attention(qa, ka, vat).reshape(t, 256)
        dil = [_dil_attention(zr.reshape(b, 1, s, ZR_COLS), (Z_DIL0 - Z_HEAD) // 256),
               _dil_attention(d1, 0), _dil_attention(d2, 0)]
        lam_init = 0.8 - 0.6 * math.exp(-0.3 * l)
        oc = _diff_attention(zr3, vct, p["lam"], p["g_diff"], lam_init).reshape(t, 256)
        od = _na_attention(zr3, bias_tiles[l]).reshape(t, 256)
        x2d = _merge(x2d, s, p["g_mix"], oa, [o for o, _ in dil], [ls for _, ls in dil], oc, od,
                     p["w_gate"], p["w_branch"], p["w_out"])
        k_mem, v_mem = _mem_kv(mem, p["g_mem"], p["w_ck"], p["w_cv"])
        x2d, h3, aff_t = _cross(x2d, b, s, p["g_cross"], p["w_cq"], k_mem, v_mem, p["w_co"], p["g_ffn"],
                                p["w_router_t"])
        pos = _select(aff_t, cap)
        base, npass, end = _segments(pos)
        xe, gt = _expert_gather(base, npass, h3, pos, aff_t, _rows_pad(t, cap))
        ye = _expert_ffn(end, xe, gt, p["w_e_gate"], p["w_e_up"], p["w_e_down"])
        x2d = _expert_scatter(base, npass, x2d, pos, ye, g_final[None], final_norm=(l == len(layers) - 1))
    return x2d.reshape(b, s, d)


def kernel(x_prompt, x_sample, mem_prompt, mem_sample, w_in, g_mix, g_cq, w_uq, g_ckv, w_ukv, lam_q1, lam_k1, lam_q2, lam_k2, g_diff, na_bias, w_gate, w_branch, w_out, g_cross, g_mem, w_cq, w_ck, w_cv, w_co, g_ffn, w_router, w_e_gate, w_e_up, w_e_down, g_final):
    depth = w_in.shape[0]
    layers = [_prep_layer(l, w_in, g_mix, g_cq, w_uq, g_ckv, w_ukv, lam_q1, lam_k1, lam_q2, lam_k2, g_diff,
                          na_bias, w_gate, w_branch, w_out, g_cross, g_mem, w_cq, w_ck, w_cv, w_co, g_ffn,
                          w_router, w_e_gate, w_e_up, w_e_down) for l in range(depth)]
    bias_tiles = [_na_bias_tiles(p["na_bias"]) for p in layers]
    y_prompt = _trunk(x_prompt, mem_prompt, layers, bias_tiles, g_final)
    y_sample = _trunk(x_sample, mem_sample, layers, bias_tiles, g_final)
    return (y_prompt, y_sample)
```

```python
import functools
import math

import jax
import jax.numpy as jnp
from jax import lax
from jax.experimental import pallas as pl
from jax.experimental.pallas import tpu as pltpu

BF16 = jnp.bfloat16
F32 = jnp.float32
I32 = jnp.int32

D_MODEL = 1024
EPS = 1e-6
NEG = -1e30
ROPE_THETA = 500000.0
HEAD_DIM = 64
MLA_HEADS = 4
MLA_Q_RANK = 256
MLA_KV_RANK = 128
MLA_NOPE = 64
MLA_ROPE = 32
MLA_SCALE = (MLA_NOPE + MLA_ROPE) ** -0.5
DIL_GROUPS = ((128, 1), (512, 4), (2048, 16))
DIL_SIDE = 64
DIL_SCALE = HEAD_DIM ** -0.5
DIFF_QK = 32
DIFF_SCALE = DIFF_QK ** -0.5
NA_ROWS = 8
NA_COLS = 16
GRID_W = 64
NA_SCALE = HEAD_DIM ** -0.5
MEM_HEADS = 4
MEM_HEAD_DIM = D_MODEL // MEM_HEADS
MEM_SCALE = MEM_HEAD_DIM ** -0.5
N_EXPERTS = 16
EXPERT_FF = 512
CAPACITY_FACTOR = 2

LANES = 128
VMEM_LIMIT = 56 * 1024 * 1024

Z_CQ = 0
Z_CKV = 256
Z_KPE = 384
Z_HEAD = 512
Z_DIL0 = 512
Z_DIL_GROUP = 768
Z_QC = 1280
Z_KC = 1536
Z_QD = 1792
Z_KD = 2048
Z_VD = 2304
Z_DILX = 2560
Z_COLS = Z_DILX + 2 * Z_DIL_GROUP
ZR_COLS = Z_DILX - Z_HEAD
LOG2E = math.log2(math.e)

TM = 512
TQ = 256
TK = 512
MLA_UNROLL = 4
DIFF_UNROLL = 2
TU = 128
DIL_TILES = 2
LOCAL_TILES = 2
NA_TQ = 128
NA_WIN_ROWS = 10
NA_BIAS_OFF = 2
NA_BIAS_N = 18
RB = 512
CH = 512
WIN = 96
XE_COLS = D_MODEL + LANES
ROW_ALIGN = 16
GATHER_GROUP = 4
CNT_CH = 4096


def _params(n_grid):
    return pltpu.CompilerParams(dimension_semantics=("arbitrary",) * n_grid,
                                vmem_limit_bytes=VMEM_LIMIT)


def _rms(x, g):
    return x * lax.rsqrt(jnp.mean(x * x, axis=-1, keepdims=True) + EPS) * g


def _dot(a, b):
    return jnp.dot(a, b, preferred_element_type=F32)


def _dot_nt(a, b):
    return lax.dot_general(a, b, (((1,), (1,)), ((), ())), preferred_element_type=F32)


def _dot_tn(a, b):
    return lax.dot_general(a, b, (((0,), (0,)), ((), ())), preferred_element_type=F32)


def _rope_block(x, c, sa, sb, half):
    return x * c + pltpu.roll(x, LANES - half, 1) * sa + pltpu.roll(x, half, 1) * sb


def _proj_kernel(x_ref, g_ref, w_ref, rope_ref, gcq_ref, gckv_ref, wuq_ref, wk_ref, wvt_ref, wvct_ref,
                 zr_ref, qa_ref, ka_ref, vat_ref, vct_ref, d1_ref, d2_ref, scr_ref):
    hb = _rms(x_ref[...], g_ref[...]).astype(BF16)

    def rope(x, kind, half):
        return _rope_block(x, rope_ref[3 * kind], rope_ref[3 * kind + 1], rope_ref[3 * kind + 2], half)

    z0 = _dot(hb, w_ref[:, 0:Z_HEAD])
    cq = _rms(z0[:, Z_CQ:Z_CQ + MLA_Q_RANK], gcq_ref[...]).astype(BF16)
    ckv = _rms(z0[:, Z_CKV:Z_CKV + MLA_KV_RANK], gckv_ref[...]).astype(BF16)
    kpe = rope(z0[:, Z_KPE:Z_KPE + LANES], 0, MLA_ROPE // 2)
    qa = _dot(cq, wuq_ref[...]) * (MLA_SCALE * LOG2E)
    kn = _dot(ckv, wk_ref[...])
    for h in range(MLA_HEADS):
        sl = slice(h * LANES, (h + 1) * LANES)
        qa_ref[0, h] = rope(qa[:, sl], 0, MLA_ROPE // 2).astype(BF16)
        ka_ref[0, h] = (kn[:, sl] + kpe).astype(BF16)
    vat_ref[0] = _dot_nt(wvt_ref[...], ckv).astype(BF16)
    vct_ref[0] = _dot_nt(wvct_ref[...], hb).astype(BF16)

    n_scr = [0]

    def seg(start, scale, kind, half, dil_out=None):
        z = _dot(hb, w_ref[:, start:start + 256])
        if scale is not None:
            z = z * scale
        for j in range(2):
            zb = z[:, j * LANES:(j + 1) * LANES]
            if kind is not None:
                zb = rope(zb, kind, half)
            if dil_out is None:
                o = start - Z_HEAD + j * LANES
                zr_ref[:, o:o + LANES] = zb.astype(BF16)
            else:
                out_ref, col, dil = dil_out
                slot = n_scr[0] % scr_ref.shape[0]
                n_scr[0] += 1
                scr_ref[slot] = zb
                for r in range(dil):
                    out_ref[0, r, :, col + j * LANES:col + (j + 1) * LANES] = (
                        scr_ref[slot, pl.ds(r, TM // dil, stride=dil), :].astype(BF16))

    dil_half = HEAD_DIM // 4 // 2
    diff_half = DIFF_QK // 4 // 2
    seg(Z_DIL0, DIL_SCALE, 1, dil_half)
    seg(Z_DIL0 + 256, None, 1, dil_half)
    seg(Z_DIL0 + 512, None, None, 0)
    for g, out_ref in ((1, d1_ref), (2, d2_ref)):
        base = Z_DILX + (g - 1) * Z_DIL_GROUP
        dil = DIL_GROUPS[g][1]
        seg(base, DIL_SCALE, 1, dil_half, (out_ref, 0, dil))
        seg(base + 256, None, 1, dil_half, (out_ref, 256, dil))
        seg(base + 512, None, None, 0, (out_ref, 512, dil))
    seg(Z_QC, DIFF_SCALE * LOG2E, 2, diff_half)
    seg(Z_KC, None, 2, diff_half)
    seg(Z_QD, NA_SCALE, None, 0)
    seg(Z_KD, None, None, 0)
    seg(Z_VD, None, None, 0)


def _const_spec(shape):
    nd = len(shape)
    return pl.BlockSpec(shape, lambda *_: (0,) * nd)


def _project(x2d, b, s, g_mix, w_z, rope_tab, g_cq, g_ckv, w_uq, w_k, w_vt, w_vct):
    t = b * s
    n_s = s // TM
    v_cols = MLA_HEADS * HEAD_DIM
    dils = [dl for _, dl in DIL_GROUPS[1:]]
    return pl.pallas_call(
        _proj_kernel,
        grid=(t // TM,),
        in_specs=[
            pl.BlockSpec((TM, D_MODEL), lambda i: (i, 0)),
            _const_spec((1, D_MODEL)),
            _const_spec((D_MODEL, Z_COLS)),
            pl.BlockSpec((9, TM, LANES), lambda i: (0, i % n_s, 0)),
            _const_spec((1, MLA_Q_RANK)),
            _const_spec((1, MLA_KV_RANK)),
            _const_spec((MLA_Q_RANK, MLA_HEADS * LANES)),
            _const_spec((MLA_KV_RANK, MLA_HEADS * LANES)),
            _const_spec((v_cols, MLA_KV_RANK)),
            _const_spec((v_cols, D_MODEL)),
        ],
        out_specs=[
            pl.BlockSpec((TM, ZR_COLS), lambda i: (i, 0)),
            pl.BlockSpec((1, MLA_HEADS, TM, LANES), lambda i: (i // n_s, 0, i % n_s, 0)),
            pl.BlockSpec((1, MLA_HEADS, TM, LANES), lambda i: (i // n_s, 0, i % n_s, 0)),
            pl.BlockSpec((1, v_cols, TM), lambda i: (i // n_s, 0, i % n_s)),
            pl.BlockSpec((1, v_cols, TM), lambda i: (i // n_s, 0, i % n_s)),
        ] + [pl.BlockSpec((1, dl, TM // dl, Z_DIL_GROUP), lambda i: (i // n_s, 0, i % n_s, 0)) for dl in dils],
        out_shape=[
            jax.ShapeDtypeStruct((t, ZR_COLS), BF16),
            jax.ShapeDtypeStruct((b, MLA_HEADS, s, LANES), BF16),
            jax.ShapeDtypeStruct((b, MLA_HEADS, s, LANES), BF16),
            jax.ShapeDtypeStruct((b, v_cols, s), BF16),
            jax.ShapeDtypeStruct((b, v_cols, s), BF16),
        ] + [jax.ShapeDtypeStruct((b, dl, s // dl, Z_DIL_GROUP), BF16) for dl in dils],
        scratch_shapes=[pltpu.VMEM((4, TM, LANES), F32)],
        compiler_params=_params(1),
        name="proj",
    )(x2d, g_mix, w_z, rope_tab, g_cq, g_ckv, w_uq, w_k, w_vt, w_vct)


def _chunk(c):
    return pl.ds(pl.multiple_of(c * TK, TK), TK)


def _dense_attention(qs, k_at, vt_at, s_ref, p_ref, n_chunks, unroll):
    n = len(qs)

    def step(c, cur, nxt, carry):
        rows_next = _chunk(jnp.minimum(c + 1, n_chunks - 1))
        rows_prev = _chunk(jnp.maximum(c - 1, 0))
        new = []
        for i in range(n):
            m, l, acc, alpha_prev = carry[i]
            s_ref[i, nxt] = _dot_nt(k_at(i, rows_next), qs[i])
            acc = alpha_prev * acc + _dot(vt_at(i, rows_prev), p_ref[i, nxt])
            st = s_ref[i, cur]
            m_new = jnp.maximum(m, jnp.max(st, axis=0, keepdims=True))
            alpha = jnp.exp2(m - m_new)
            p = jnp.exp2(st - m_new)
            l = alpha * l + jnp.sum(p, axis=0, keepdims=True)
            p_ref[i, cur] = p.astype(BF16)
            new.append((m_new, l, acc, alpha))
        return tuple(new)

    def body(j, carry):
        for u in range(unroll):
            carry = step(unroll * j + u, u % 2, (u + 1) % 2, carry)
        return carry

    for i in range(n):
        s_ref[i, 0] = _dot_nt(k_at(i, pl.ds(0, TK)), qs[i])
        p_ref[i, 1] = jnp.zeros((TK, TQ), BF16)
    init = tuple((jnp.full((1, TQ), NEG, F32), jnp.zeros((1, TQ), F32), jnp.zeros((HEAD_DIM, TQ), F32),
                  jnp.ones((1, TQ), F32)) for _ in range(n))
    assert unroll % 2 == 0 and n_chunks % unroll == 0
    res = lax.fori_loop(0, n_chunks // unroll, body, init)
    out = []
    for i in range(n):
        _, l, acc, alpha = res[i]
        acc = alpha * acc + _dot(vt_at(i, pl.ds((n_chunks - 1) * TK, TK)), p_ref[i, 1])
        out.append((acc, l))
    return out


def _mla_kernel(q_ref, k_ref, vt_ref, o_ref, s_ref, p_ref, *, n_chunks):
    res = _dense_attention(
        [q_ref[0, hh] for hh in range(2)],
        lambda i, rows: k_ref[0, i, rows, :],
        lambda i, rows: vt_ref[0, i * HEAD_DIM:(i + 1) * HEAD_DIM, rows],
        s_ref, p_ref, n_chunks, MLA_UNROLL)
    out_t = jnp.concatenate([acc / l for acc, l in res], axis=0)
    o_ref[0] = out_t.T.astype(BF16)


def _mla_attention(qa, ka, vat):
    b, _, s, _ = qa.shape
    return pl.pallas_call(
        functools.partial(_mla_kernel, n_chunks=s // TK),
        grid=(b, 2, s // TQ),
        in_specs=[
            pl.BlockSpec((1, 2, TQ, LANES), lambda bi, p, qi: (bi, p, qi, 0)),
            pl.BlockSpec((1, 2, s, LANES), lambda bi, p, qi: (bi, p, 0, 0)),
            pl.BlockSpec((1, LANES, s), lambda bi, p, qi: (bi, p, 0)),
        ],
        out_specs=pl.BlockSpec((1, TQ, LANES), lambda bi, p, qi: (bi, qi, p)),
        out_shape=jax.ShapeDtypeStruct((b, s, 2 * LANES), BF16),
        scratch_shapes=[pltpu.VMEM((2, 2, TK, TQ), F32), pltpu.VMEM((2, 2, TK, TQ), BF16)],
        compiler_params=_params(3),
        name="mla_attn",
    )(qa, ka, vat)


def _diff_kernel(q_ref, k_ref, vt_ref, lam_ref, g_ref, o_ref, s_ref, p_ref, *, n_chunks, lam_init):
    lane = lax.broadcasted_iota(I32, (TQ, LANES), 1)
    lv = lam_ref[...]
    lam = (jnp.exp(jnp.sum(lv[0:1] * lv[1:2], axis=-1, keepdims=True))
           - jnp.exp(jnp.sum(lv[2:3] * lv[3:4], axis=-1, keepdims=True)) + lam_init)
    q = q_ref[0]
    qs = [jnp.where((lane >= i * DIFF_QK) & (lane < (i + 1) * DIFF_QK), q, jnp.zeros_like(q)) for i in range(4)]
    res = _dense_attention(
        qs,
        lambda i, rows: k_ref[0, rows, :],
        lambda i, rows: vt_ref[0, (i // 2) * HEAD_DIM:(i // 2 + 1) * HEAD_DIM, rows],
        s_ref, p_ref, n_chunks, DIFF_UNROLL)
    parts = [acc / l for acc, l in res]
    out_t = jnp.concatenate([parts[2 * hh] - lam * parts[2 * hh + 1] for hh in range(2)], axis=0)
    o = out_t.T
    head0 = lane < HEAD_DIM
    sq = o * o
    ms0 = jnp.sum(jnp.where(head0, sq, 0.0), axis=-1, keepdims=True)
    ms1 = jnp.sum(jnp.where(head0, 0.0, sq), axis=-1, keepdims=True)
    ms = jnp.where(head0, ms0, ms1) * (1.0 / HEAD_DIM)
    o_ref[0] = (o * lax.rsqrt(ms + EPS) * g_ref[...] * (1.0 - lam_init)).astype(BF16)


def _diff_attention(zr3, vct, lam_vecs, g_diff2, lam_init):
    b, s, _ = zr3.shape
    qb, kb = (Z_QC - Z_HEAD) // LANES, (Z_KC - Z_HEAD) // LANES
    return pl.pallas_call(
        functools.partial(_diff_kernel, n_chunks=s // TK, lam_init=lam_init),
        grid=(b, 2, s // TQ),
        in_specs=[
            pl.BlockSpec((1, TQ, LANES), lambda bi, p, qi: (bi, qi, qb + p)),
            pl.BlockSpec((1, s, LANES), lambda bi, p, qi: (bi, 0, kb + p)),
            pl.BlockSpec((1, LANES, s), lambda bi, p, qi: (bi, p, 0)),
            _const_spec((4, DIFF_QK)),
            _const_spec((1, LANES)),
        ],
        out_specs=pl.BlockSpec((1, TQ, LANES), lambda bi, p, qi: (bi, qi, p)),
        out_shape=jax.ShapeDtypeStruct((b, s, 2 * LANES), BF16),
        scratch_shapes=[pltpu.VMEM((4, 2, TK, TQ), F32), pltpu.VMEM((4, 2, TK, TQ), BF16)],
        compiler_params=_params(3),
        name="diff_attn",
    )(zr3, zr3, vct, lam_vecs, g_diff2)


def _local_heads(tiles):
    n = tiles[0][0].shape[0]
    lane = lax.broadcasted_iota(I32, (n, 4 * HEAD_DIM), 1)
    in_head = [(lane >= h * HEAD_DIM) & (lane < (h + 1) * HEAD_DIM) for h in range(4)]
    scores = [[_dot_nt(jnp.where(in_head[h], q, jnp.zeros_like(q)), kw) for h in range(4)]
              for q, kw, _, _, _ in tiles]
    probs, lses = [], []
    for (_, _, _, valid, biases), sc in zip(tiles, scores):
        tile_p, tile_l = [], []
        for h in range(4):
            s = sc[h] if biases is None else sc[h] + biases[h]
            s = jnp.where(valid, s, NEG)
            m = jnp.max(s, axis=-1, keepdims=True)
            e = jnp.exp(s - m)
            l = jnp.sum(e, axis=-1, keepdims=True)
            tile_p.append((e / l).astype(BF16))
            tile_l.append(m + jnp.log(l))
        probs.append(tile_p)
        lses.append(tile_l)
    outs = [[_dot(p, vw) for p in tile_p] for (_, _, vw, _, _), tile_p in zip(tiles, probs)]
    res = []
    for tile_o, tile_l in zip(outs, lses):
        out, lse = tile_o[3], tile_l[3]
        for h in (2, 1, 0):
            out = jnp.where(in_head[h], tile_o[h], out)
            lse = jnp.where(in_head[h], tile_l[h], lse)
        res.append((out, jnp.broadcast_to(lse, out.shape)))
    return res


def _dil_kernel(q_ref, k_ref, v_ref, o_ref, lse_ref, *, u_len, win):
    tiles = []
    for i in range(DIL_TILES):
        u0 = (pl.program_id(2) * DIL_TILES + i) * TU
        start = pl.multiple_of(jnp.clip(u0 - DIL_SIDE, 0, u_len - win), DIL_SIDE)
        qpos = u0 + lax.broadcasted_iota(I32, (TU, win), 0)
        kpos = start + lax.broadcasted_iota(I32, (TU, win), 1)
        tiles.append((q_ref[0, 0, i * TU:(i + 1) * TU, :], k_ref[0, 0, pl.ds(start, win), :],
                      v_ref[0, 0, pl.ds(start, win), :], jnp.abs(kpos - qpos) <= DIL_SIDE, None))
    for i, (out, lse_out) in enumerate(_local_heads(tiles)):
        o_ref[0, 0, i * TU:(i + 1) * TU, :] = out.astype(BF16)
        lse_ref[0, 0, i * TU:(i + 1) * TU, :] = lse_out


def _dil_attention(src, qu):
    b, dil, u_len, _ = src.shape
    win = TU + 2 * DIL_SIDE
    step = DIL_TILES * TU
    assert u_len >= win and u_len % step == 0
    return pl.pallas_call(
        functools.partial(_dil_kernel, u_len=u_len, win=win),
        grid=(b, dil, u_len // step),
        in_specs=[
            pl.BlockSpec((1, 1, step, 256), lambda bi, r, ui: (bi, r, ui, qu)),
            pl.BlockSpec((1, 1, u_len, 256), lambda bi, r, ui: (bi, r, 0, qu + 1)),
            pl.BlockSpec((1, 1, u_len, 256), lambda bi, r, ui: (bi, r, 0, qu + 2)),
        ],
        out_specs=[
            pl.BlockSpec((1, 1, step, 256), lambda bi, r, ui: (bi, r, ui, 0)),
            pl.BlockSpec((1, 1, step, 256), lambda bi, r, ui: (bi, r, ui, 0)),
        ],
        out_shape=[
            jax.ShapeDtypeStruct((b, dil, u_len, 256), BF16),
            jax.ShapeDtypeStruct((b, dil, u_len, 256), F32),
        ],
        compiler_params=_params(3),
        name=f"dil_attn_{dil}",
    )(src, src, src)


def _na_bias_kernel(tbl_ref, o_ref):
    h = pl.program_id(0)
    d = pl.program_id(1)
    n_rel_r, n_rel_c = 2 * NA_ROWS - 1, 2 * NA_COLS - 1
    dl = jnp.clip(d - NA_BIAS_OFF, 0, n_rel_r - 1)
    dr = jnp.clip(d - NA_BIAS_OFF + 1, 0, n_rel_r - 1)
    qc = lax.broadcasted_iota(I32, (GRID_W, LANES), 0)
    ln = lax.broadcasted_iota(I32, (GRID_W, LANES), 1)
    left = ln < GRID_W
    rel = jnp.where(left, ln, ln - GRID_W) - qc + (NA_COLS - 1)
    acc = jnp.zeros((GRID_W, LANES), F32)
    for dd in range(n_rel_c):
        vl = tbl_ref[(h * n_rel_r + dl) * n_rel_c + dd]
        vr = tbl_ref[(h * n_rel_r + dr) * n_rel_c + dd]
        acc = jnp.where(rel == dd, jnp.where(left, vl, vr), acc)
    o_ref[0, 0] = acc


def _na_bias_tiles(na_bias_l):
    n_h = na_bias_l.shape[0]
    return pl.pallas_call(
        _na_bias_kernel,
        grid=(n_h, NA_BIAS_N),
        in_specs=[pl.BlockSpec(memory_space=pltpu.SMEM)],
        out_specs=pl.BlockSpec((1, 1, GRID_W, LANES), lambda h, d: (h, d, 0, 0)),
        out_shape=jax.ShapeDtypeStruct((n_h, NA_BIAS_N, GRID_W, LANES), F32),
        compiler_params=_params(2),
        name="na_bias",
    )(na_bias_l.reshape(-1))


def _na_kernel(q_ref, k_ref, v_ref, bias_ref, o_ref, *, n_rows):
    n_keys = NA_WIN_ROWS * GRID_W
    qi = lax.broadcasted_iota(I32, (NA_TQ, n_keys), 0)
    kj = lax.broadcasted_iota(I32, (NA_TQ, n_keys), 1)
    tiles = []
    for i in range(LOCAL_TILES):
        r0 = (pl.program_id(1) * LOCAL_TILES + i) * 2
        ws = jnp.clip(r0 - NA_ROWS // 2, 0, n_rows - NA_WIN_ROWS)
        start = pl.multiple_of(ws * GRID_W, GRID_W)
        qr, qc = r0 + qi // GRID_W, qi % GRID_W
        kr, kc = ws + kj // GRID_W, kj % GRID_W
        rs = jnp.clip(qr - NA_ROWS // 2, 0, n_rows - NA_ROWS)
        cs = jnp.clip(qc - NA_COLS // 2, 0, GRID_W - NA_COLS)
        valid = (kr >= rs) & (kr < rs + NA_ROWS) & (kc >= cs) & (kc < cs + NA_COLS)
        biases = []
        for h in range(4):
            halves = []
            for half in range(2):
                d0 = ws - r0 - half + (NA_ROWS - 1) + NA_BIAS_OFF
                halves.append(jnp.concatenate(
                    [bias_ref[h, pl.ds(d0 + 2 * m, 1)][0] for m in range(NA_WIN_ROWS // 2)], axis=1))
            biases.append(jnp.concatenate(halves, axis=0))
        tiles.append((q_ref[0, i * NA_TQ:(i + 1) * NA_TQ, :], k_ref[0, pl.ds(start, n_keys), :],
                      v_ref[0, pl.ds(start, n_keys), :], valid, biases))
    for i, (out, _) in enumerate(_local_heads(tiles)):
        o_ref[0, i * NA_TQ:(i + 1) * NA_TQ, :] = out.astype(BF16)


def _na_attention(zr3, bias_tiles):
    b, s, _ = zr3.shape
    n_rows = s // GRID_W
    assert n_rows >= NA_WIN_ROWS
    qu, ku, vu = (Z_QD - Z_HEAD) // 256, (Z_KD - Z_HEAD) // 256, (Z_VD - Z_HEAD) // 256
    return pl.pallas_call(
        functools.partial(_na_kernel, n_rows=n_rows),
        grid=(b, s // (LOCAL_TILES * NA_TQ)),
        in_specs=[
            pl.BlockSpec((1, LOCAL_TILES * NA_TQ, 256), lambda bi, i: (bi, i, qu)),
            pl.BlockSpec((1, s, 256), lambda bi, i: (bi, 0, ku)),
            pl.BlockSpec((1, s, 256), lambda bi, i: (bi, 0, vu)),
            _const_spec(bias_tiles.shape),
        ],
        out_specs=pl.BlockSpec((1, LOCAL_TILES * NA_TQ, 256), lambda bi, i: (bi, i, 0)),
        out_shape=jax.ShapeDtypeStruct((b, s, 256), BF16),
        compiler_params=_params(2),
        name="na_attn",
    )(zr3, zr3, zr3, bias_tiles)


def _merge_kernel(x_ref, g_ref, oa_ref, ob0_ref, ob1_ref, ob2_ref, l0_ref, l1_ref, l2_ref, oc_ref, od_ref,
                  wg_ref, wb_ref, wo_ref, y_ref, scr_ref):
    x = x_ref[...]
    hb = _rms(x, g_ref[...]).astype(BF16)

    n_scr = [0]

    def token_order(ref):
        dil = ref.shape[1]
        if dil == 1:
            return ref[0, 0].astype(F32)
        slot = n_scr[0]
        n_scr[0] += 2
        for c in range(2):
            for r in range(dil):
                scr_ref[slot + c, pl.ds(r, TM // dil, stride=dil), :] = (
                    ref[0, r, :, c * LANES:(c + 1) * LANES].astype(F32))
        return jnp.concatenate([scr_ref[slot], scr_ref[slot + 1]], axis=1)

    lses = [token_order(r) for r in (l0_ref, l1_ref, l2_ref)]
    mx = jnp.maximum(jnp.maximum(lses[0], lses[1]), lses[2])
    ws = [jnp.exp(l - mx) for l in lses]
    den = ws[0] + ws[1] + ws[2]
    obs = [token_order(r) for r in (ob0_ref, ob1_ref, ob2_ref)]
    ob = (ws[0] / den) * obs[0]
    for g in range(1, 3):
        ob = ob + (ws[g] / den) * obs[g]
    branches = (oa_ref[...], ob.astype(BF16), oc_ref[...], od_ref[...])
    merged = None
    for i, o in enumerate(branches):
        gate = jax.nn.sigmoid(_dot(hb, wg_ref[i]))
        term = gate * _dot(o, wb_ref[i])
        merged = term if merged is None else merged + term
    y_ref[...] = x + _dot(merged.astype(BF16), wo_ref[...])


def _merge(x2d, s, g_mix, oa, obs, lses, oc, od, w_gate, w_branch, w_out):
    t = x2d.shape[0]
    n_s = s // TM
    tok = lambda w: pl.BlockSpec((TM, w), lambda i: (i, 0))
    res = lambda a: pl.BlockSpec((1, a.shape[1], TM // a.shape[1], 256), lambda i: (i // n_s, 0, i % n_s, 0))
    n_slots = 2 * sum(a.shape[1] > 1 for a in list(obs) + list(lses))
    return pl.pallas_call(
        _merge_kernel,
        grid=(t // TM,),
        in_specs=[tok(D_MODEL), _const_spec((1, D_MODEL)), tok(256)] + [res(a) for a in obs] + [res(a) for a in lses] + [
            tok(256), tok(256),
            _const_spec(w_gate.shape), _const_spec(w_branch.shape), _const_spec(w_out.shape)],
        out_specs=tok(D_MODEL),
        out_shape=jax.ShapeDtypeStruct((t, D_MODEL), F32),
        scratch_shapes=[pltpu.VMEM((n_slots, TM, LANES), F32)],
        compiler_params=_params(1),
        name="merge",
    )(x2d, g_mix, oa, obs[0], obs[1], obs[2], lses[0], lses[1], lses[2], oc, od, w_gate, w_branch, w_out)


def _memkv_kernel(m_ref, g_ref, wk_ref, wv_ref, k_ref, v_ref):
    mb = _rms(m_ref[0], g_ref[...]).astype(BF16)
    k_ref[0] = _dot(mb, wk_ref[...]).astype(BF16)
    v_ref[0] = _dot(mb, wv_ref[...]).astype(BF16)


def _mem_kv(mem, g_mem, w_ck, w_cv):
    b, m_len, d = mem.shape
    blk = pl.BlockSpec((1, m_len, d), lambda bi: (bi, 0, 0))
    return pl.pallas_call(
        _memkv_kernel,
        grid=(b,),
        in_specs=[blk, _const_spec((1, d)), _const_spec((d, d)), _const_spec((d, d))],
        out_specs=[blk, blk],
        out_shape=[jax.ShapeDtypeStruct((b, m_len, d), BF16)] * 2,
        compiler_params=_params(1),
        name="mem_kv",
    )(mem, g_mem, w_ck, w_cv)


def _cross_kernel(x_ref, g_ref, wq_ref, k_ref, v_ref, wo_ref, gf_ref, wr_ref, y_ref, h_ref, aff_ref):
    x = x_ref[...]
    hb = _rms(x, g_ref[...]).astype(BF16)
    q = (_dot(hb, wq_ref[...]) * MEM_SCALE).astype(BF16)
    heads = [slice(h * MEM_HEAD_DIM, (h + 1) * MEM_HEAD_DIM) for h in range(MEM_HEADS)]
    scores = [_dot_nt(q[:, sl], k_ref[0, :, sl]) for sl in heads]
    probs = []
    for s in scores:
        m = jnp.max(s, axis=-1, keepdims=True)
        e = jnp.exp(s - m)
        probs.append((e / jnp.sum(e, axis=-1, keepdims=True)).astype(BF16))
    outs = [_dot(p, v_ref[0, :, sl]).astype(BF16) for p, sl in zip(probs, heads)]
    y = x + _dot(jnp.concatenate(outs, axis=1), wo_ref[...])
    y_ref[...] = y
    h3 = _rms(y, gf_ref[...]).astype(BF16)
    h_ref[...] = h3
    logits = _dot_nt(wr_ref[...], h3)
    m = jnp.max(logits, axis=0, keepdims=True)
    e = jnp.exp(logits - m)
    aff_ref[...] = e / jnp.sum(e, axis=0, keepdims=True)


def _cross(x2d, b, s, g_cross, w_cq, k_mem, v_mem, w_co, g_ffn, w_router_t):
    t = b * s
    n_s = s // TM
    m_len = k_mem.shape[1]
    tok = lambda: pl.BlockSpec((TM, D_MODEL), lambda i: (i, 0))
    mem_blk = pl.BlockSpec((1, m_len, D_MODEL), lambda i: (i // n_s, 0, 0))
    return pl.pallas_call(
        _cross_kernel,
        grid=(t // TM,),
        in_specs=[tok(), _const_spec((1, D_MODEL)), _const_spec((D_MODEL, D_MODEL)), mem_blk, mem_blk,
                  _const_spec((D_MODEL, D_MODEL)), _const_spec((1, D_MODEL)),
                  _const_spec((N_EXPERTS, D_MODEL))],
        out_specs=[tok(), tok(), pl.BlockSpec((N_EXPERTS, TM), lambda i: (0, i))],
        out_shape=[jax.ShapeDtypeStruct((t, D_MODEL), F32), jax.ShapeDtypeStruct((t, D_MODEL), BF16),
                   jax.ShapeDtypeStruct((N_EXPERTS, t), F32)],
        compiler_params=_params(1),
        name="cross",
    )(x2d, g_cross, w_cq, k_mem, v_mem, w_co, g_ffn, w_router_t)


def _select_kernel(aff_ref, pos_ref, *, n_tok, cap):
    n_cnt = n_tok // CNT_CH

    def count_ge(cand):
        def body(c, acc):
            bits = pltpu.bitcast(aff_ref[:, pl.ds(pl.multiple_of(c * CNT_CH, CNT_CH), CNT_CH)], I32)
            return acc + jnp.sum(jnp.where(bits >= cand, 1.0, 0.0), axis=1, keepdims=True)
        return lax.fori_loop(0, n_cnt, body, jnp.zeros((N_EXPERTS, 1), F32))

    def bit_body(i, prefix):
        cand = prefix | lax.shift_left(jnp.int32(1), 30 - i)
        return jnp.where(count_ge(cand) >= cap, cand, prefix)

    thr = lax.fori_loop(0, 31, bit_body, jnp.zeros((N_EXPERTS, 1), I32))
    need_eq = cap - count_ge(thr + 1)

    tri_r = lax.broadcasted_iota(I32, (CH, CH), 0)
    tri_c = lax.broadcasted_iota(I32, (CH, CH), 1)
    tri = jnp.where(tri_r <= tri_c, 1.0, 0.0).astype(BF16)

    def chunk_body(c, carry):
        c_eq, c_row = carry
        sl = pl.ds(pl.multiple_of(c * CH, CH), CH)
        bits = pltpu.bitcast(aff_ref[:, sl], I32)
        eq = jnp.where(bits == thr, 1.0, 0.0)
        gt = jnp.where(bits > thr, 1.0, 0.0)
        eq_before = c_eq + _dot(eq.astype(BF16), tri) - eq
        sel = gt + eq * jnp.where(eq_before < need_eq, 1.0, 0.0)
        row = c_row + _dot(sel.astype(BF16), tri) - sel
        pos_ref[:, sl] = jnp.where(sel > 0.5, row, -1.0).astype(I32)
        taken = jnp.sum(sel, axis=1, keepdims=True)
        c_row = c_row + jnp.ceil(taken * (1.0 / ROW_ALIGN)) * ROW_ALIGN
        return (c_eq + jnp.sum(eq, axis=1, keepdims=True), c_row)

    zero = jnp.zeros((N_EXPERTS, 1), F32)
    lax.fori_loop(0, n_tok // CH, chunk_body, (zero, zero))


def _select(aff_t, cap):
    n_tok = aff_t.shape[1]
    return pl.pallas_call(
        functools.partial(_select_kernel, n_tok=n_tok, cap=cap),
        grid=(1,),
        in_specs=[_const_spec(aff_t.shape)],
        out_specs=_const_spec(aff_t.shape),
        out_shape=jax.ShapeDtypeStruct(aff_t.shape, I32),
        compiler_params=_params(1),
        name="select",
    )(aff_t)


def _window_hits(pos_ref, base_ref, e, k, n_k, p):
    rows = base_ref[e * n_k + k] + p * WIN + lax.broadcasted_iota(I32, (WIN, CH), 0)
    return pos_ref[e:e + 1, :] == rows


def _window_to_hbm(scr, hbm, sem, e, row0):
    return pltpu.make_async_copy(scr.at[pl.ds(e * WIN, WIN)], hbm.at[e, pl.ds(row0, WIN)], sem.at[e])


def _window_from_hbm(hbm, scr, sem, e, row0):
    return pltpu.make_async_copy(hbm.at[e, pl.ds(row0, WIN)], scr.at[pl.ds(e * WIN, WIN)], sem.at[e])


def _gather_kernel(base_ref, npass_ref, h_ref, pos_ref, aff_ref, xe_in, xe_ref, x_scr, sem_x, *, n_k):
    del xe_in
    k = pl.program_id(0)
    n_pass = npass_ref[k]
    xs = x_scr.at[k % 2]
    gate_lane = lax.broadcasted_iota(I32, (WIN, LANES), 1)

    def window_copies(kk, p):
        copies = []
        for e in range(N_EXPERTS):
            row0 = pl.multiple_of(base_ref[e * n_k + kk] + p * WIN, ROW_ALIGN)
            copies.append(_window_to_hbm(x_scr.at[kk % 2], xe_ref, sem_x, e, row0))
        return copies

    def one_pass(p, carry):
        for e0 in range(0, N_EXPERTS, GATHER_GROUP):
            hits = [_window_hits(pos_ref, base_ref, e, k, n_k, p) for e in range(e0, e0 + GATHER_GROUP)]
            onehot = jnp.concatenate([jnp.where(h, 1.0, 0.0).astype(BF16) for h in hits], axis=0)
            xs[e0 * WIN:(e0 + GATHER_GROUP) * WIN, 0:D_MODEL] = _dot(onehot, h_ref[...]).astype(BF16)
            for i, h in enumerate(hits):
                e = e0 + i
                gate = jnp.sum(jnp.where(h, aff_ref[e:e + 1, :], 0.0), axis=1, keepdims=True)
                hi = gate.astype(BF16).astype(F32)
                parts = jnp.where(gate_lane < LANES // 2, hi, gate - hi)
                xs[e * WIN:(e + 1) * WIN, D_MODEL:XE_COLS] = parts.astype(BF16)

        @pl.when((p == 0) & (k > 0))
        def _():
            for c in window_copies(k - 1, npass_ref[k - 1] - 1):
                c.wait()

        for c in window_copies(k, p):
            c.start()

        @pl.when(p < n_pass - 1)
        def _():
            for c in window_copies(k, p):
                c.wait()

        return carry

    lax.fori_loop(0, n_pass, one_pass, 0)

    @pl.when(k == n_k - 1)
    def _():
        for c in window_copies(k, n_pass - 1):
            c.wait()


def _expert_gather(base, npass, h3, pos, aff_t, rows_pad):
    n_tok = h3.shape[0]
    n_k = n_tok // CH
    xe0 = jnp.zeros((N_EXPERTS, rows_pad, XE_COLS), BF16)
    grid_spec = pltpu.PrefetchScalarGridSpec(
        num_scalar_prefetch=2,
        grid=(n_k,),
        in_specs=[
            pl.BlockSpec((CH, D_MODEL), lambda k, *_: (k, 0)),
            pl.BlockSpec((N_EXPERTS, CH), lambda k, *_: (0, k)),
            pl.BlockSpec((N_EXPERTS, CH), lambda k, *_: (0, k)),
            pl.BlockSpec(memory_space=pl.ANY),
        ],
        out_specs=pl.BlockSpec(memory_space=pl.ANY),
        scratch_shapes=[pltpu.VMEM((2, N_EXPERTS * WIN, XE_COLS), BF16), pltpu.SemaphoreType.DMA((N_EXPERTS,))],
    )
    return pl.pallas_call(
        functools.partial(_gather_kernel, n_k=n_k),
        grid_spec=grid_spec,
        out_shape=jax.ShapeDtypeStruct(xe0.shape, BF16),
        input_output_aliases={5: 0},
        compiler_params=_params(1),
        name="expert_gather",
    )(base, npass, h3, pos, aff_t, xe0)


def _ffn_kernel(end_ref, xe_ref, wg_ref, wu_ref, wd_ref, ye_ref):
    e, j = pl.program_id(0), pl.program_id(1)

    @pl.when(j * RB < end_ref[e])
    def _():
        xe = xe_ref[0, :, 0:D_MODEL]
        parts = xe_ref[0, :, D_MODEL:XE_COLS].astype(F32)
        gate = parts[:, 0:1] + parts[:, LANES // 2:LANES // 2 + 1]
        hid = jax.nn.silu(_dot(xe, wg_ref[0])) * _dot(xe, wu_ref[0])
        ye_ref[0] = (_dot(hid.astype(BF16), wd_ref[0]) * gate).astype(BF16)

    @pl.when(j * RB >= end_ref[e])
    def _():
        ye_ref[0] = jnp.zeros((RB, D_MODEL), BF16)


def _expert_ffn(end, xe, w_e_gate, w_e_up, w_e_down):
    rows_pad = xe.shape[1]
    grid_spec = pltpu.PrefetchScalarGridSpec(
        num_scalar_prefetch=1,
        grid=(N_EXPERTS, rows_pad // RB),
        in_specs=[
            pl.BlockSpec((1, RB, XE_COLS), lambda e, j, *_: (e, j, 0)),
            pl.BlockSpec((1, D_MODEL, EXPERT_FF), lambda e, j, *_: (e, 0, 0)),
            pl.BlockSpec((1, D_MODEL, EXPERT_FF), lambda e, j, *_: (e, 0, 0)),
            pl.BlockSpec((1, EXPERT_FF, D_MODEL), lambda e, j, *_: (e, 0, 0)),
        ],
        out_specs=pl.BlockSpec((1, RB, D_MODEL), lambda e, j, *_: (e, j, 0)),
    )
    return pl.pallas_call(
        _ffn_kernel,
        grid_spec=grid_spec,
        out_shape=jax.ShapeDtypeStruct(xe.shape[:2] + (D_MODEL,), BF16),
        compiler_params=_params(2),
        name="expert_ffn",
    )(end, xe, w_e_gate, w_e_up, w_e_down)


def _scatter_kernel(base_ref, npass_ref, x_ref, pos_ref, ye_ref, gfin_ref, y_ref, stage, sem,
                    *, final_norm, n_k):
    k = pl.program_id(0)
    slot = k % 2
    y_ref[...] = x_ref[...]

    def window_copies(kk, p, s):
        copies = []
        for e in range(N_EXPERTS):
            row0 = pl.multiple_of(base_ref[e * n_k + kk] + p * WIN, ROW_ALIGN)
            copies.append(_window_from_hbm(ye_ref, stage.at[s], sem.at[s], e, row0))
        return copies

    def onehot(p):
        return jnp.concatenate(
            [jnp.where(_window_hits(pos_ref, base_ref, e, k, n_k, p), 1.0, 0.0).astype(BF16)
             for e in range(N_EXPERTS)], axis=0)

    @pl.when(k == 0)
    def _():
        for c in window_copies(0, 0, 0):
            c.start()

    @pl.when(k + 1 < n_k)
    def _():
        for c in window_copies(k + 1, 0, 1 - slot):
            c.start()

    hot = onehot(0)
    for c in window_copies(k, 0, slot):
        c.wait()
    y_ref[...] += _dot_tn(hot, stage[slot])

    def later_pass(p, carry):
        copies = window_copies(k, p, slot)
        for c in copies:
            c.start()
        hot = onehot(p)
        for c in copies:
            c.wait()
        y_ref[...] += _dot_tn(hot, stage[slot])
        return carry

    lax.fori_loop(1, npass_ref[k], later_pass, 0)
    if final_norm:
        y_ref[...] = _rms(y_ref[...], gfin_ref[...])


def _expert_scatter(base, npass, x2d, pos, ye, g_final, final_norm):
    t = x2d.shape[0]
    n_k = t // CH
    grid_spec = pltpu.PrefetchScalarGridSpec(
        num_scalar_prefetch=2,
        grid=(n_k,),
        in_specs=[
            pl.BlockSpec((CH, D_MODEL), lambda k, *_: (k, 0)),
            pl.BlockSpec((N_EXPERTS, CH), lambda k, *_: (0, k)),
            pl.BlockSpec(memory_space=pl.ANY),
            pl.BlockSpec((1, D_MODEL), lambda k, *_: (0, 0)),
        ],
        out_specs=pl.BlockSpec((CH, D_MODEL), lambda k, *_: (k, 0)),
        scratch_shapes=[pltpu.VMEM((2, N_EXPERTS * WIN, D_MODEL), BF16), pltpu.SemaphoreType.DMA((2, N_EXPERTS))],
    )
    return pl.pallas_call(
        functools.partial(_scatter_kernel, final_norm=final_norm, n_k=n_k),
        grid_spec=grid_spec,
        out_shape=jax.ShapeDtypeStruct((t, D_MODEL), F32),
        compiler_params=_params(1),
        name="expert_scatter",
    )(base, npass, x2d, pos, ye, g_final)


def _segments(pos):
    n_e, n_tok = pos.shape
    n_k = n_tok // CH
    cnt = jnp.sum((pos >= 0).reshape(n_e, n_k, CH), axis=-1, dtype=I32)
    padded = (cnt + (ROW_ALIGN - 1)) // ROW_ALIGN * ROW_ALIGN
    end = jnp.cumsum(padded, axis=1)
    base = end - padded
    npass = jnp.maximum(jnp.max((cnt + (WIN - 1)) // WIN, axis=0), 1)
    return base.reshape(-1).astype(I32), npass.astype(I32), end[:, -1].astype(I32)


def _rows_pad(n_tok, cap):
    rows = cap + ROW_ALIGN * (n_tok // CH) + CH + WIN
    return (rows + RB - 1) // RB * RB


def _rope_tables(s):
    pos = jnp.arange(s, dtype=F32)[:, None]
    lane = jnp.arange(LANES)
    tabs = []
    for period, base, rot in ((LANES, MLA_NOPE, MLA_ROPE), (HEAD_DIM, 0, HEAD_DIM // 4), (DIFF_QK, 0, DIFF_QK // 4)):
        half = rot // 2
        rel = lane % period - base
        inv_freq = ROPE_THETA ** (-jnp.arange(half, dtype=F32) * (2.0 / rot))
        ang = pos * inv_freq[None, :]
        cos, sin = jnp.cos(ang), jnp.sin(ang)
        idx = jnp.clip(jnp.where(rel >= half, rel - half, rel), 0, half - 1)
        first = (rel >= 0) & (rel < half)
        second = (rel >= half) & (rel < rot)
        tabs.append(jnp.where(first | second, cos[:, idx], 1.0))
        tabs.append(jnp.where(first, -sin[:, idx], 0.0))
        tabs.append(jnp.where(second, sin[:, idx], 0.0))
    return jnp.stack(tabs).astype(F32)


def _prep_layer(l, w_in, g_mix, g_cq, w_uq, g_ckv, w_ukv, lam_q1, lam_k1, lam_q2, lam_k2, g_diff, na_bias,
                w_gate, w_branch, w_out, g_cross, g_mem, w_cq, w_ck, w_cv, w_co, g_ffn, w_router,
                w_e_gate, w_e_up, w_e_down):
    d = D_MODEL
    wi = w_in[l]
    kpe0 = MLA_Q_RANK + MLA_KV_RANK
    z64 = jnp.zeros((d, MLA_NOPE), F32)
    z32 = jnp.zeros((d, LANES - MLA_NOPE - MLA_ROPE), F32)
    rest = wi[:, kpe0 + MLA_ROPE:]
    n_g = len(DIL_GROUPS)
    n_dil = n_g * Z_DIL_GROUP
    vc0 = n_dil + 512
    dil_cols = rest[:, :n_dil].reshape(d, 3, n_g, 256).transpose(0, 2, 1, 3).reshape(d, n_dil)
    w_z = jnp.concatenate([wi[:, :kpe0], z64, wi[:, kpe0:kpe0 + MLA_ROPE], z32, dil_cols[:, :Z_DIL_GROUP],
                           rest[:, n_dil:vc0], rest[:, vc0 + 256:], dil_cols[:, Z_DIL_GROUP:]], axis=1)
    w_vct = rest[:, vc0:vc0 + 256].T
    qh = MLA_NOPE + MLA_ROPE
    wuq = w_uq[l].reshape(MLA_Q_RANK, MLA_HEADS, qh)
    wuq = jnp.pad(wuq, ((0, 0), (0, 0), (0, LANES - qh))).reshape(MLA_Q_RANK, MLA_HEADS * LANES)
    wukv = w_ukv[l].reshape(MLA_KV_RANK, MLA_HEADS, MLA_NOPE + HEAD_DIM)
    wk = jnp.pad(wukv[:, :, :MLA_NOPE], ((0, 0), (0, 0), (0, LANES - MLA_NOPE))).reshape(MLA_KV_RANK, MLA_HEADS * LANES)
    wv = wukv[:, :, MLA_NOPE:].reshape(MLA_KV_RANK, MLA_HEADS * HEAD_DIM)
    return dict(
        g_mix=g_mix[l][None], w_z=w_z.astype(BF16), g_cq=g_cq[l][None], g_ckv=g_ckv[l][None],
        w_uq=wuq.astype(BF16), w_k=wk.astype(BF16), w_vt=wv.T.astype(BF16), w_vct=w_vct.astype(BF16),
        lam=jnp.stack([lam_q1[l], lam_k1[l], lam_q2[l], lam_k2[l]]).astype(F32),
        g_diff=jnp.tile(g_diff[l], 2)[None], na_bias=na_bias[l],
        w_gate=w_gate[l].astype(BF16), w_branch=w_branch[l].astype(BF16), w_out=w_out[l].astype(BF16),
        g_cross=g_cross[l][None], g_mem=g_mem[l][None], w_cq=w_cq[l].astype(BF16), w_ck=w_ck[l].astype(BF16),
        w_cv=w_cv[l].astype(BF16), w_co=w_co[l].astype(BF16), g_ffn=g_ffn[l][None],
        w_router_t=w_router[l].T.astype(BF16), w_e_gate=w_e_gate[l].astype(BF16),
        w_e_up=w_e_up[l].astype(BF16), w_e_down=w_e_down[l].astype(BF16),
    )


def _trunk(x, mem, layers, bias_tiles, g_final):
    b, s, d = x.shape
    t = b * s
    cap = CAPACITY_FACTOR * t // N_EXPERTS
    assert s % TM == 0 and s % TQ == 0 and s % TK == 0 and t % CH == 0 and cap % RB == 0
    rope_tab = _rope_tables(s)
    x2d = x.reshape(t, d)
    for l, p in enumerate(layers):
        zr, qa, ka, vat, vct, d1, d2 = _project(x2d, b, s, p["g_mix"], p["w_z"], rope_tab, p["g_cq"], p["g_ckv"],
                                                p["w_uq"], p["w_k"], p["w_vt"], p["w_vct"])
        zr3 = zr.reshape(b, s, ZR_COLS)
        oa = _mla_attention(qa, ka, vat).reshape(t, 256)
        dil = [_dil_attention(zr.reshape(b, 1, s, ZR_COLS), (Z_DIL0 - Z_HEAD) // 256),
               _dil_attention(d1, 0), _dil_attention(d2, 0)]
        lam_init = 0.8 - 0.6 * math.exp(-0.3 * l)
        oc = _diff_attention(zr3, vct, p["lam"], p["g_diff"], lam_init).reshape(t, 256)
        od = _na_attention(zr3, bias_tiles[l]).reshape(t, 256)
        x2d = _merge(x2d, s, p["g_mix"], oa, [o for o, _ in dil], [ls for _, ls in dil], oc, od,
                     p["w_gate"], p["w_branch"], p["w_out"])
        k_mem, v_mem = _mem_kv(mem, p["g_mem"], p["w_ck"], p["w_cv"])
        x2d, h3, aff_t = _cross(x2d, b, s, p["g_cross"], p["w_cq"], k_mem, v_mem, p["w_co"], p["g_ffn"],
                                p["w_router_t"])
        pos = _select(aff_t, cap)
        base, npass, end = _segments(pos)
        xe = _expert_gather(base, npass, h3, pos, aff_t, _rows_pad(t, cap))
        ye = _expert_ffn(end, xe, p["w_e_gate"], p["w_e_up"], p["w_e_down"])
        x2d = _expert_scatter(base, npass, x2d, pos, ye, g_final[None], final_norm=(l == len(layers) - 1))
    return x2d.reshape(b, s, d)


def kernel(x_prompt, x_sample, mem_prompt, mem_sample, w_in, g_mix, g_cq, w_uq, g_ckv, w_ukv, lam_q1, lam_k1, lam_q2, lam_k2, g_diff, na_bias, w_gate, w_branch, w_out, g_cross, g_mem, w_cq, w_ck, w_cv, w_co, g_ffn, w_router, w_e_gate, w_e_up, w_e_down, g_final):
    depth = w_in.shape[0]
    layers = [_prep_layer(l, w_in, g_mix, g_cq, w_uq, g_ckv, w_ukv, lam_q1, lam_k1, lam_q2, lam_k2, g_diff,
                          na_bias, w_gate, w_branch, w_out, g_cross, g_mem, w_cq, w_ck, w_cv, w_co, g_ffn,
                          w_router, w_e_gate, w_e_up, w_e_down) for l in range(depth)]
    bias_tiles = [_na_bias_tiles(p["na_bias"]) for p in layers]
    y_prompt = _trunk(x_prompt, mem_prompt, layers, bias_tiles, g_final)
    y_sample = _trunk(x_sample, mem_sample, layers, bias_tiles, g_final)
    return (y_prompt, y_sample)
```

```python
import functools
import math

import jax
import jax.numpy as jnp
from jax import lax
from jax.experimental import pallas as pl
from jax.experimental.pallas import tpu as pltpu

BF16 = jnp.bfloat16
F32 = jnp.float32
I32 = jnp.int32

D_MODEL = 1024
EPS = 1e-6
NEG = -1e30
ROPE_THETA = 500000.0
HEAD_DIM = 64
MLA_HEADS = 4
MLA_Q_RANK = 256
MLA_KV_RANK = 128
MLA_NOPE = 64
MLA_ROPE = 32
MLA_SCALE = (MLA_NOPE + MLA_ROPE) ** -0.5
DIL_GROUPS = ((128, 1), (512, 4), (2048, 16))
DIL_SIDE = 64
DIL_SCALE = HEAD_DIM ** -0.5
DIFF_QK = 32
DIFF_SCALE = DIFF_QK ** -0.5
NA_ROWS = 8
NA_COLS = 16
GRID_W = 64
NA_SCALE = HEAD_DIM ** -0.5
MEM_HEADS = 4
MEM_HEAD_DIM = D_MODEL // MEM_HEADS
MEM_SCALE = MEM_HEAD_DIM ** -0.5
N_EXPERTS = 16
EXPERT_FF = 512
CAPACITY_FACTOR = 2

LANES = 128
VMEM_LIMIT = 56 * 1024 * 1024

Z_CQ = 0
Z_CKV = 256
Z_KPE = 384
Z_HEAD = 512
Z_DIL0 = 512
Z_DIL_GROUP = 768
Z_QC = 1280
Z_KC = 1536
Z_QD = 1792
Z_KD = 2048
Z_VD = 2304
Z_DILX = 2560
Z_COLS = Z_DILX + 2 * Z_DIL_GROUP
ZR_COLS = Z_DILX - Z_HEAD
LOG2E = math.log2(math.e)

TM = 512
TQ = 256
TK = 512
MLA_UNROLL = 8
DIFF_UNROLL = 2
TU = 128
DIL_TILES = 2
LOCAL_TILES = 2
NA_TQ = 128
NA_WIN_ROWS = 10
NA_BIAS_OFF = 2
NA_BIAS_N = 18
RB = 512
CH = 512
WIN = 128
XE_COLS = D_MODEL + LANES
ROW_ALIGN = 16
GATHER_GROUP = 4
CNT_CH = 4096


def _params(n_grid):
    return pltpu.CompilerParams(dimension_semantics=("arbitrary",) * n_grid,
                                vmem_limit_bytes=VMEM_LIMIT)


def _rms(x, g):
    return x * lax.rsqrt(jnp.mean(x * x, axis=-1, keepdims=True) + EPS) * g


def _dot(a, b):
    return jnp.dot(a, b, preferred_element_type=F32)


def _dot_nt(a, b):
    return lax.dot_general(a, b, (((1,), (1,)), ((), ())), preferred_element_type=F32)


def _dot_tn(a, b):
    return lax.dot_general(a, b, (((0,), (0,)), ((), ())), preferred_element_type=F32)


def _rope_block(x, c, sa, sb, half):
    return x * c + pltpu.roll(x, LANES - half, 1) * sa + pltpu.roll(x, half, 1) * sb


def _proj_kernel(x_ref, g_ref, w_ref, rope_ref, gcq_ref, gckv_ref, wuq_ref, wk_ref, wvt_ref, wvct_ref,
                 zr_ref, qa_ref, ka_ref, vat_ref, vct_ref, d1_ref, d2_ref, scr_ref):
    hb = _rms(x_ref[...], g_ref[...]).astype(BF16)

    def rope(x, kind, half):
        return _rope_block(x, rope_ref[3 * kind], rope_ref[3 * kind + 1], rope_ref[3 * kind + 2], half)

    z0 = _dot(hb, w_ref[:, 0:Z_HEAD])
    cq = _rms(z0[:, Z_CQ:Z_CQ + MLA_Q_RANK], gcq_ref[...]).astype(BF16)
    ckv = _rms(z0[:, Z_CKV:Z_CKV + MLA_KV_RANK], gckv_ref[...]).astype(BF16)
    kpe = rope(z0[:, Z_KPE:Z_KPE + LANES], 0, MLA_ROPE // 2)
    qa = _dot(cq, wuq_ref[...]) * (MLA_SCALE * LOG2E)
    kn = _dot(ckv, wk_ref[...])
    for h in range(MLA_HEADS):
        sl = slice(h * LANES, (h + 1) * LANES)
        qa_ref[0, h] = rope(qa[:, sl], 0, MLA_ROPE // 2).astype(BF16)
        ka_ref[0, h] = (kn[:, sl] + kpe).astype(BF16)
    vat_ref[0] = _dot_nt(wvt_ref[...], ckv).astype(BF16)
    vct_ref[0] = _dot_nt(wvct_ref[...], hb).astype(BF16)

    n_scr = [0]

    def seg(start, scale, kind, half, dil_out=None):
        z = _dot(hb, w_ref[:, start:start + 256])
        if scale is not None:
            z = z * scale
        for j in range(2):
            zb = z[:, j * LANES:(j + 1) * LANES]
            if kind is not None:
                zb = rope(zb, kind, half)
            if dil_out is None:
                o = start - Z_HEAD + j * LANES
                zr_ref[:, o:o + LANES] = zb.astype(BF16)
            else:
                out_ref, col, dil = dil_out
                slot = n_scr[0] % scr_ref.shape[0]
                n_scr[0] += 1
                scr_ref[slot] = zb
                for r in range(dil):
                    out_ref[0, r, :, col + j * LANES:col + (j + 1) * LANES] = (
                        scr_ref[slot, pl.ds(r, TM // dil, stride=dil), :].astype(BF16))

    dil_half = HEAD_DIM // 4 // 2
    diff_half = DIFF_QK // 4 // 2
    seg(Z_DIL0, DIL_SCALE, 1, dil_half)
    seg(Z_DIL0 + 256, None, 1, dil_half)
    seg(Z_DIL0 + 512, None, None, 0)
    for g, out_ref in ((1, d1_ref), (2, d2_ref)):
        base = Z_DILX + (g - 1) * Z_DIL_GROUP
        dil = DIL_GROUPS[g][1]
        seg(base, DIL_SCALE, 1, dil_half, (out_ref, 0, dil))
        seg(base + 256, None, 1, dil_half, (out_ref, 256, dil))
        seg(base + 512, None, None, 0, (out_ref, 512, dil))
    seg(Z_QC, DIFF_SCALE * LOG2E, 2, diff_half)
    seg(Z_KC, None, 2, diff_half)
    seg(Z_QD, NA_SCALE, None, 0)
    seg(Z_KD, None, None, 0)
    seg(Z_VD, None, None, 0)


def _const_spec(shape):
    nd = len(shape)
    return pl.BlockSpec(shape, lambda *_: (0,) * nd)


def _project(x2d, b, s, g_mix, w_z, rope_tab, g_cq, g_ckv, w_uq, w_k, w_vt, w_vct):
    t = b * s
    n_s = s // TM
    v_cols = MLA_HEADS * HEAD_DIM
    dils = [dl for _, dl in DIL_GROUPS[1:]]
    return pl.pallas_call(
        _proj_kernel,
        grid=(t // TM,),
        in_specs=[
            pl.BlockSpec((TM, D_MODEL), lambda i: (i, 0)),
            _const_spec((1, D_MODEL)),
            _const_spec((D_MODEL, Z_COLS)),
            pl.BlockSpec((9, TM, LANES), lambda i: (0, i % n_s, 0)),
            _const_spec((1, MLA_Q_RANK)),
            _const_spec((1, MLA_KV_RANK)),
            _const_spec((MLA_Q_RANK, MLA_HEADS * LANES)),
            _const_spec((MLA_KV_RANK, MLA_HEADS * LANES)),
            _const_spec((v_cols, MLA_KV_RANK)),
            _const_spec((v_cols, D_MODEL)),
        ],
        out_specs=[
            pl.BlockSpec((TM, ZR_COLS), lambda i: (i, 0)),
            pl.BlockSpec((1, MLA_HEADS, TM, LANES), lambda i: (i // n_s, 0, i % n_s, 0)),
            pl.BlockSpec((1, MLA_HEADS, TM, LANES), lambda i: (i // n_s, 0, i % n_s, 0)),
            pl.BlockSpec((1, v_cols, TM), lambda i: (i // n_s, 0, i % n_s)),
            pl.BlockSpec((1, v_cols, TM), lambda i: (i // n_s, 0, i % n_s)),
        ] + [pl.BlockSpec((1, dl, TM // dl, Z_DIL_GROUP), lambda i: (i // n_s, 0, i % n_s, 0)) for dl in dils],
        out_shape=[
            jax.ShapeDtypeStruct((t, ZR_COLS), BF16),
            jax.ShapeDtypeStruct((b, MLA_HEADS, s, LANES), BF16),
            jax.ShapeDtypeStruct((b, MLA_HEADS, s, LANES), BF16),
            jax.ShapeDtypeStruct((b, v_cols, s), BF16),
            jax.ShapeDtypeStruct((b, v_cols, s), BF16),
        ] + [jax.ShapeDtypeStruct((b, dl, s // dl, Z_DIL_GROUP), BF16) for dl in dils],
        scratch_shapes=[pltpu.VMEM((4, TM, LANES), F32)],
        compiler_params=_params(1),
        name="proj",
    )(x2d, g_mix, w_z, rope_tab, g_cq, g_ckv, w_uq, w_k, w_vt, w_vct)


def _chunk(c):
    return pl.ds(pl.multiple_of(c * TK, TK), TK)


def _dense_attention(qs, k_at, vt_at, s_ref, p_ref, n_chunks, unroll):
    n = len(qs)

    def step(c, cur, nxt, carry):
        rows_next = _chunk(jnp.minimum(c + 1, n_chunks - 1))
        rows_prev = _chunk(jnp.maximum(c - 1, 0))
        new = []
        for i in range(n):
            m, l, acc, alpha_prev = carry[i]
            s_ref[i, nxt] = _dot_nt(k_at(i, rows_next), qs[i])
            acc = alpha_prev * acc + _dot(vt_at(i, rows_prev), p_ref[i, nxt])
            st = s_ref[i, cur]
            m_new = jnp.maximum(m, jnp.max(st, axis=0, keepdims=True))
            alpha = jnp.exp2(m - m_new)
            p = jnp.exp2(st - m_new)
            l = alpha * l + jnp.sum(p, axis=0, keepdims=True)
            p_ref[i, cur] = p.astype(BF16)
            new.append((m_new, l, acc, alpha))
        return tuple(new)

    def body(j, carry):
        for u in range(unroll):
            carry = step(unroll * j + u, u % 2, (u + 1) % 2, carry)
        return carry

    for i in range(n):
        s_ref[i, 0] = _dot_nt(k_at(i, pl.ds(0, TK)), qs[i])
        p_ref[i, 1] = jnp.zeros((TK, TQ), BF16)
    init = tuple((jnp.full((1, TQ), NEG, F32), jnp.zeros((1, TQ), F32), jnp.zeros((HEAD_DIM, TQ), F32),
                  jnp.ones((1, TQ), F32)) for _ in range(n))
    assert unroll % 2 == 0 and n_chunks % unroll == 0
    res = lax.fori_loop(0, n_chunks // unroll, body, init)
    out = []
    for i in range(n):
        _, l, acc, alpha = res[i]
        acc = alpha * acc + _dot(vt_at(i, pl.ds((n_chunks - 1) * TK, TK)), p_ref[i, 1])
        out.append((acc, l))
    return out


def _mla_kernel(q_ref, k_ref, vt_ref, o_ref, s_ref, p_ref, *, n_chunks):
    res = _dense_attention(
        [q_ref[0, hh] for hh in range(2)],
        lambda i, rows: k_ref[0, i, rows, :],
        lambda i, rows: vt_ref[0, i * HEAD_DIM:(i + 1) * HEAD_DIM, rows],
        s_ref, p_ref, n_chunks, MLA_UNROLL)
    out_t = jnp.concatenate([acc / l for acc, l in res], axis=0)
    o_ref[0] = out_t.T.astype(BF16)


def _mla_attention(qa, ka, vat):
    b, _, s, _ = qa.shape
    return pl.pallas_call(
        functools.partial(_mla_kernel, n_chunks=s // TK),
        grid=(b, 2, s // TQ),
        in_specs=[
            pl.BlockSpec((1, 2, TQ, LANES), lambda bi, p, qi: (bi, p, qi, 0)),
            pl.BlockSpec((1, 2, s, LANES), lambda bi, p, qi: (bi, p, 0, 0)),
            pl.BlockSpec((1, LANES, s), lambda bi, p, qi: (bi, p, 0)),
        ],
        out_specs=pl.BlockSpec((1, TQ, LANES), lambda bi, p, qi: (bi, qi, p)),
        out_shape=jax.ShapeDtypeStruct((b, s, 2 * LANES), BF16),
        scratch_shapes=[pltpu.VMEM((2, 2, TK, TQ), F32), pltpu.VMEM((2, 2, TK, TQ), BF16)],
        compiler_params=_params(3),
        name="mla_attn",
    )(qa, ka, vat)


def _diff_kernel(q_ref, k_ref, vt_ref, lam_ref, g_ref, o_ref, s_ref, p_ref, *, n_chunks, lam_init):
    lane = lax.broadcasted_iota(I32, (TQ, LANES), 1)
    lv = lam_ref[...]
    lam = (jnp.exp(jnp.sum(lv[0:1] * lv[1:2], axis=-1, keepdims=True))
           - jnp.exp(jnp.sum(lv[2:3] * lv[3:4], axis=-1, keepdims=True)) + lam_init)
    q = q_ref[0]
    qs = [jnp.where((lane >= i * DIFF_QK) & (lane < (i + 1) * DIFF_QK), q, jnp.zeros_like(q)) for i in range(4)]
    res = _dense_attention(
        qs,
        lambda i, rows: k_ref[0, rows, :],
        lambda i, rows: vt_ref[0, (i // 2) * HEAD_DIM:(i // 2 + 1) * HEAD_DIM, rows],
        s_ref, p_ref, n_chunks, DIFF_UNROLL)
    parts = [acc / l for acc, l in res]
    out_t = jnp.concatenate([parts[2 * hh] - lam * parts[2 * hh + 1] for hh in range(2)], axis=0)
    o = out_t.T
    head0 = lane < HEAD_DIM
    sq = o * o
    ms0 = jnp.sum(jnp.where(head0, sq, 0.0), axis=-1, keepdims=True)
    ms1 = jnp.sum(jnp.where(head0, 0.0, sq), axis=-1, keepdims=True)
    ms = jnp.where(head0, ms0, ms1) * (1.0 / HEAD_DIM)
    o_ref[0] = (o * lax.rsqrt(ms + EPS) * g_ref[...] * (1.0 - lam_init)).astype(BF16)


def _diff_attention(zr3, vct, lam_vecs, g_diff2, lam_init):
    b, s, _ = zr3.shape
    qb, kb = (Z_QC - Z_HEAD) // LANES, (Z_KC - Z_HEAD) // LANES
    return pl.pallas_call(
        functools.partial(_diff_kernel, n_chunks=s // TK, lam_init=lam_init),
        grid=(b, 2, s // TQ),
        in_specs=[
            pl.BlockSpec((1, TQ, LANES), lambda bi, p, qi: (bi, qi, qb + p)),
            pl.BlockSpec((1, s, LANES), lambda bi, p, qi: (bi, 0, kb + p)),
            pl.BlockSpec((1, LANES, s), lambda bi, p, qi: (bi, p, 0)),
            _const_spec((4, DIFF_QK)),
            _const_spec((1, LANES)),
        ],
        out_specs=pl.BlockSpec((1, TQ, LANES), lambda bi, p, qi: (bi, qi, p)),
        out_shape=jax.ShapeDtypeStruct((b, s, 2 * LANES), BF16),
        scratch_shapes=[pltpu.VMEM((4, 2, TK, TQ), F32), pltpu.VMEM((4, 2, TK, TQ), BF16)],
        compiler_params=_params(3),
        name="diff_attn",
    )(zr3, zr3, vct, lam_vecs, g_diff2)


def _local_heads(tiles):
    n = tiles[0][0].shape[0]
    lane = lax.broadcasted_iota(I32, (n, 4 * HEAD_DIM), 1)
    in_head = [(lane >= h * HEAD_DIM) & (lane < (h + 1) * HEAD_DIM) for h in range(4)]
    scores = [[_dot_nt(jnp.where(in_head[h], q, jnp.zeros_like(q)), kw) for h in range(4)]
              for q, kw, _, _, _ in tiles]
    probs, lses = [], []
    for (_, _, _, valid, biases), sc in zip(tiles, scores):
        tile_p, tile_l = [], []
        for h in range(4):
            s = sc[h] if biases is None else sc[h] + biases[h]
            s = jnp.where(valid, s, NEG)
            m = jnp.max(s, axis=-1, keepdims=True)
            e = jnp.exp(s - m)
            l = jnp.sum(e, axis=-1, keepdims=True)
            tile_p.append((e / l).astype(BF16))
            tile_l.append(m + jnp.log(l))
        probs.append(tile_p)
        lses.append(tile_l)
    outs = [[_dot(p, vw) for p in tile_p] for (_, _, vw, _, _), tile_p in zip(tiles, probs)]
    res = []
    for tile_o, tile_l in zip(outs, lses):
        out, lse = tile_o[3], tile_l[3]
        for h in (2, 1, 0):
            out = jnp.where(in_head[h], tile_o[h], out)
            lse = jnp.where(in_head[h], tile_l[h], lse)
        res.append((out, jnp.broadcast_to(lse, out.shape)))
    return res


def _dil_kernel(q_ref, k_ref, v_ref, o_ref, lse_ref, *, u_len, win):
    tiles = []
    for i in range(DIL_TILES):
        u0 = (pl.program_id(2) * DIL_TILES + i) * TU
        start = pl.multiple_of(jnp.clip(u0 - DIL_SIDE, 0, u_len - win), DIL_SIDE)
        qpos = u0 + lax.broadcasted_iota(I32, (TU, win), 0)
        kpos = start + lax.broadcasted_iota(I32, (TU, win), 1)
        tiles.append((q_ref[0, 0, i * TU:(i + 1) * TU, :], k_ref[0, 0, pl.ds(start, win), :],
                      v_ref[0, 0, pl.ds(start, win), :], jnp.abs(kpos - qpos) <= DIL_SIDE, None))
    for i, (out, lse_out) in enumerate(_local_heads(tiles)):
        o_ref[0, 0, i * TU:(i + 1) * TU, :] = out.astype(BF16)
        lse_ref[0, 0, i * TU:(i + 1) * TU, :] = lse_out


def _dil_attention(src, qu):
    b, dil, u_len, _ = src.shape
    win = TU + 2 * DIL_SIDE
    step = DIL_TILES * TU
    assert u_len >= win and u_len % step == 0
    return pl.pallas_call(
        functools.partial(_dil_kernel, u_len=u_len, win=win),
        grid=(b, dil, u_len // step),
        in_specs=[
            pl.BlockSpec((1, 1, step, 256), lambda bi, r, ui: (bi, r, ui, qu)),
            pl.BlockSpec((1, 1, u_len, 256), lambda bi, r, ui: (bi, r, 0, qu + 1)),
            pl.BlockSpec((1, 1, u_len, 256), lambda bi, r, ui: (bi, r, 0, qu + 2)),
        ],
        out_specs=[
            pl.BlockSpec((1, 1, step, 256), lambda bi, r, ui: (bi, r, ui, 0)),
            pl.BlockSpec((1, 1, step, 256), lambda bi, r, ui: (bi, r, ui, 0)),
        ],
        out_shape=[
            jax.ShapeDtypeStruct((b, dil, u_len, 256), BF16),
            jax.ShapeDtypeStruct((b, dil, u_len, 256), F32),
        ],
        compiler_params=_params(3),
        name=f"dil_attn_{dil}",
    )(src, src, src)


def _na_bias_kernel(tbl_ref, o_ref):
    h = pl.program_id(0)
    d = pl.program_id(1)
    n_rel_r, n_rel_c = 2 * NA_ROWS - 1, 2 * NA_COLS - 1
    dl = jnp.clip(d - NA_BIAS_OFF, 0, n_rel_r - 1)
    dr = jnp.clip(d - NA_BIAS_OFF + 1, 0, n_rel_r - 1)
    qc = lax.broadcasted_iota(I32, (GRID_W, LANES), 0)
    ln = lax.broadcasted_iota(I32, (GRID_W, LANES), 1)
    left = ln < GRID_W
    rel = jnp.where(left, ln, ln - GRID_W) - qc + (NA_COLS - 1)
    acc = jnp.zeros((GRID_W, LANES), F32)
    for dd in range(n_rel_c):
        vl = tbl_ref[(h * n_rel_r + dl) * n_rel_c + dd]
        vr = tbl_ref[(h * n_rel_r + dr) * n_rel_c + dd]
        acc = jnp.where(rel == dd, jnp.where(left, vl, vr), acc)
    o_ref[0, 0] = acc


def _na_bias_tiles(na_bias_l):
    n_h = na_bias_l.shape[0]
    return pl.pallas_call(
        _na_bias_kernel,
        grid=(n_h, NA_BIAS_N),
        in_specs=[pl.BlockSpec(memory_space=pltpu.SMEM)],
        out_specs=pl.BlockSpec((1, 1, GRID_W, LANES), lambda h, d: (h, d, 0, 0)),
        out_shape=jax.ShapeDtypeStruct((n_h, NA_BIAS_N, GRID_W, LANES), F32),
        compiler_params=_params(2),
        name="na_bias",
    )(na_bias_l.reshape(-1))


def _na_kernel(q_ref, k_ref, v_ref, bias_ref, o_ref, *, n_rows):
    n_keys = NA_WIN_ROWS * GRID_W
    qi = lax.broadcasted_iota(I32, (NA_TQ, n_keys), 0)
    kj = lax.broadcasted_iota(I32, (NA_TQ, n_keys), 1)
    tiles = []
    for i in range(LOCAL_TILES):
        r0 = (pl.program_id(1) * LOCAL_TILES + i) * 2
        ws = jnp.clip(r0 - NA_ROWS // 2, 0, n_rows - NA_WIN_ROWS)
        start = pl.multiple_of(ws * GRID_W, GRID_W)
        qr, qc = r0 + qi // GRID_W, qi % GRID_W
        kr, kc = ws + kj // GRID_W, kj % GRID_W
        rs = jnp.clip(qr - NA_ROWS // 2, 0, n_rows - NA_ROWS)
        cs = jnp.clip(qc - NA_COLS // 2, 0, GRID_W - NA_COLS)
        valid = (kr >= rs) & (kr < rs + NA_ROWS) & (kc >= cs) & (kc < cs + NA_COLS)
        biases = []
        for h in range(4):
            halves = []
            for half in range(2):
                d0 = ws - r0 - half + (NA_ROWS - 1) + NA_BIAS_OFF
                halves.append(jnp.concatenate(
                    [bias_ref[h, pl.ds(d0 + 2 * m, 1)][0] for m in range(NA_WIN_ROWS // 2)], axis=1))
            biases.append(jnp.concatenate(halves, axis=0))
        tiles.append((q_ref[0, i * NA_TQ:(i + 1) * NA_TQ, :], k_ref[0, pl.ds(start, n_keys), :],
                      v_ref[0, pl.ds(start, n_keys), :], valid, biases))
    for i, (out, _) in enumerate(_local_heads(tiles)):
        o_ref[0, i * NA_TQ:(i + 1) * NA_TQ, :] = out.astype(BF16)


def _na_attention(zr3, bias_tiles):
    b, s, _ = zr3.shape
    n_rows = s // GRID_W
    assert n_rows >= NA_WIN_ROWS
    qu, ku, vu = (Z_QD - Z_HEAD) // 256, (Z_KD - Z_HEAD) // 256, (Z_VD - Z_HEAD) // 256
    return pl.pallas_call(
        functools.partial(_na_kernel, n_rows=n_rows),
        grid=(b, s // (LOCAL_TILES * NA_TQ)),
        in_specs=[
            pl.BlockSpec((1, LOCAL_TILES * NA_TQ, 256), lambda bi, i: (bi, i, qu)),
            pl.BlockSpec((1, s, 256), lambda bi, i: (bi, 0, ku)),
            pl.BlockSpec((1, s, 256), lambda bi, i: (bi, 0, vu)),
            _const_spec(bias_tiles.shape),
        ],
        out_specs=pl.BlockSpec((1, LOCAL_TILES * NA_TQ, 256), lambda bi, i: (bi, i, 0)),
        out_shape=jax.ShapeDtypeStruct((b, s, 256), BF16),
        compiler_params=_params(2),
        name="na_attn",
    )(zr3, zr3, zr3, bias_tiles)


def _merge_kernel(x_ref, g_ref, oa_ref, ob0_ref, ob1_ref, ob2_ref, l0_ref, l1_ref, l2_ref, oc_ref, od_ref,
                  wg_ref, wb_ref, wo_ref, y_ref, scr_ref):
    x = x_ref[...]
    hb = _rms(x, g_ref[...]).astype(BF16)

    n_scr = [0]

    def token_order(ref):
        dil = ref.shape[1]
        if dil == 1:
            return ref[0, 0].astype(F32)
        slot = n_scr[0]
        n_scr[0] += 2
        for c in range(2):
            for r in range(dil):
                scr_ref[slot + c, pl.ds(r, TM // dil, stride=dil), :] = (
                    ref[0, r, :, c * LANES:(c + 1) * LANES].astype(F32))
        return jnp.concatenate([scr_ref[slot], scr_ref[slot + 1]], axis=1)

    lses = [token_order(r) for r in (l0_ref, l1_ref, l2_ref)]
    mx = jnp.maximum(jnp.maximum(lses[0], lses[1]), lses[2])
    ws = [jnp.exp(l - mx) for l in lses]
    den = ws[0] + ws[1] + ws[2]
    obs = [token_order(r) for r in (ob0_ref, ob1_ref, ob2_ref)]
    ob = (ws[0] / den) * obs[0]
    for g in range(1, 3):
        ob = ob + (ws[g] / den) * obs[g]
    branches = (oa_ref[...], ob.astype(BF16), oc_ref[...], od_ref[...])
    merged = None
    for i, o in enumerate(branches):
        gate = jax.nn.sigmoid(_dot(hb, wg_ref[i]))
        term = gate * _dot(o, wb_ref[i])
        merged = term if merged is None else merged + term
    y_ref[...] = x + _dot(merged.astype(BF16), wo_ref[...])


def _merge(x2d, s, g_mix, oa, obs, lses, oc, od, w_gate, w_branch, w_out):
    t = x2d.shape[0]
    n_s = s // TM
    tok = lambda w: pl.BlockSpec((TM, w), lambda i: (i, 0))
    res = lambda a: pl.BlockSpec((1, a.shape[1], TM // a.shape[1], 256), lambda i: (i // n_s, 0, i % n_s, 0))
    n_slots = 2 * sum(a.shape[1] > 1 for a in list(obs) + list(lses))
    return pl.pallas_call(
        _merge_kernel,
        grid=(t // TM,),
        in_specs=[tok(D_MODEL), _const_spec((1, D_MODEL)), tok(256)] + [res(a) for a in obs] + [res(a) for a in lses] + [
            tok(256), tok(256),
            _const_spec(w_gate.shape), _const_spec(w_branch.shape), _const_spec(w_out.shape)],
        out_specs=tok(D_MODEL),
        out_shape=jax.ShapeDtypeStruct((t, D_MODEL), F32),
        scratch_shapes=[pltpu.VMEM((n_slots, TM, LANES), F32)],
        compiler_params=_params(1),
        name="merge",
    )(x2d, g_mix, oa, obs[0], obs[1], obs[2], lses[0], lses[1], lses[2], oc, od, w_gate, w_branch, w_out)


def _memkv_kernel(m_ref, g_ref, wk_ref, wv_ref, k_ref, v_ref):
    mb = _rms(m_ref[0], g_ref[...]).astype(BF16)
    k_ref[0] = _dot(mb, wk_ref[...]).astype(BF16)
    v_ref[0] = _dot(mb, wv_ref[...]).astype(BF16)


def _mem_kv(mem, g_mem, w_ck, w_cv):
    b, m_len, d = mem.shape
    blk = pl.BlockSpec((1, m_len, d), lambda bi: (bi, 0, 0))
    return pl.pallas_call(
        _memkv_kernel,
        grid=(b,),
        in_specs=[blk, _const_spec((1, d)), _const_spec((d, d)), _const_spec((d, d))],
        out_specs=[blk, blk],
        out_shape=[jax.ShapeDtypeStruct((b, m_len, d), BF16)] * 2,
        compiler_params=_params(1),
        name="mem_kv",
    )(mem, g_mem, w_ck, w_cv)


def _cross_kernel(x_ref, g_ref, wq_ref, k_ref, v_ref, wo_ref, gf_ref, wr_ref, y_ref, h_ref, aff_ref):
    x = x_ref[...]
    hb = _rms(x, g_ref[...]).astype(BF16)
    q = (_dot(hb, wq_ref[...]) * MEM_SCALE).astype(BF16)
    heads = [slice(h * MEM_HEAD_DIM, (h + 1) * MEM_HEAD_DIM) for h in range(MEM_HEADS)]
    scores = [_dot_nt(q[:, sl], k_ref[0, :, sl]) for sl in heads]
    probs = []
    for s in scores:
        m = jnp.max(s, axis=-1, keepdims=True)
        e = jnp.exp(s - m)
        probs.append((e / jnp.sum(e, axis=-1, keepdims=True)).astype(BF16))
    outs = [_dot(p, v_ref[0, :, sl]).astype(BF16) for p, sl in zip(probs, heads)]
    y = x + _dot(jnp.concatenate(outs, axis=1), wo_ref[...])
    y_ref[...] = y
    h3 = _rms(y, gf_ref[...]).astype(BF16)
    h_ref[...] = h3
    logits = _dot_nt(wr_ref[...], h3)
    m = jnp.max(logits, axis=0, keepdims=True)
    e = jnp.exp(logits - m)
    aff_ref[...] = e / jnp.sum(e, axis=0, keepdims=True)


def _cross(x2d, b, s, g_cross, w_cq, k_mem, v_mem, w_co, g_ffn, w_router_t):
    t = b * s
    n_s = s // TM
    m_len = k_mem.shape[1]
    tok = lambda: pl.BlockSpec((TM, D_MODEL), lambda i: (i, 0))
    mem_blk = pl.BlockSpec((1, m_len, D_MODEL), lambda i: (i // n_s, 0, 0))
    return pl.pallas_call(
        _cross_kernel,
        grid=(t // TM,),
        in_specs=[tok(), _const_spec((1, D_MODEL)), _const_spec((D_MODEL, D_MODEL)), mem_blk, mem_blk,
                  _const_spec((D_MODEL, D_MODEL)), _const_spec((1, D_MODEL)),
                  _const_spec((N_EXPERTS, D_MODEL))],
        out_specs=[tok(), tok(), pl.BlockSpec((N_EXPERTS, TM), lambda i: (0, i))],
        out_shape=[jax.ShapeDtypeStruct((t, D_MODEL), F32), jax.ShapeDtypeStruct((t, D_MODEL), BF16),
                   jax.ShapeDtypeStruct((N_EXPERTS, t), F32)],
        compiler_params=_params(1),
        name="cross",
    )(x2d, g_cross, w_cq, k_mem, v_mem, w_co, g_ffn, w_router_t)


def _select_kernel(aff_ref, pos_ref, *, n_tok, cap):
    n_cnt = n_tok // CNT_CH

    def count_ge(cand):
        def body(c, acc):
            bits = pltpu.bitcast(aff_ref[:, pl.ds(pl.multiple_of(c * CNT_CH, CNT_CH), CNT_CH)], I32)
            return acc + jnp.sum(jnp.where(bits >= cand, 1.0, 0.0), axis=1, keepdims=True)
        return lax.fori_loop(0, n_cnt, body, jnp.zeros((N_EXPERTS, 1), F32))

    def bit_body(i, prefix):
        cand = prefix | lax.shift_left(jnp.int32(1), 30 - i)
        return jnp.where(count_ge(cand) >= cap, cand, prefix)

    thr = lax.fori_loop(0, 31, bit_body, jnp.zeros((N_EXPERTS, 1), I32))
    need_eq = cap - count_ge(thr + 1)

    tri_r = lax.broadcasted_iota(I32, (CH, CH), 0)
    tri_c = lax.broadcasted_iota(I32, (CH, CH), 1)
    tri = jnp.where(tri_r <= tri_c, 1.0, 0.0).astype(BF16)

    def chunk_body(c, carry):
        c_eq, c_row = carry
        sl = pl.ds(pl.multiple_of(c * CH, CH), CH)
        bits = pltpu.bitcast(aff_ref[:, sl], I32)
        eq = jnp.where(bits == thr, 1.0, 0.0)
        gt = jnp.where(bits > thr, 1.0, 0.0)
        eq_before = c_eq + _dot(eq.astype(BF16), tri) - eq
        sel = gt + eq * jnp.where(eq_before < need_eq, 1.0, 0.0)
        row = c_row + _dot(sel.astype(BF16), tri) - sel
        pos_ref[:, sl] = jnp.where(sel > 0.5, row, -1.0).astype(I32)
        taken = jnp.sum(sel, axis=1, keepdims=True)
        c_row = c_row + jnp.ceil(taken * (1.0 / ROW_ALIGN)) * ROW_ALIGN
        return (c_eq + jnp.sum(eq, axis=1, keepdims=True), c_row)

    zero = jnp.zeros((N_EXPERTS, 1), F32)
    lax.fori_loop(0, n_tok // CH, chunk_body, (zero, zero))


def _select(aff_t, cap):
    n_tok = aff_t.shape[1]
    return pl.pallas_call(
        functools.partial(_select_kernel, n_tok=n_tok, cap=cap),
        grid=(1,),
        in_specs=[_const_spec(aff_t.shape)],
        out_specs=_const_spec(aff_t.shape),
        out_shape=jax.ShapeDtypeStruct(aff_t.shape, I32),
        compiler_params=_params(1),
        name="select",
    )(aff_t)


def _window_hits(pos_ref, base_ref, e, k, n_k, p):
    rows = base_ref[e * n_k + k] + p * WIN + lax.broadcasted_iota(I32, (WIN, CH), 0)
    return pos_ref[e:e + 1, :] == rows


def _window_to_hbm(scr, hbm, sem, e, row0):
    return pltpu.make_async_copy(scr.at[pl.ds(e * WIN, WIN)], hbm.at[e, pl.ds(row0, WIN)], sem.at[e])


def _window_from_hbm(hbm, scr, sem, e, row0):
    return pltpu.make_async_copy(hbm.at[e, pl.ds(row0, WIN)], scr.at[pl.ds(e * WIN, WIN)], sem.at[e])


def _gather_kernel(base_ref, npass_ref, h_ref, pos_ref, aff_ref, xe_in, xe_ref, x_scr, sem_x, *, n_k):
    del xe_in
    k = pl.program_id(0)
    n_pass = npass_ref[k]
    xs = x_scr.at[k % 2]
    gate_lane = lax.broadcasted_iota(I32, (WIN, LANES), 1)

    def window_copies(kk, p):
        copies = []
        for e in range(N_EXPERTS):
            row0 = pl.multiple_of(base_ref[e * n_k + kk] + p * WIN, ROW_ALIGN)
            copies.append(_window_to_hbm(x_scr.at[kk % 2], xe_ref, sem_x, e, row0))
        return copies

    def one_pass(p, carry):
        for e0 in range(0, N_EXPERTS, GATHER_GROUP):
            hits = [_window_hits(pos_ref, base_ref, e, k, n_k, p) for e in range(e0, e0 + GATHER_GROUP)]
            onehot = jnp.concatenate([jnp.where(h, 1.0, 0.0).astype(BF16) for h in hits], axis=0)
            xs[e0 * WIN:(e0 + GATHER_GROUP) * WIN, 0:D_MODEL] = _dot(onehot, h_ref[...]).astype(BF16)
            for i, h in enumerate(hits):
                e = e0 + i
                gate = jnp.sum(jnp.where(h, aff_ref[e:e + 1, :], 0.0), axis=1, keepdims=True)
                hi = gate.astype(BF16).astype(F32)
                parts = jnp.where(gate_lane < LANES // 2, hi, gate - hi)
                xs[e * WIN:(e + 1) * WIN, D_MODEL:XE_COLS] = parts.astype(BF16)

        @pl.when((p == 0) & (k > 0))
        def _():
            for c in window_copies(k - 1, npass_ref[k - 1] - 1):
                c.wait()

        for c in window_copies(k, p):
            c.start()

        @pl.when(p < n_pass - 1)
        def _():
            for c in window_copies(k, p):
                c.wait()

        return carry

    lax.fori_loop(0, n_pass, one_pass, 0)

    @pl.when(k == n_k - 1)
    def _():
        for c in window_copies(k, n_pass - 1):
            c.wait()


def _expert_gather(base, npass, h3, pos, aff_t, rows_pad):
    n_tok = h3.shape[0]
    n_k = n_tok // CH
    xe0 = jnp.zeros((N_EXPERTS, rows_pad, XE_COLS), BF16)
    grid_spec = pltpu.PrefetchScalarGridSpec(
        num_scalar_prefetch=2,
        grid=(n_k,),
        in_specs=[
            pl.BlockSpec((CH, D_MODEL), lambda k, *_: (k, 0)),
            pl.BlockSpec((N_EXPERTS, CH), lambda k, *_: (0, k)),
            pl.BlockSpec((N_EXPERTS, CH), lambda k, *_: (0, k)),
            pl.BlockSpec(memory_space=pl.ANY),
        ],
        out_specs=pl.BlockSpec(memory_space=pl.ANY),
        scratch_shapes=[pltpu.VMEM((2, N_EXPERTS * WIN, XE_COLS), BF16), pltpu.SemaphoreType.DMA((N_EXPERTS,))],
    )
    return pl.pallas_call(
        functools.partial(_gather_kernel, n_k=n_k),
        grid_spec=grid_spec,
        out_shape=jax.ShapeDtypeStruct(xe0.shape, BF16),
        input_output_aliases={5: 0},
        compiler_params=_params(1),
        name="expert_gather",
    )(base, npass, h3, pos, aff_t, xe0)


def _ffn_kernel(end_ref, xe_ref, wg_ref, wu_ref, wd_ref, ye_ref):
    e, j = pl.program_id(0), pl.program_id(1)

    @pl.when(j * RB < end_ref[e])
    def _():
        xe = xe_ref[0, :, 0:D_MODEL]
        parts = xe_ref[0, :, D_MODEL:XE_COLS].astype(F32)
        gate = parts[:, 0:1] + parts[:, LANES // 2:LANES // 2 + 1]
        hid = jax.nn.silu(_dot(xe, wg_ref[0])) * _dot(xe, wu_ref[0])
        ye_ref[0] = (_dot(hid.astype(BF16), wd_ref[0]) * gate).astype(BF16)

    @pl.when(j * RB >= end_ref[e])
    def _():
        ye_ref[0] = jnp.zeros((RB, D_MODEL), BF16)


def _expert_ffn(end, xe, w_e_gate, w_e_up, w_e_down):
    rows_pad = xe.shape[1]
    grid_spec = pltpu.PrefetchScalarGridSpec(
        num_scalar_prefetch=1,
        grid=(N_EXPERTS, rows_pad // RB),
        in_specs=[
            pl.BlockSpec((1, RB, XE_COLS), lambda e, j, *_: (e, j, 0)),
            pl.BlockSpec((1, D_MODEL, EXPERT_FF), lambda e, j, *_: (e, 0, 0)),
            pl.BlockSpec((1, D_MODEL, EXPERT_FF), lambda e, j, *_: (e, 0, 0)),
            pl.BlockSpec((1, EXPERT_FF, D_MODEL), lambda e, j, *_: (e, 0, 0)),
        ],
        out_specs=pl.BlockSpec((1, RB, D_MODEL), lambda e, j, *_: (e, j, 0)),
    )
    return pl.pallas_call(
        _ffn_kernel,
        grid_spec=grid_spec,
        out_shape=jax.ShapeDtypeStruct(xe.shape[:2] + (D_MODEL,), BF16),
        compiler_params=_params(2),
        name="expert_ffn",
    )(end, xe, w_e_gate, w_e_up, w_e_down)


def _scatter_kernel(base_ref, npass_ref, x_ref, pos_ref, ye_ref, gfin_ref, y_ref, stage, sem,
                    *, final_norm, n_k):
    k = pl.program_id(0)
    slot = k % 2
    y_ref[...] = x_ref[...]

    def window_copies(kk, p, s):
        copies = []
        for e in range(N_EXPERTS):
            row0 = pl.multiple_of(base_ref[e * n_k + kk] + p * WIN, ROW_ALIGN)
            copies.append(_window_from_hbm(ye_ref, stage.at[s], sem.at[s], e, row0))
        return copies

    def onehot(p):
        return jnp.concatenate(
            [jnp.where(_window_hits(pos_ref, base_ref, e, k, n_k, p), 1.0, 0.0).astype(BF16)
             for e in range(N_EXPERTS)], axis=0)

    @pl.when(k == 0)
    def _():
        for c in window_copies(0, 0, 0):
            c.start()

    @pl.when(k + 1 < n_k)
    def _():
        for c in window_copies(k + 1, 0, 1 - slot):
            c.start()

    hot = onehot(0)
    for c in window_copies(k, 0, slot):
        c.wait()
    y_ref[...] += _dot_tn(hot, stage[slot])

    def later_pass(p, carry):
        copies = window_copies(k, p, slot)
        for c in copies:
            c.start()
        hot = onehot(p)
        for c in copies:
            c.wait()
        y_ref[...] += _dot_tn(hot, stage[slot])
        return carry

    lax.fori_loop(1, npass_ref[k], later_pass, 0)
    if final_norm:
        y_ref[...] = _rms(y_ref[...], gfin_ref[...])


def _expert_scatter(base, npass, x2d, pos, ye, g_final, final_norm):
    t = x2d.shape[0]
    n_k = t // CH
    grid_spec = pltpu.PrefetchScalarGridSpec(
        num_scalar_prefetch=2,
        grid=(n_k,),
        in_specs=[
            pl.BlockSpec((CH, D_MODEL), lambda k, *_: (k, 0)),
            pl.BlockSpec((N_EXPERTS, CH), lambda k, *_: (0, k)),
            pl.BlockSpec(memory_space=pl.ANY),
            pl.BlockSpec((1, D_MODEL), lambda k, *_: (0, 0)),
        ],
        out_specs=pl.BlockSpec((CH, D_MODEL), lambda k, *_: (k, 0)),
        scratch_shapes=[pltpu.VMEM((2, N_EXPERTS * WIN, D_MODEL), BF16), pltpu.SemaphoreType.DMA((2, N_EXPERTS))],
    )
    return pl.pallas_call(
        functools.partial(_scatter_kernel, final_norm=final_norm, n_k=n_k),
        grid_spec=grid_spec,
        out_shape=jax.ShapeDtypeStruct((t, D_MODEL), F32),
        compiler_params=_params(1),
        name="expert_scatter",
    )(base, npass, x2d, pos, ye, g_final)


def _segments(pos):
    n_e, n_tok = pos.shape
    n_k = n_tok // CH
    cnt = jnp.sum((pos >= 0).reshape(n_e, n_k, CH), axis=-1, dtype=I32)
    padded = (cnt + (ROW_ALIGN - 1)) // ROW_ALIGN * ROW_ALIGN
    end = jnp.cumsum(padded, axis=1)
    base = end - padded
    npass = jnp.maximum(jnp.max((cnt + (WIN - 1)) // WIN, axis=0), 1)
    return base.reshape(-1).astype(I32), npass.astype(I32), end[:, -1].astype(I32)


def _rows_pad(n_tok, cap):
    rows = cap + ROW_ALIGN * (n_tok // CH) + CH + WIN
    return (rows + RB - 1) // RB * RB


def _rope_tables(s):
    pos = jnp.arange(s, dtype=F32)[:, None]
    lane = jnp.arange(LANES)
    tabs = []
    for period, base, rot in ((LANES, MLA_NOPE, MLA_ROPE), (HEAD_DIM, 0, HEAD_DIM // 4), (DIFF_QK, 0, DIFF_QK // 4)):
        half = rot // 2
        rel = lane % period - base
        inv_freq = ROPE_THETA ** (-jnp.arange(half, dtype=F32) * (2.0 / rot))
        ang = pos * inv_freq[None, :]
        cos, sin = jnp.cos(ang), jnp.sin(ang)
        idx = jnp.clip(jnp.where(rel >= half, rel - half, rel), 0, half - 1)
        first = (rel >= 0) & (rel < half)
        second = (rel >= half) & (rel < rot)
        tabs.append(jnp.where(first | second, cos[:, idx], 1.0))
        tabs.append(jnp.where(first, -sin[:, idx], 0.0))
        tabs.append(jnp.where(second, sin[:, idx], 0.0))
    return jnp.stack(tabs).astype(F32)


def _prep_layer(l, w_in, g_mix, g_cq, w_uq, g_ckv, w_ukv, lam_q1, lam_k1, lam_q2, lam_k2, g_diff, na_bias,
                w_gate, w_branch, w_out, g_cross, g_mem, w_cq, w_ck, w_cv, w_co, g_ffn, w_router,
                w_e_gate, w_e_up, w_e_down):
    d = D_MODEL
    wi = w_in[l]
    kpe0 = MLA_Q_RANK + MLA_KV_RANK
    z64 = jnp.zeros((d, MLA_NOPE), F32)
    z32 = jnp.zeros((d, LANES - MLA_NOPE - MLA_ROPE), F32)
    rest = wi[:, kpe0 + MLA_ROPE:]
    n_g = len(DIL_GROUPS)
    n_dil = n_g * Z_DIL_GROUP
    vc0 = n_dil + 512
    dil_cols = rest[:, :n_dil].reshape(d, 3, n_g, 256).transpose(0, 2, 1, 3).reshape(d, n_dil)
    w_z = jnp.concatenate([wi[:, :kpe0], z64, wi[:, kpe0:kpe0 + MLA_ROPE], z32, dil_cols[:, :Z_DIL_GROUP],
                           rest[:, n_dil:vc0], rest[:, vc0 + 256:], dil_cols[:, Z_DIL_GROUP:]], axis=1)
    w_vct = rest[:, vc0:vc0 + 256].T
    qh = MLA_NOPE + MLA_ROPE
    wuq = w_uq[l].reshape(MLA_Q_RANK, MLA_HEADS, qh)
    wuq = jnp.pad(wuq, ((0, 0), (0, 0), (0, LANES - qh))).reshape(MLA_Q_RANK, MLA_HEADS * LANES)
    wukv = w_ukv[l].reshape(MLA_KV_RANK, MLA_HEADS, MLA_NOPE + HEAD_DIM)
    wk = jnp.pad(wukv[:, :, :MLA_NOPE], ((0, 0), (0, 0), (0, LANES - MLA_NOPE))).reshape(MLA_KV_RANK, MLA_HEADS * LANES)
    wv = wukv[:, :, MLA_NOPE:].reshape(MLA_KV_RANK, MLA_HEADS * HEAD_DIM)
    return dict(
        g_mix=g_mix[l][None], w_z=w_z.astype(BF16), g_cq=g_cq[l][None], g_ckv=g_ckv[l][None],
        w_uq=wuq.astype(BF16), w_k=wk.astype(BF16), w_vt=wv.T.astype(BF16), w_vct=w_vct.astype(BF16),
        lam=jnp.stack([lam_q1[l], lam_k1[l], lam_q2[l], lam_k2[l]]).astype(F32),
        g_diff=jnp.tile(g_diff[l], 2)[None], na_bias=na_bias[l],
        w_gate=w_gate[l].astype(BF16), w_branch=w_branch[l].astype(BF16), w_out=w_out[l].astype(BF16),
        g_cross=g_cross[l][None], g_mem=g_mem[l][None], w_cq=w_cq[l].astype(BF16), w_ck=w_ck[l].astype(BF16),
        w_cv=w_cv[l].astype(BF16), w_co=w_co[l].astype(BF16), g_ffn=g_ffn[l][None],
        w_router_t=w_router[l].T.astype(BF16), w_e_gate=w_e_gate[l].astype(BF16),
        w_e_up=w_e_up[l].astype(BF16), w_e_down=w_e_down[l].astype(BF16),
    )


def _trunk(x, mem, layers, bias_tiles, g_final):
    b, s, d = x.shape
    t = b * s
    cap = CAPACITY_FACTOR * t // N_EXPERTS
    assert s % TM == 0 and s % TQ == 0 and s % TK == 0 and t % CH == 0 and cap % RB == 0
    rope_tab = _rope_tables(s)
    x2d = x.reshape(t, d)
    for l, p in enumerate(layers):
        zr, qa, ka, vat, vct, d1, d2 = _project(x2d, b, s, p["g_mix"], p["w_z"], rope_tab, p["g_cq"], p["g_ckv"],
                                                p["w_uq"], p["w_k"], p["w_vt"], p["w_vct"])
        zr3 = zr.reshape(b, s, ZR_COLS)
        oa = _mla_attention(qa, ka, vat).reshape(t, 256)
        dil = [_dil_attention(zr.reshape(b, 1, s, ZR_COLS), (Z_DIL0 - Z_HEAD) // 256),
               _dil_attention(d1, 0), _dil_attention(d2, 0)]
        lam_init = 0.8 - 0.6 * math.exp(-0.3 * l)
        oc = _diff_attention(zr3, vct, p["lam"], p["g_diff"], lam_init).reshape(t, 256)
        od = _na_attention(zr3, bias_tiles[l]).reshape(t, 256)
        x2d = _merge(x2d, s, p["g_mix"], oa, [o for o, _ in dil], [ls for _, ls in dil], oc, od,
                     p["w_gate"], p["w_branch"], p["w_out"])
        k_mem, v_mem = _mem_kv(mem, p["g_mem"], p["w_ck"], p["w_cv"])
        x2d, h3, aff_t = _cross(x2d, b, s, p["g_cross"], p["w_cq"], k_mem, v_mem, p["w_co"], p["g_ffn"],
                                p["w_router_t"])
        pos = _select(aff_t, cap)
        base, npass, end = _segments(pos)
        xe = _expert_gather(base, npass, h3, pos, aff_t, _rows_pad(t, cap))
        ye = _expert_ffn(end, xe, p["w_e_gate"], p["w_e_up"], p["w_e_down"])
        x2d = _expert_scatter(base, npass, x2d, pos, ye, g_final[None], final_norm=(l == len(layers) - 1))
    return x2d.reshape(b, s, d)


def kernel(x_prompt, x_sample, mem_prompt, mem_sample, w_in, g_mix, g_cq, w_uq, g_ckv, w_ukv, lam_q1, lam_k1, lam_q2, lam_k2, g_diff, na_bias, w_gate, w_branch, w_out, g_cross, g_mem, w_cq, w_ck, w_cv, w_co, g_ffn, w_router, w_e_gate, w_e_up, w_e_down, g_final):
    depth = w_in.shape[0]
    layers = [_prep_layer(l, w_in, g_mix, g_cq, w_uq, g_ckv, w_ukv, lam_q1, lam_k1, lam_q2, lam_k2, g_diff,
                          na_bias, w_gate, w_branch, w_out, g_cross, g_mem, w_cq, w_ck, w_cv, w_co, g_ffn,
                          w_router, w_e_gate, w_e_up, w_e_down) for l in range(depth)]
    bias_tiles = [_na_bias_tiles(p["na_bias"]) for p in layers]
    y_prompt = _trunk(x_prompt, mem_prompt, layers, bias_tiles, g_final)
    y_sample = _trunk(x_sample, mem_sample, layers, bias_tiles, g_final)
    return (y_prompt, y_sample)
```

```python
import functools
import math

import jax
import jax.numpy as jnp
from jax import lax
from jax.experimental import pallas as pl
from jax.experimental.pallas import tpu as pltpu

BF16 = jnp.bfloat16
F32 = jnp.float32
I32 = jnp.int32

D_MODEL = 1024
EPS = 1e-6
NEG = -1e30
ROPE_THETA = 500000.0
HEAD_DIM = 64
MLA_HEADS = 4
MLA_Q_RANK = 256
MLA_KV_RANK = 128
MLA_NOPE = 64
MLA_ROPE = 32
MLA_SCALE = (MLA_NOPE + MLA_ROPE) ** -0.5
DIL_GROUPS = ((128, 1), (512, 4), (2048, 16))
DIL_SIDE = 64
DIL_SCALE = HEAD_DIM ** -0.5
DIFF_QK = 32
DIFF_SCALE = DIFF_QK ** -0.5
NA_ROWS = 8
NA_COLS = 16
GRID_W = 64
NA_SCALE = HEAD_DIM ** -0.5
MEM_HEADS = 4
MEM_HEAD_DIM = D_MODEL // MEM_HEADS
MEM_SCALE = MEM_HEAD_DIM ** -0.5
N_EXPERTS = 16
EXPERT_FF = 512
CAPACITY_FACTOR = 2

LANES = 128
VMEM_LIMIT = 56 * 1024 * 1024

Z_CQ = 0
Z_CKV = 256
Z_KPE = 384
Z_HEAD = 512
Z_DIL0 = 512
Z_DIL_GROUP = 768
Z_QC = 1280
Z_KC = 1536
Z_QD = 1792
Z_KD = 2048
Z_VD = 2304
Z_DILX = 2560
Z_COLS = Z_DILX + 2 * Z_DIL_GROUP
ZR_COLS = Z_DILX - Z_HEAD
LOG2E = math.log2(math.e)

TM = 512
TQ = 256
TK = 512
MLA_UNROLL = 8
DIFF_UNROLL = 8
TU = 128
DIL_TILES = 2
LOCAL_TILES = 2
NA_TQ = 128
NA_WIN_ROWS = 10
NA_BIAS_OFF = 2
NA_BIAS_N = 18
RB = 512
CH = 512
WIN = 128
XE_COLS = D_MODEL + LANES
ROW_ALIGN = 16
GATHER_GROUP = 4
CNT_CH = 4096


def _params(n_grid):
    return pltpu.CompilerParams(dimension_semantics=("arbitrary",) * n_grid,
                                vmem_limit_bytes=VMEM_LIMIT)


def _rms(x, g):
    return x * lax.rsqrt(jnp.mean(x * x, axis=-1, keepdims=True) + EPS) * g


def _dot(a, b):
    return jnp.dot(a, b, preferred_element_type=F32)


def _dot_nt(a, b):
    return lax.dot_general(a, b, (((1,), (1,)), ((), ())), preferred_element_type=F32)


def _dot_tn(a, b):
    return lax.dot_general(a, b, (((0,), (0,)), ((), ())), preferred_element_type=F32)


def _rope_block(x, c, sa, sb, half):
    return x * c + pltpu.roll(x, LANES - half, 1) * sa + pltpu.roll(x, half, 1) * sb


def _proj_kernel(x_ref, g_ref, w_ref, rope_ref, gcq_ref, gckv_ref, wuq_ref, wk_ref, wvt_ref, wvct_ref,
                 zr_ref, qa_ref, ka_ref, vat_ref, vct_ref, d1_ref, d2_ref, scr_ref):
    hb = _rms(x_ref[...], g_ref[...]).astype(BF16)

    def rope(x, kind, half):
        return _rope_block(x, rope_ref[3 * kind], rope_ref[3 * kind + 1], rope_ref[3 * kind + 2], half)

    z0 = _dot(hb, w_ref[:, 0:Z_HEAD])
    cq = _rms(z0[:, Z_CQ:Z_CQ + MLA_Q_RANK], gcq_ref[...]).astype(BF16)
    ckv = _rms(z0[:, Z_CKV:Z_CKV + MLA_KV_RANK], gckv_ref[...]).astype(BF16)
    kpe = rope(z0[:, Z_KPE:Z_KPE + LANES], 0, MLA_ROPE // 2)
    qa = _dot(cq, wuq_ref[...]) * (MLA_SCALE * LOG2E)
    kn = _dot(ckv, wk_ref[...])
    for h in range(MLA_HEADS):
        sl = slice(h * LANES, (h + 1) * LANES)
        qa_ref[0, h] = rope(qa[:, sl], 0, MLA_ROPE // 2).astype(BF16)
        ka_ref[0, h] = (kn[:, sl] + kpe).astype(BF16)
    vat_ref[0] = _dot_nt(wvt_ref[...], ckv).astype(BF16)
    vct_ref[0] = _dot_nt(wvct_ref[...], hb).astype(BF16)

    n_scr = [0]

    def seg(start, scale, kind, half, dil_out=None):
        z = _dot(hb, w_ref[:, start:start + 256])
        if scale is not None:
            z = z * scale
        for j in range(2):
            zb = z[:, j * LANES:(j + 1) * LANES]
            if kind is not None:
                zb = rope(zb, kind, half)
            if dil_out is None:
                o = start - Z_HEAD + j * LANES
                zr_ref[:, o:o + LANES] = zb.astype(BF16)
            else:
                out_ref, col, dil = dil_out
                slot = n_scr[0] % scr_ref.shape[0]
                n_scr[0] += 1
                scr_ref[slot] = zb
                for r in range(dil):
                    out_ref[0, r, :, col + j * LANES:col + (j + 1) * LANES] = (
                        scr_ref[slot, pl.ds(r, TM // dil, stride=dil), :].astype(BF16))

    dil_half = HEAD_DIM // 4 // 2
    diff_half = DIFF_QK // 4 // 2
    seg(Z_DIL0, DIL_SCALE, 1, dil_half)
    seg(Z_DIL0 + 256, None, 1, dil_half)
    seg(Z_DIL0 + 512, None, None, 0)
    for g, out_ref in ((1, d1_ref), (2, d2_ref)):
        base = Z_DILX + (g - 1) * Z_DIL_GROUP
        dil = DIL_GROUPS[g][1]
        seg(base, DIL_SCALE, 1, dil_half, (out_ref, 0, dil))
        seg(base + 256, None, 1, dil_half, (out_ref, 256, dil))
        seg(base + 512, None, None, 0, (out_ref, 512, dil))
    seg(Z_QC, DIFF_SCALE * LOG2E, 2, diff_half)
    seg(Z_KC, None, 2, diff_half)
    seg(Z_QD, NA_SCALE, None, 0)
    seg(Z_KD, None, None, 0)
    seg(Z_VD, None, None, 0)


def _const_spec(shape):
    nd = len(shape)
    return pl.BlockSpec(shape, lambda *_: (0,) * nd)


def _project(x2d, b, s, g_mix, w_z, rope_tab, g_cq, g_ckv, w_uq, w_k, w_vt, w_vct):
    t = b * s
    n_s = s // TM
    v_cols = MLA_HEADS * HEAD_DIM
    dils = [dl for _, dl in DIL_GROUPS[1:]]
    return pl.pallas_call(
        _proj_kernel,
        grid=(t // TM,),
        in_specs=[
            pl.BlockSpec((TM, D_MODEL), lambda i: (i, 0)),
            _const_spec((1, D_MODEL)),
            _const_spec((D_MODEL, Z_COLS)),
            pl.BlockSpec((9, TM, LANES), lambda i: (0, i % n_s, 0)),
            _const_spec((1, MLA_Q_RANK)),
            _const_spec((1, MLA_KV_RANK)),
            _const_spec((MLA_Q_RANK, MLA_HEADS * LANES)),
            _const_spec((MLA_KV_RANK, MLA_HEADS * LANES)),
            _const_spec((v_cols, MLA_KV_RANK)),
            _const_spec((v_cols, D_MODEL)),
        ],
        out_specs=[
            pl.BlockSpec((TM, ZR_COLS), lambda i: (i, 0)),
            pl.BlockSpec((1, MLA_HEADS, TM, LANES), lambda i: (i // n_s, 0, i % n_s, 0)),
            pl.BlockSpec((1, MLA_HEADS, TM, LANES), lambda i: (i // n_s, 0, i % n_s, 0)),
            pl.BlockSpec((1, v_cols, TM), lambda i: (i // n_s, 0, i % n_s)),
            pl.BlockSpec((1, v_cols, TM), lambda i: (i // n_s, 0, i % n_s)),
        ] + [pl.BlockSpec((1, dl, TM // dl, Z_DIL_GROUP), lambda i: (i // n_s, 0, i % n_s, 0)) for dl in dils],
        out_shape=[
            jax.ShapeDtypeStruct((t, ZR_COLS), BF16),
            jax.ShapeDtypeStruct((b, MLA_HEADS, s, LANES), BF16),
            jax.ShapeDtypeStruct((b, MLA_HEADS, s, LANES), BF16),
            jax.ShapeDtypeStruct((b, v_cols, s), BF16),
            jax.ShapeDtypeStruct((b, v_cols, s), BF16),
        ] + [jax.ShapeDtypeStruct((b, dl, s // dl, Z_DIL_GROUP), BF16) for dl in dils],
        scratch_shapes=[pltpu.VMEM((4, TM, LANES), F32)],
        compiler_params=_params(1),
        name="proj",
    )(x2d, g_mix, w_z, rope_tab, g_cq, g_ckv, w_uq, w_k, w_vt, w_vct)


def _chunk(c):
    return pl.ds(pl.multiple_of(c * TK, TK), TK)


def _dense_attention(qs, k_at, vt_at, s_ref, p_ref, n_chunks, unroll):
    n = len(qs)

    def step(c, cur, nxt, carry):
        rows_next = _chunk(jnp.minimum(c + 1, n_chunks - 1))
        rows_prev = _chunk(jnp.maximum(c - 1, 0))
        new = []
        for i in range(n):
            m, l, acc, alpha_prev = carry[i]
            s_ref[i, nxt] = _dot_nt(k_at(i, rows_next), qs[i])
            acc = alpha_prev * acc + _dot(vt_at(i, rows_prev), p_ref[i, nxt])
            st = s_ref[i, cur]
            m_new = jnp.maximum(m, jnp.max(st, axis=0, keepdims=True))
            alpha = jnp.exp2(m - m_new)
            p = jnp.exp2(st - m_new)
            l = alpha * l + jnp.sum(p, axis=0, keepdims=True)
            p_ref[i, cur] = p.astype(BF16)
            new.append((m_new, l, acc, alpha))
        return tuple(new)

    def body(j, carry):
        for u in range(unroll):
            carry = step(unroll * j + u, u % 2, (u + 1) % 2, carry)
        return carry

    for i in range(n):
        s_ref[i, 0] = _dot_nt(k_at(i, pl.ds(0, TK)), qs[i])
        p_ref[i, 1] = jnp.zeros((TK, TQ), BF16)
    init = tuple((jnp.full((1, TQ), NEG, F32), jnp.zeros((1, TQ), F32), jnp.zeros((HEAD_DIM, TQ), F32),
                  jnp.ones((1, TQ), F32)) for _ in range(n))
    assert unroll % 2 == 0 and n_chunks % unroll == 0
    res = lax.fori_loop(0, n_chunks // unroll, body, init)
    out = []
    for i in range(n):
        _, l, acc, alpha = res[i]
        acc = alpha * acc + _dot(vt_at(i, pl.ds((n_chunks - 1) * TK, TK)), p_ref[i, 1])
        out.append((acc, l))
    return out


def _mla_kernel(q_ref, k_ref, vt_ref, o_ref, s_ref, p_ref, *, n_chunks):
    res = _dense_attention(
        [q_ref[0, hh] for hh in range(2)],
        lambda i, rows: k_ref[0, i, rows, :],
        lambda i, rows: vt_ref[0, i * HEAD_DIM:(i + 1) * HEAD_DIM, rows],
        s_ref, p_ref, n_chunks, MLA_UNROLL)
    out_t = jnp.concatenate([acc / l for acc, l in res], axis=0)
    o_ref[0] = out_t.T.astype(BF16)


def _mla_attention(qa, ka, vat):
    b, _, s, _ = qa.shape
    return pl.pallas_call(
        functools.partial(_mla_kernel, n_chunks=s // TK),
        grid=(b, 2, s // TQ),
        in_specs=[
            pl.BlockSpec((1, 2, TQ, LANES), lambda bi, p, qi: (bi, p, qi, 0)),
            pl.BlockSpec((1, 2, s, LANES), lambda bi, p, qi: (bi, p, 0, 0)),
            pl.BlockSpec((1, LANES, s), lambda bi, p, qi: (bi, p, 0)),
        ],
        out_specs=pl.BlockSpec((1, TQ, LANES), lambda bi, p, qi: (bi, qi, p)),
        out_shape=jax.ShapeDtypeStruct((b, s, 2 * LANES), BF16),
        scratch_shapes=[pltpu.VMEM((2, 2, TK, TQ), F32), pltpu.VMEM((2, 2, TK, TQ), BF16)],
        compiler_params=_params(3),
        name="mla_attn",
    )(qa, ka, vat)


def _diff_kernel(q_ref, k_ref, vt_ref, lam_ref, g_ref, o_ref, s_ref, p_ref, *, n_chunks, lam_init):
    lane = lax.broadcasted_iota(I32, (TQ, LANES), 1)
    lv = lam_ref[...]
    lam = (jnp.exp(jnp.sum(lv[0:1] * lv[1:2], axis=-1, keepdims=True))
           - jnp.exp(jnp.sum(lv[2:3] * lv[3:4], axis=-1, keepdims=True)) + lam_init)
    q = q_ref[0]
    qs = [jnp.where((lane >= i * DIFF_QK) & (lane < (i + 1) * DIFF_QK), q, jnp.zeros_like(q)) for i in range(4)]
    heads = []
    for hh in range(2):
        (acc1, l1), (acc2, l2) = _dense_attention(
            qs[2 * hh:2 * hh + 2],
            lambda i, rows: k_ref[0, rows, :],
            lambda i, rows: vt_ref[0, hh * HEAD_DIM:(hh + 1) * HEAD_DIM, rows],
            s_ref.at[pl.ds(2 * hh, 2)], p_ref.at[pl.ds(2 * hh, 2)], n_chunks, DIFF_UNROLL)
        heads.append(acc1 / l1 - lam * (acc2 / l2))
    out_t = jnp.concatenate(heads, axis=0)
    o = out_t.T
    head0 = lane < HEAD_DIM
    sq = o * o
    ms0 = jnp.sum(jnp.where(head0, sq, 0.0), axis=-1, keepdims=True)
    ms1 = jnp.sum(jnp.where(head0, 0.0, sq), axis=-1, keepdims=True)
    ms = jnp.where(head0, ms0, ms1) * (1.0 / HEAD_DIM)
    o_ref[0] = (o * lax.rsqrt(ms + EPS) * g_ref[...] * (1.0 - lam_init)).astype(BF16)


def _diff_attention(zr3, vct, lam_vecs, g_diff2, lam_init):
    b, s, _ = zr3.shape
    qb, kb = (Z_QC - Z_HEAD) // LANES, (Z_KC - Z_HEAD) // LANES
    return pl.pallas_call(
        functools.partial(_diff_kernel, n_chunks=s // TK, lam_init=lam_init),
        grid=(b, 2, s // TQ),
        in_specs=[
            pl.BlockSpec((1, TQ, LANES), lambda bi, p, qi: (bi, qi, qb + p)),
            pl.BlockSpec((1, s, LANES), lambda bi, p, qi: (bi, 0, kb + p)),
            pl.BlockSpec((1, LANES, s), lambda bi, p, qi: (bi, p, 0)),
            _const_spec((4, DIFF_QK)),
            _const_spec((1, LANES)),
        ],
        out_specs=pl.BlockSpec((1, TQ, LANES), lambda bi, p, qi: (bi, qi, p)),
        out_shape=jax.ShapeDtypeStruct((b, s, 2 * LANES), BF16),
        scratch_shapes=[pltpu.VMEM((4, 2, TK, TQ), F32), pltpu.VMEM((4, 2, TK, TQ), BF16)],
        compiler_params=_params(3),
        name="diff_attn",
    )(zr3, zr3, vct, lam_vecs, g_diff2)


def _local_heads(tiles):
    n = tiles[0][0].shape[0]
    lane = lax.broadcasted_iota(I32, (n, 4 * HEAD_DIM), 1)
    in_head = [(lane >= h * HEAD_DIM) & (lane < (h + 1) * HEAD_DIM) for h in range(4)]
    scores = [[_dot_nt(jnp.where(in_head[h], q, jnp.zeros_like(q)), kw) for h in range(4)]
              for q, kw, _, _, _ in tiles]
    probs, lses = [], []
    for (_, _, _, valid, biases), sc in zip(tiles, scores):
        tile_p, tile_l = [], []
        for h in range(4):
            s = sc[h] if biases is None else sc[h] + biases[h]
            s = jnp.where(valid, s, NEG)
            m = jnp.max(s, axis=-1, keepdims=True)
            e = jnp.exp(s - m)
            l = jnp.sum(e, axis=-1, keepdims=True)
            tile_p.append((e / l).astype(BF16))
            tile_l.append(m + jnp.log(l))
        probs.append(tile_p)
        lses.append(tile_l)
    outs = [[_dot(p, vw) for p in tile_p] for (_, _, vw, _, _), tile_p in zip(tiles, probs)]
    res = []
    for tile_o, tile_l in zip(outs, lses):
        out, lse = tile_o[3], tile_l[3]
        for h in (2, 1, 0):
            out = jnp.where(in_head[h], tile_o[h], out)
            lse = jnp.where(in_head[h], tile_l[h], lse)
        res.append((out, jnp.broadcast_to(lse, out.shape)))
    return res


def _dil_kernel(q_ref, k_ref, v_ref, o_ref, lse_ref, *, u_len, win):
    tiles = []
    for i in range(DIL_TILES):
        u0 = (pl.program_id(2) * DIL_TILES + i) * TU
        start = pl.multiple_of(jnp.clip(u0 - DIL_SIDE, 0, u_len - win), DIL_SIDE)
        qpos = u0 + lax.broadcasted_iota(I32, (TU, win), 0)
        kpos = start + lax.broadcasted_iota(I32, (TU, win), 1)
        tiles.append((q_ref[0, 0, i * TU:(i + 1) * TU, :], k_ref[0, 0, pl.ds(start, win), :],
                      v_ref[0, 0, pl.ds(start, win), :], jnp.abs(kpos - qpos) <= DIL_SIDE, None))
    for i, (out, lse_out) in enumerate(_local_heads(tiles)):
        o_ref[0, 0, i * TU:(i + 1) * TU, :] = out.astype(BF16)
        lse_ref[0, 0, i * TU:(i + 1) * TU, :] = lse_out


def _dil_attention(src, qu):
    b, dil, u_len, _ = src.shape
    win = TU + 2 * DIL_SIDE
    step = DIL_TILES * TU
    assert u_len >= win and u_len % step == 0
    return pl.pallas_call(
        functools.partial(_dil_kernel, u_len=u_len, win=win),
        grid=(b, dil, u_len // step),
        in_specs=[
            pl.BlockSpec((1, 1, step, 256), lambda bi, r, ui: (bi, r, ui, qu)),
            pl.BlockSpec((1, 1, u_len, 256), lambda bi, r, ui: (bi, r, 0, qu + 1)),
            pl.BlockSpec((1, 1, u_len, 256), lambda bi, r, ui: (bi, r, 0, qu + 2)),
        ],
        out_specs=[
            pl.BlockSpec((1, 1, step, 256), lambda bi, r, ui: (bi, r, ui, 0)),
            pl.BlockSpec((1, 1, step, 256), lambda bi, r, ui: (bi, r, ui, 0)),
        ],
        out_shape=[
            jax.ShapeDtypeStruct((b, dil, u_len, 256), BF16),
            jax.ShapeDtypeStruct((b, dil, u_len, 256), F32),
        ],
        compiler_params=_params(3),
        name=f"dil_attn_{dil}",
    )(src, src, src)


def _na_bias_kernel(tbl_ref, o_ref):
    h = pl.program_id(0)
    d = pl.program_id(1)
    n_rel_r, n_rel_c = 2 * NA_ROWS - 1, 2 * NA_COLS - 1
    dl = jnp.clip(d - NA_BIAS_OFF, 0, n_rel_r - 1)
    dr = jnp.clip(d - NA_BIAS_OFF + 1, 0, n_rel_r - 1)
    qc = lax.broadcasted_iota(I32, (GRID_W, LANES), 0)
    ln = lax.broadcasted_iota(I32, (GRID_W, LANES), 1)
    left = ln < GRID_W
    rel = jnp.where(left, ln, ln - GRID_W) - qc + (NA_COLS - 1)
    acc = jnp.zeros((GRID_W, LANES), F32)
    for dd in range(n_rel_c):
        vl = tbl_ref[(h * n_rel_r + dl) * n_rel_c + dd]
        vr = tbl_ref[(h * n_rel_r + dr) * n_rel_c + dd]
        acc = jnp.where(rel == dd, jnp.where(left, vl, vr), acc)
    o_ref[0, 0] = acc


def _na_bias_tiles(na_bias_l):
    n_h = na_bias_l.shape[0]
    return pl.pallas_call(
        _na_bias_kernel,
        grid=(n_h, NA_BIAS_N),
        in_specs=[pl.BlockSpec(memory_space=pltpu.SMEM)],
        out_specs=pl.BlockSpec((1, 1, GRID_W, LANES), lambda h, d: (h, d, 0, 0)),
        out_shape=jax.ShapeDtypeStruct((n_h, NA_BIAS_N, GRID_W, LANES), F32),
        compiler_params=_params(2),
        name="na_bias",
    )(na_bias_l.reshape(-1))


def _na_kernel(q_ref, k_ref, v_ref, bias_ref, o_ref, *, n_rows):
    n_keys = NA_WIN_ROWS * GRID_W
    qi = lax.broadcasted_iota(I32, (NA_TQ, n_keys), 0)
    kj = lax.broadcasted_iota(I32, (NA_TQ, n_keys), 1)
    tiles = []
    for i in range(LOCAL_TILES):
        r0 = (pl.program_id(1) * LOCAL_TILES + i) * 2
        ws = jnp.clip(r0 - NA_ROWS // 2, 0, n_rows - NA_WIN_ROWS)
        start = pl.multiple_of(ws * GRID_W, GRID_W)
        qr, qc = r0 + qi // GRID_W, qi % GRID_W
        kr, kc = ws + kj // GRID_W, kj % GRID_W
        rs = jnp.clip(qr - NA_ROWS // 2, 0, n_rows - NA_ROWS)
        cs = jnp.clip(qc - NA_COLS // 2, 0, GRID_W - NA_COLS)
        valid = (kr >= rs) & (kr < rs + NA_ROWS) & (kc >= cs) & (kc < cs + NA_COLS)
        biases = []
        for h in range(4):
            halves = []
            for half in range(2):
                d0 = ws - r0 - half + (NA_ROWS - 1) + NA_BIAS_OFF
                halves.append(jnp.concatenate(
                    [bias_ref[h, pl.ds(d0 + 2 * m, 1)][0] for m in range(NA_WIN_ROWS // 2)], axis=1))
            biases.append(jnp.concatenate(halves, axis=0))
        tiles.append((q_ref[0, i * NA_TQ:(i + 1) * NA_TQ, :], k_ref[0, pl.ds(start, n_keys), :],
                      v_ref[0, pl.ds(start, n_keys), :], valid, biases))
    for i, (out, _) in enumerate(_local_heads(tiles)):
        o_ref[0, i * NA_TQ:(i + 1) * NA_TQ, :] = out.astype(BF16)


def _na_attention(zr3, bias_tiles):
    b, s, _ = zr3.shape
    n_rows = s // GRID_W
    assert n_rows >= NA_WIN_ROWS
    qu, ku, vu = (Z_QD - Z_HEAD) // 256, (Z_KD - Z_HEAD) // 256, (Z_VD - Z_HEAD) // 256
    return pl.pallas_call(
        functools.partial(_na_kernel, n_rows=n_rows),
        grid=(b, s // (LOCAL_TILES * NA_TQ)),
        in_specs=[
            pl.BlockSpec((1, LOCAL_TILES * NA_TQ, 256), lambda bi, i: (bi, i, qu)),
            pl.BlockSpec((1, s, 256), lambda bi, i: (bi, 0, ku)),
            pl.BlockSpec((1, s, 256), lambda bi, i: (bi, 0, vu)),
            _const_spec(bias_tiles.shape),
        ],
        out_specs=pl.BlockSpec((1, LOCAL_TILES * NA_TQ, 256), lambda bi, i: (bi, i, 0)),
        out_shape=jax.ShapeDtypeStruct((b, s, 256), BF16),
        compiler_params=_params(2),
        name="na_attn",
    )(zr3, zr3, zr3, bias_tiles)


def _merge_kernel(x_ref, g_ref, oa_ref, ob0_ref, ob1_ref, ob2_ref, l0_ref, l1_ref, l2_ref, oc_ref, od_ref,
                  wg_ref, wb_ref, wo_ref, y_ref, scr_ref):
    x = x_ref[...]
    hb = _rms(x, g_ref[...]).astype(BF16)

    n_scr = [0]

    def token_order(ref):
        dil = ref.shape[1]
        if dil == 1:
            return ref[0, 0].astype(F32)
        slot = n_scr[0]
        n_scr[0] += 2
        for c in range(2):
            for r in range(dil):
                scr_ref[slot + c, pl.ds(r, TM // dil, stride=dil), :] = (
                    ref[0, r, :, c * LANES:(c + 1) * LANES].astype(F32))
        return jnp.concatenate([scr_ref[slot], scr_ref[slot + 1]], axis=1)

    lses = [token_order(r) for r in (l0_ref, l1_ref, l2_ref)]
    mx = jnp.maximum(jnp.maximum(lses[0], lses[1]), lses[2])
    ws = [jnp.exp(l - mx) for l in lses]
    den = ws[0] + ws[1] + ws[2]
    obs = [token_order(r) for r in (ob0_ref, ob1_ref, ob2_ref)]
    ob = (ws[0] / den) * obs[0]
    for g in range(1, 3):
        ob = ob + (ws[g] / den) * obs[g]
    branches = (oa_ref[...], ob.astype(BF16), oc_ref[...], od_ref[...])
    merged = None
    for i, o in enumerate(branches):
        gate = jax.nn.sigmoid(_dot(hb, wg_ref[i]))
        term = gate * _dot(o, wb_ref[i])
        merged = term if merged is None else merged + term
    y_ref[...] = x + _dot(merged.astype(BF16), wo_ref[...])


def _merge(x2d, s, g_mix, oa, obs, lses, oc, od, w_gate, w_branch, w_out):
    t = x2d.shape[0]
    n_s = s // TM
    tok = lambda w: pl.BlockSpec((TM, w), lambda i: (i, 0))
    res = lambda a: pl.BlockSpec((1, a.shape[1], TM // a.shape[1], 256), lambda i: (i // n_s, 0, i % n_s, 0))
    n_slots = 2 * sum(a.shape[1] > 1 for a in list(obs) + list(lses))
    return pl.pallas_call(
        _merge_kernel,
        grid=(t // TM,),
        in_specs=[tok(D_MODEL), _const_spec((1, D_MODEL)), tok(256)] + [res(a) for a in obs] + [res(a) for a in lses] + [
            tok(256), tok(256),
            _const_spec(w_gate.shape), _const_spec(w_branch.shape), _const_spec(w_out.shape)],
        out_specs=tok(D_MODEL),
        out_shape=jax.ShapeDtypeStruct((t, D_MODEL), F32),
        scratch_shapes=[pltpu.VMEM((n_slots, TM, LANES), F32)],
        compiler_params=_params(1),
        name="merge",
    )(x2d, g_mix, oa, obs[0], obs[1], obs[2], lses[0], lses[1], lses[2], oc, od, w_gate, w_branch, w_out)


def _memkv_kernel(m_ref, g_ref, wk_ref, wv_ref, k_ref, v_ref):
    mb = _rms(m_ref[0], g_ref[...]).astype(BF16)
    k_ref[0] = _dot(mb, wk_ref[...]).astype(BF16)
    v_ref[0] = _dot(mb, wv_ref[...]).astype(BF16)


def _mem_kv(mem, g_mem, w_ck, w_cv):
    b, m_len, d = mem.shape
    blk = pl.BlockSpec((1, m_len, d), lambda bi: (bi, 0, 0))
    return pl.pallas_call(
        _memkv_kernel,
        grid=(b,),
        in_specs=[blk, _const_spec((1, d)), _const_spec((d, d)), _const_spec((d, d))],
        out_specs=[blk, blk],
        out_shape=[jax.ShapeDtypeStruct((b, m_len, d), BF16)] * 2,
        compiler_params=_params(1),
        name="mem_kv",
    )(mem, g_mem, w_ck, w_cv)


def _cross_kernel(x_ref, g_ref, wq_ref, k_ref, v_ref, wo_ref, gf_ref, wr_ref, y_ref, h_ref, aff_ref):
    x = x_ref[...]
    hb = _rms(x, g_ref[...]).astype(BF16)
    q = (_dot(hb, wq_ref[...]) * MEM_SCALE).astype(BF16)
    heads = [slice(h * MEM_HEAD_DIM, (h + 1) * MEM_HEAD_DIM) for h in range(MEM_HEADS)]
    scores = [_dot_nt(q[:, sl], k_ref[0, :, sl]) for sl in heads]
    probs = []
    for s in scores:
        m = jnp.max(s, axis=-1, keepdims=True)
        e = jnp.exp(s - m)
        probs.append((e / jnp.sum(e, axis=-1, keepdims=True)).astype(BF16))
    outs = [_dot(p, v_ref[0, :, sl]).astype(BF16) for p, sl in zip(probs, heads)]
    y = x + _dot(jnp.concatenate(outs, axis=1), wo_ref[...])
    y_ref[...] = y
    h3 = _rms(y, gf_ref[...]).astype(BF16)
    h_ref[...] = h3
    logits = _dot_nt(wr_ref[...], h3)
    m = jnp.max(logits, axis=0, keepdims=True)
    e = jnp.exp(logits - m)
    aff_ref[...] = e / jnp.sum(e, axis=0, keepdims=True)


def _cross(x2d, b, s, g_cross, w_cq, k_mem, v_mem, w_co, g_ffn, w_router_t):
    t = b * s
    n_s = s // TM
    m_len = k_mem.shape[1]
    tok = lambda: pl.BlockSpec((TM, D_MODEL), lambda i: (i, 0))
    mem_blk = pl.BlockSpec((1, m_len, D_MODEL), lambda i: (i // n_s, 0, 0))
    return pl.pallas_call(
        _cross_kernel,
        grid=(t // TM,),
        in_specs=[tok(), _const_spec((1, D_MODEL)), _const_spec((D_MODEL, D_MODEL)), mem_blk, mem_blk,
                  _const_spec((D_MODEL, D_MODEL)), _const_spec((1, D_MODEL)),
                  _const_spec((N_EXPERTS, D_MODEL))],
        out_specs=[tok(), tok(), pl.BlockSpec((N_EXPERTS, TM), lambda i: (0, i))],
        out_shape=[jax.ShapeDtypeStruct((t, D_MODEL), F32), jax.ShapeDtypeStruct((t, D_MODEL), BF16),
                   jax.ShapeDtypeStruct((N_EXPERTS, t), F32)],
        compiler_params=_params(1),
        name="cross",
    )(x2d, g_cross, w_cq, k_mem, v_mem, w_co, g_ffn, w_router_t)


def _select_kernel(aff_ref, pos_ref, *, n_tok, cap):
    n_cnt = n_tok // CNT_CH

    def count_ge(cand):
        def body(c, acc):
            bits = pltpu.bitcast(aff_ref[:, pl.ds(pl.multiple_of(c * CNT_CH, CNT_CH), CNT_CH)], I32)
            return acc + jnp.sum(jnp.where(bits >= cand, 1.0, 0.0), axis=1, keepdims=True)
        return lax.fori_loop(0, n_cnt, body, jnp.zeros((N_EXPERTS, 1), F32))

    def bit_body(i, prefix):
        cand = prefix | lax.shift_left(jnp.int32(1), 30 - i)
        return jnp.where(count_ge(cand) >= cap, cand, prefix)

    thr = lax.fori_loop(0, 31, bit_body, jnp.zeros((N_EXPERTS, 1), I32))
    need_eq = cap - count_ge(thr + 1)

    tri_r = lax.broadcasted_iota(I32, (CH, CH), 0)
    tri_c = lax.broadcasted_iota(I32, (CH, CH), 1)
    tri = jnp.where(tri_r <= tri_c, 1.0, 0.0).astype(BF16)

    def chunk_body(c, carry):
        c_eq, c_row = carry
        sl = pl.ds(pl.multiple_of(c * CH, CH), CH)
        bits = pltpu.bitcast(aff_ref[:, sl], I32)
        eq = jnp.where(bits == thr, 1.0, 0.0)
        gt = jnp.where(bits > thr, 1.0, 0.0)
        eq_before = c_eq + _dot(eq.astype(BF16), tri) - eq
        sel = gt + eq * jnp.where(eq_before < need_eq, 1.0, 0.0)
        row = c_row + _dot(sel.astype(BF16), tri) - sel
        pos_ref[:, sl] = jnp.where(sel > 0.5, row, -1.0).astype(I32)
        taken = jnp.sum(sel, axis=1, keepdims=True)
        c_row = c_row + jnp.ceil(taken * (1.0 / ROW_ALIGN)) * ROW_ALIGN
        return (c_eq + jnp.sum(eq, axis=1, keepdims=True), c_row)

    zero = jnp.zeros((N_EXPERTS, 1), F32)
    lax.fori_loop(0, n_tok // CH, chunk_body, (zero, zero))


def _select(aff_t, cap):
    n_tok = aff_t.shape[1]
    return pl.pallas_call(
        functools.partial(_select_kernel, n_tok=n_tok, cap=cap),
        grid=(1,),
        in_specs=[_const_spec(aff_t.shape)],
        out_specs=_const_spec(aff_t.shape),
        out_shape=jax.ShapeDtypeStruct(aff_t.shape, I32),
        compiler_params=_params(1),
        name="select",
    )(aff_t)


def _window_hits(pos_ref, base_ref, e, k, n_k, p):
    rows = base_ref[e * n_k + k] + p * WIN + lax.broadcasted_iota(I32, (WIN, CH), 0)
    return pos_ref[e:e + 1, :] == rows


def _window_to_hbm(scr, hbm, sem, e, row0):
    return pltpu.make_async_copy(scr.at[pl.ds(e * WIN, WIN)], hbm.at[e, pl.ds(row0, WIN)], sem.at[e])


def _window_from_hbm(hbm, scr, sem, e, row0):
    return pltpu.make_async_copy(hbm.at[e, pl.ds(row0, WIN)], scr.at[pl.ds(e * WIN, WIN)], sem.at[e])


def _gather_kernel(base_ref, npass_ref, h_ref, pos_ref, aff_ref, xe_in, xe_ref, x_scr, sem_x, *, n_k):
    del xe_in
    k = pl.program_id(0)
    n_pass = npass_ref[k]
    xs = x_scr.at[k % 2]
    gate_lane = lax.broadcasted_iota(I32, (WIN, LANES), 1)

    def window_copies(kk, p):
        copies = []
        for e in range(N_EXPERTS):
            row0 = pl.multiple_of(base_ref[e * n_k + kk] + p * WIN, ROW_ALIGN)
            copies.append(_window_to_hbm(x_scr.at[kk % 2], xe_ref, sem_x, e, row0))
        return copies

    def one_pass(p, carry):
        for e0 in range(0, N_EXPERTS, GATHER_GROUP):
            hits = [_window_hits(pos_ref, base_ref, e, k, n_k, p) for e in range(e0, e0 + GATHER_GROUP)]
            onehot = jnp.concatenate([jnp.where(h, 1.0, 0.0).astype(BF16) for h in hits], axis=0)
            xs[e0 * WIN:(e0 + GATHER_GROUP) * WIN, 0:D_MODEL] = _dot(onehot, h_ref[...]).astype(BF16)
            for i, h in enumerate(hits):
                e = e0 + i
                gate = jnp.sum(jnp.where(h, aff_ref[e:e + 1, :], 0.0), axis=1, keepdims=True)
                hi = gate.astype(BF16).astype(F32)
                parts = jnp.where(gate_lane < LANES // 2, hi, gate - hi)
                xs[e * WIN:(e + 1) * WIN, D_MODEL:XE_COLS] = parts.astype(BF16)

        @pl.when((p == 0) & (k > 0))
        def _():
            for c in window_copies(k - 1, npass_ref[k - 1] - 1):
                c.wait()

        for c in window_copies(k, p):
            c.start()

        @pl.when(p < n_pass - 1)
        def _():
            for c in window_copies(k, p):
                c.wait()

        return carry

    lax.fori_loop(0, n_pass, one_pass, 0)

    @pl.when(k == n_k - 1)
    def _():
        for c in window_copies(k, n_pass - 1):
            c.wait()


def _expert_gather(base, npass, h3, pos, aff_t, rows_pad):
    n_tok = h3.shape[0]
    n_k = n_tok // CH
    xe0 = jnp.zeros((N_EXPERTS, rows_pad, XE_COLS), BF16)
    grid_spec = pltpu.PrefetchScalarGridSpec(
        num_scalar_prefetch=2,
        grid=(n_k,),
        in_specs=[
            pl.BlockSpec((CH, D_MODEL), lambda k, *_: (k, 0)),
            pl.BlockSpec((N_EXPERTS, CH), lambda k, *_: (0, k)),
            pl.BlockSpec((N_EXPERTS, CH), lambda k, *_: (0, k)),
            pl.BlockSpec(memory_space=pl.ANY),
        ],
        out_specs=pl.BlockSpec(memory_space=pl.ANY),
        scratch_shapes=[pltpu.VMEM((2, N_EXPERTS * WIN, XE_COLS), BF16), pltpu.SemaphoreType.DMA((N_EXPERTS,))],
    )
    return pl.pallas_call(
        functools.partial(_gather_kernel, n_k=n_k),
        grid_spec=grid_spec,
        out_shape=jax.ShapeDtypeStruct(xe0.shape, BF16),
        input_output_aliases={5: 0},
        compiler_params=_params(1),
        name="expert_gather",
    )(base, npass, h3, pos, aff_t, xe0)


def _ffn_kernel(end_ref, xe_ref, wg_ref, wu_ref, wd_ref, ye_ref):
    e, j = pl.program_id(0), pl.program_id(1)

    @pl.when(j * RB < end_ref[e])
    def _():
        xe = xe_ref[0, :, 0:D_MODEL]
        parts = xe_ref[0, :, D_MODEL:XE_COLS].astype(F32)
        gate = parts[:, 0:1] + parts[:, LANES // 2:LANES // 2 + 1]
        hid = jax.nn.silu(_dot(xe, wg_ref[0])) * _dot(xe, wu_ref[0])
        ye_ref[0] = (_dot(hid.astype(BF16), wd_ref[0]) * gate).astype(BF16)

    @pl.when(j * RB >= end_ref[e])
    def _():
        ye_ref[0] = jnp.zeros((RB, D_MODEL), BF16)


def _expert_ffn(end, xe, w_e_gate, w_e_up, w_e_down):
    rows_pad = xe.shape[1]
    grid_spec = pltpu.PrefetchScalarGridSpec(
        num_scalar_prefetch=1,
        grid=(N_EXPERTS, rows_pad // RB),
        in_specs=[
            pl.BlockSpec((1, RB, XE_COLS), lambda e, j, *_: (e, j, 0)),
            pl.BlockSpec((1, D_MODEL, EXPERT_FF), lambda e, j, *_: (e, 0, 0)),
            pl.BlockSpec((1, D_MODEL, EXPERT_FF), lambda e, j, *_: (e, 0, 0)),
            pl.BlockSpec((1, EXPERT_FF, D_MODEL), lambda e, j, *_: (e, 0, 0)),
        ],
        out_specs=pl.BlockSpec((1, RB, D_MODEL), lambda e, j, *_: (e, j, 0)),
    )
    return pl.pallas_call(
        _ffn_kernel,
        grid_spec=grid_spec,
        out_shape=jax.ShapeDtypeStruct(xe.shape[:2] + (D_MODEL,), BF16),
        compiler_params=_params(2),
        name="expert_ffn",
    )(end, xe, w_e_gate, w_e_up, w_e_down)


def _scatter_kernel(base_ref, npass_ref, x_ref, pos_ref, ye_ref, gfin_ref, y_ref, stage, sem,
                    *, final_norm, n_k):
    k = pl.program_id(0)
    slot = k % 2
    y_ref[...] = x_ref[...]

    def window_copies(kk, p, s):
        copies = []
        for e in range(N_EXPERTS):
            row0 = pl.multiple_of(base_ref[e * n_k + kk] + p * WIN, ROW_ALIGN)
            copies.append(_window_from_hbm(ye_ref, stage.at[s], sem.at[s], e, row0))
        return copies

    def onehot(p):
        return jnp.concatenate(
            [jnp.where(_window_hits(pos_ref, base_ref, e, k, n_k, p), 1.0, 0.0).astype(BF16)
             for e in range(N_EXPERTS)], axis=0)

    @pl.when(k == 0)
    def _():
        for c in window_copies(0, 0, 0):
            c.start()

    @pl.when(k + 1 < n_k)
    def _():
        for c in window_copies(k + 1, 0, 1 - slot):
            c.start()

    hot = onehot(0)
    for c in window_copies(k, 0, slot):
        c.wait()
    y_ref[...] += _dot_tn(hot, stage[slot])

    def later_pass(p, carry):
        copies = window_copies(k, p, slot)
        for c in copies:
            c.start()
        hot = onehot(p)
        for c in copies:
            c.wait()
        y_ref[...] += _dot_tn(hot, stage[slot])
        return carry

    lax.fori_loop(1, npass_ref[k], later_pass, 0)
    if final_norm:
        y_ref[...] = _rms(y_ref[...], gfin_ref[...])


def _expert_scatter(base, npass, x2d, pos, ye, g_final, final_norm):
    t = x2d.shape[0]
    n_k = t // CH
    grid_spec = pltpu.PrefetchScalarGridSpec(
        num_scalar_prefetch=2,
        grid=(n_k,),
        in_specs=[
            pl.BlockSpec((CH, D_MODEL), lambda k, *_: (k, 0)),
            pl.BlockSpec((N_EXPERTS, CH), lambda k, *_: (0, k)),
            pl.BlockSpec(memory_space=pl.ANY),
            pl.BlockSpec((1, D_MODEL), lambda k, *_: (0, 0)),
        ],
        out_specs=pl.BlockSpec((CH, D_MODEL), lambda k, *_: (k, 0)),
        scratch_shapes=[pltpu.VMEM((2, N_EXPERTS * WIN, D_MODEL), BF16), pltpu.SemaphoreType.DMA((2, N_EXPERTS))],
    )
    return pl.pallas_call(
        functools.partial(_scatter_kernel, final_norm=final_norm, n_k=n_k),
        grid_spec=grid_spec,
        out_shape=jax.ShapeDtypeStruct((t, D_MODEL), F32),
        compiler_params=_params(1),
        name="expert_scatter",
    )(base, npass, x2d, pos, ye, g_final)


def _segments(pos):
    n_e, n_tok = pos.shape
    n_k = n_tok // CH
    cnt = jnp.sum((pos >= 0).reshape(n_e, n_k, CH), axis=-1, dtype=I32)
    padded = (cnt + (ROW_ALIGN - 1)) // ROW_ALIGN * ROW_ALIGN
    end = jnp.cumsum(padded, axis=1)
    base = end - padded
    npass = jnp.maximum(jnp.max((cnt + (WIN - 1)) // WIN, axis=0), 1)
    return base.reshape(-1).astype(I32), npass.astype(I32), end[:, -1].astype(I32)


def _rows_pad(n_tok, cap):
    rows = cap + ROW_ALIGN * (n_tok // CH) + CH + WIN
    return (rows + RB - 1) // RB * RB


def _rope_tables(s):
    pos = jnp.arange(s, dtype=F32)[:, None]
    lane = jnp.arange(LANES)
    tabs = []
    for period, base, rot in ((LANES, MLA_NOPE, MLA_ROPE), (HEAD_DIM, 0, HEAD_DIM // 4), (DIFF_QK, 0, DIFF_QK // 4)):
        half = rot // 2
        rel = lane % period - base
        inv_freq = ROPE_THETA ** (-jnp.arange(half, dtype=F32) * (2.0 / rot))
        ang = pos * inv_freq[None, :]
        cos, sin = jnp.cos(ang), jnp.sin(ang)
        idx = jnp.clip(jnp.where(rel >= half, rel - half, rel), 0, half - 1)
        first = (rel >= 0) & (rel < half)
        second = (rel >= half) & (rel < rot)
        tabs.append(jnp.where(first | second, cos[:, idx], 1.0))
        tabs.append(jnp.where(first, -sin[:, idx], 0.0))
        tabs.append(jnp.where(second, sin[:, idx], 0.0))
    return jnp.stack(tabs).astype(F32)


def _prep_layer(l, w_in, g_mix, g_cq, w_uq, g_ckv, w_ukv, lam_q1, lam_k1, lam_q2, lam_k2, g_diff, na_bias,
                w_gate, w_branch, w_out, g_cross, g_mem, w_cq, w_ck, w_cv, w_co, g_ffn, w_router,
                w_e_gate, w_e_up, w_e_down):
    d = D_MODEL
    wi = w_in[l]
    kpe0 = MLA_Q_RANK + MLA_KV_RANK
    z64 = jnp.zeros((d, MLA_NOPE), F32)
    z32 = jnp.zeros((d, LANES - MLA_NOPE - MLA_ROPE), F32)
    rest = wi[:, kpe0 + MLA_ROPE:]
    n_g = len(DIL_GROUPS)
    n_dil = n_g * Z_DIL_GROUP
    vc0 = n_dil + 512
    dil_cols = rest[:, :n_dil].reshape(d, 3, n_g, 256).transpose(0, 2, 1, 3).reshape(d, n_dil)
    w_z = jnp.concatenate([wi[:, :kpe0], z64, wi[:, kpe0:kpe0 + MLA_ROPE], z32, dil_cols[:, :Z_DIL_GROUP],
                           rest[:, n_dil:vc0], rest[:, vc0 + 256:], dil_cols[:, Z_DIL_GROUP:]], axis=1)
    w_vct = rest[:, vc0:vc0 + 256].T
    qh = MLA_NOPE + MLA_ROPE
    wuq = w_uq[l].reshape(MLA_Q_RANK, MLA_HEADS, qh)
    wuq = jnp.pad(wuq, ((0, 0), (0, 0), (0, LANES - qh))).reshape(MLA_Q_RANK, MLA_HEADS * LANES)
    wukv = w_ukv[l].reshape(MLA_KV_RANK, MLA_HEADS, MLA_NOPE + HEAD_DIM)
    wk = jnp.pad(wukv[:, :, :MLA_NOPE], ((0, 0), (0, 0), (0, LANES - MLA_NOPE))).reshape(MLA_KV_RANK, MLA_HEADS * LANES)
    wv = wukv[:, :, MLA_NOPE:].reshape(MLA_KV_RANK, MLA_HEADS * HEAD_DIM)
    return dict(
        g_mix=g_mix[l][None], w_z=w_z.astype(BF16), g_cq=g_cq[l][None], g_ckv=g_ckv[l][None],
        w_uq=wuq.astype(BF16), w_k=wk.astype(BF16), w_vt=wv.T.astype(BF16), w_vct=w_vct.astype(BF16),
        lam=jnp.stack([lam_q1[l], lam_k1[l], lam_q2[l], lam_k2[l]]).astype(F32),
        g_diff=jnp.tile(g_diff[l], 2)[None], na_bias=na_bias[l],
        w_gate=w_gate[l].astype(BF16), w_branch=w_branch[l].astype(BF16), w_out=w_out[l].astype(BF16),
        g_cross=g_cross[l][None], g_mem=g_mem[l][None], w_cq=w_cq[l].astype(BF16), w_ck=w_ck[l].astype(BF16),
        w_cv=w_cv[l].astype(BF16), w_co=w_co[l].astype(BF16), g_ffn=g_ffn[l][None],
        w_router_t=w_router[l].T.astype(BF16), w_e_gate=w_e_gate[l].astype(BF16),
        w_e_up=w_e_up[l].astype(BF16), w_e_down=w_e_down[l].astype(BF16),
    )


def _trunk(x, mem, layers, bias_tiles, g_final):
    b, s, d = x.shape
    t = b * s
    cap = CAPACITY_FACTOR * t // N_EXPERTS
    assert s % TM == 0 and s % TQ == 0 and s % TK == 0 and t % CH == 0 and cap % RB == 0
    rope_tab = _rope_tables(s)
    x2d = x.reshape(t, d)
    for l, p in enumerate(layers):
        zr, qa, ka, vat, vct, d1, d2 = _project(x2d, b, s, p["g_mix"], p["w_z"], rope_tab, p["g_cq"], p["g_ckv"],
                                                p["w_uq"], p["w_k"], p["w_vt"], p["w_vct"])
        zr3 = zr.reshape(b, s, ZR_COLS)
        oa = _mla_attention(qa, ka, vat).reshape(t, 256)
        dil = [_dil_attention(zr.reshape(b, 1, s, ZR_COLS), (Z_DIL0 - Z_HEAD) // 256),
               _dil_attention(d1, 0), _dil_attention(d2, 0)]
        lam_init = 0.8 - 0.6 * math.exp(-0.3 * l)
        oc = _diff_attention(zr3, vct, p["lam"], p["g_diff"], lam_init).reshape(t, 256)
        od = _na_attention(zr3, bias_tiles[l]).reshape(t, 256)
        x2d = _merge(x2d, s, p["g_mix"], oa, [o for o, _ in dil], [ls for _, ls in dil], oc, od,
                     p["w_gate"], p["w_branch"], p["w_out"])
        k_mem, v_mem = _mem_kv(mem, p["g_mem"], p["w_ck"], p["w_cv"])
        x2d, h3, aff_t = _cross(x2d, b, s, p["g_cross"], p["w_cq"], k_mem, v_mem, p["w_co"], p["g_ffn"],
                                p["w_router_t"])
        pos = _select(aff_t, cap)
        base, npass, end = _segments(pos)
        xe = _expert_gather(base, npass, h3, pos, aff_t, _rows_pad(t, cap))
        ye = _expert_ffn(end, xe, p["w_e_gate"], p["w_e_up"], p["w_e_down"])
        x2d = _expert_scatter(base, npass, x2d, pos, ye, g_final[None], final_norm=(l == len(layers) - 1))
    return x2d.reshape(b, s, d)


def kernel(x_prompt, x_sample, mem_prompt, mem_sample, w_in, g_mix, g_cq, w_uq, g_ckv, w_ukv, lam_q1, lam_k1, lam_q2, lam_k2, g_diff, na_bias, w_gate, w_branch, w_out, g_cross, g_mem, w_cq, w_ck, w_cv, w_co, g_ffn, w_router, w_e_gate, w_e_up, w_e_down, g_final):
    depth = w_in.shape[0]
    layers = [_prep_layer(l, w_in, g_mix, g_cq, w_uq, g_ckv, w_ukv, lam_q1, lam_k1, lam_q2, lam_k2, g_diff,
                          na_bias, w_gate, w_branch, w_out, g_cross, g_mem, w_cq, w_ck, w_cv, w_co, g_ffn,
                          w_router, w_e_gate, w_e_up, w_e_down) for l in range(depth)]
    bias_tiles = [_na_bias_tiles(p["na_bias"]) for p in layers]
    y_prompt = _trunk(x_prompt, mem_prompt, layers, bias_tiles, g_final)
    y_sample = _trunk(x_sample, mem_sample, layers, bias_tiles, g_final)
    return (y_prompt, y_sample)
```
